```python
import jax, jax.numpy as jnp
from jax import lax
import numpy as np

D_MODEL = 1024
BATCH = 16
SEQ = 256
DEPTH = 2
DEC_BATCH = 4
DEC_SEQ = 4096
PAST_LEN = 256

GRID_W = 64
N_MIXERS = 2
N_FNET = (DEPTH + 1) // 2
N_HGRN = DEPTH // 2
FNET_GROUPS = 4
FNET_GROUP_DIM = D_MODEL // FNET_GROUPS
HGRN_EXPAND = 128
HGRN_HEADS = D_MODEL // HGRN_EXPAND
HGRN_DK = HGRN_EXPAND
HGRN_DV = D_MODEL // HGRN_HEADS
CHUNK = 32
D_FF = 4 * D_MODEL
N_MOD = 6
EPS = 1e-6
POS_BASE = 10000.0

kernel_name = "hybrid_fnet_hgrn2_diffusion_step"


def rmsnorm(x, g):
    xf = x.astype(jnp.float32)
    r = xf * lax.rsqrt(jnp.mean(xf * xf, axis=-1, keepdims=True) + EPS)
    return (r * g.astype(jnp.float32)).astype(x.dtype)


def grid_pos_embed(n_tok, d, dtype):
    rows = n_tok // GRID_W
    quarter = d // 4
    omega = 1.0 / (POS_BASE ** (jnp.arange(quarter, dtype=jnp.float32) / quarter))
    r = jnp.arange(rows, dtype=jnp.float32)[:, None] * omega[None, :]
    cl = jnp.arange(GRID_W, dtype=jnp.float32)[:, None] * omega[None, :]
    er = jnp.concatenate([jnp.sin(r), jnp.cos(r)], axis=-1)
    ec = jnp.concatenate([jnp.sin(cl), jnp.cos(cl)], axis=-1)
    emb = jnp.concatenate([jnp.broadcast_to(er[:, None, :], (rows, GRID_W, d // 2)),
                           jnp.broadcast_to(ec[None, :, :], (rows, GRID_W, d // 2))], axis=-1)
    return emb.reshape(n_tok, d).astype(dtype)


def fourier_mixer(h, w_o):
    b, l, d = h.shape
    hg = h.astype(jnp.float32).reshape(b, l, FNET_GROUPS, FNET_GROUP_DIM)
    mixed = jnp.fft.fft2(hg, axes=(1, 3), norm="ortho").real
    return mixed.reshape(b, l, d).astype(h.dtype) @ w_o


def gated_scan(q, k, v, logf, s0):
    dr, b, hh, l, dk = q.shape
    dv = v.shape[-1]
    n = l // CHUNK

    def to_chunks(a):
        return jnp.moveaxis(a.reshape(dr, b, hh, n, CHUNK, a.shape[-1]), 3, 0)

    tril = jnp.tril(jnp.ones((CHUNK, CHUNK), dtype=bool))

    def step(s, inp):
        qc, kc, vc, gc = inp
        bcum = jnp.cumsum(gc, axis=-2)
        o_inter = jnp.einsum('dbhtk,dbhkv->dbhtv', qc * jnp.exp(bcum), s)
        diff = bcum[..., :, None, :] - bcum[..., None, :, :]
        decay = jnp.exp(jnp.where(tril[:, :, None], diff, -jnp.inf))
        scores = jnp.einsum('dbhtk,dbhsk,dbhtsk->dbhts', qc, kc, decay)
        o_intra = jnp.einsum('dbhts,dbhsv->dbhtv', scores, vc)
        blast = bcum[..., -1:, :]
        s_new = jnp.exp(blast[..., 0, :])[..., None] * s + jnp.einsum(
            'dbhsk,dbhsv->dbhkv', kc * jnp.exp(blast - bcum), vc)
        return s_new, o_inter + o_intra

    s_fin, o = lax.scan(step, s0, (to_chunks(q), to_chunks(k), to_chunks(v), to_chunks(logf)))
    o = jnp.moveaxis(o, 0, 3).reshape(dr, b, hh, l, dv)
    return s_fin, o


def hgrn2_mixer(h, w_in, lb, g_norm, w_o, s0):
    b, l, d = h.shape
    proj = h @ w_in
    q, f_fw, f_bw, i_in, g = jnp.split(proj, 5, axis=-1)

    def heads(a, hd):
        return a.astype(jnp.float32).reshape(b, l, HGRN_HEADS, hd).transpose(0, 2, 1, 3)

    q = jax.nn.silu(heads(q, HGRN_DK))
    v = heads(i_in, HGRN_DV)
    lbh = lb.reshape(2, 1, HGRN_HEADS, 1, HGRN_DK)
    z = jnp.stack([heads(f_fw, HGRN_DK), heads(f_bw, HGRN_DK)])
    f = lbh + (1.0 - lbh) * jax.nn.sigmoid(z)
    k = 1.0 - f
    logf = jnp.log(f)
    flip = lambda a: jnp.flip(a, axis=2)
    q_s = jnp.stack([q, flip(q)])
    v_s = jnp.stack([v, flip(v)])
    k_s = jnp.stack([k[0], flip(k[1])])
    g_s = jnp.stack([logf[0], flip(logf[1])])
    s_fin, o = gated_scan(q_s, k_s, v_s, g_s, s0.astype(jnp.float32))
    o = o[0] + flip(o[1])
    o = o * lax.rsqrt(jnp.mean(o * o, axis=-1, keepdims=True) + EPS) * g_norm.astype(jnp.float32)
    o = o.transpose(0, 2, 1, 3).reshape(b, l, d) * jax.nn.silu(g.astype(jnp.float32))
    return o.astype(h.dtype) @ w_o, s_fin


def trunk(x, cond, states_in, ada_w, ada_b, norm_mix, norm_mlp, fnet_wo, hgrn_w_in, lb_all,
          hgrn_norm, hgrn_wo, mlp_w1, mlp_w2, norm_final):
    states_out = []
    sc = jax.nn.silu(cond)
    for i in range(DEPTH):
        mod = (sc @ ada_w[i] + ada_b[i]).reshape(-1, 1, N_MOD * D_MODEL)
        sh_m, sc_m, gt_m, sh_f, sc_f, gt_f = jnp.split(mod, N_MOD, axis=-1)
        h = rmsnorm(x, norm_mix[i]) * (1.0 + sc_m) + sh_m
        j = i // N_MIXERS
        if i % N_MIXERS == 0:
            y = fourier_mixer(h, fnet_wo[j])
        else:
            y, s_fin = hgrn2_mixer(h, hgrn_w_in[j], lb_all[:, i], hgrn_norm[j], hgrn_wo[j], states_in[j])
            states_out.append(s_fin)
        x = x + gt_m * y
        h = rmsnorm(x, norm_mlp[i]) * (1.0 + sc_f) + sh_f
        x = x + gt_f * (jnp.square(jax.nn.relu(h @ mlp_w1[i])) @ mlp_w2[i])
    return rmsnorm(x, norm_final), states_out


def setup_inputs(seed: int = 0) -> dict:
    key = jax.random.key(seed)
    ks = jax.random.split(key, 20)
    f32 = jnp.float32
    nrm = lambda k, s, sc: jax.random.normal(k, s, f32) * sc
    return {
        "x_prompt": nrm(ks[0], (BATCH, SEQ, D_MODEL), 1.0),
        "x_sample": nrm(ks[1], (DEC_BATCH, DEC_SEQ, D_MODEL), 1.0),
        "state_hgrn": nrm(ks[2], (DEC_BATCH, N_HGRN, 2, HGRN_HEADS, HGRN_DK, HGRN_DV), 0.5),
        "c": nrm(ks[3], (DEC_BATCH, D_MODEL), 1.0),
        "c_ctx": nrm(ks[4], (D_MODEL,), 1.0),
        "ada_w": nrm(ks[5], (DEPTH, D_MODEL, N_MOD * D_MODEL), 0.5 * D_MODEL ** -0.5),
        "ada_b": nrm(ks[6], (DEPTH, N_MOD * D_MODEL), 0.02),
        "norm_mix": 1.0 + nrm(ks[7], (DEPTH, D_MODEL), 0.05),
        "norm_mlp": 1.0 + nrm(ks[8], (DEPTH, D_MODEL), 0.05),
        "fnet_wo": nrm(ks[9], (N_FNET, D_MODEL, D_MODEL), D_MODEL ** -0.5),
        "hgrn_w_in": nrm(ks[10], (N_HGRN, D_MODEL, 5 * D_MODEL), D_MODEL ** -0.5),
        "hgrn_lb": nrm(ks[11], (2, DEPTH, D_MODEL), 1.0),
        "hgrn_norm": 1.0 + nrm(ks[12], (N_HGRN, HGRN_DV), 0.05),
        "hgrn_wo": nrm(ks[13], (N_HGRN, D_MODEL, D_MODEL), D_MODEL ** -0.5),
        "mlp_w1": nrm(ks[14], (DEPTH, D_MODEL, D_FF), D_MODEL ** -0.5),
        "mlp_w2": nrm(ks[15], (DEPTH, D_FF, D_MODEL), D_FF ** -0.5),
        "norm_final": 1.0 + nrm(ks[16], (D_MODEL,), 0.05),
    }


def reference(x_prompt, x_sample, state_hgrn, c, c_ctx, ada_w, ada_b, norm_mix, norm_mlp, fnet_wo,
              hgrn_w_in, hgrn_lb, hgrn_norm, hgrn_wo, mlp_w1, mlp_w2, norm_final):
    sm = jax.nn.softmax(hgrn_lb.astype(jnp.float32), axis=1)
    lb_all = jnp.cumsum(sm, axis=1) - sm[:, :1]
    weights = (ada_w, ada_b, norm_mix, norm_mlp, fnet_wo, hgrn_w_in, lb_all, hgrn_norm, hgrn_wo,
               mlp_w1, mlp_w2, norm_final)

    bp = x_prompt.shape[0]
    zero_states = [jnp.zeros((2, bp, HGRN_HEADS, HGRN_DK, HGRN_DV), jnp.float32) for _ in range(N_HGRN)]
    y_prompt, ctx_states = trunk(x_prompt, c_ctx, zero_states, *weights)
    new_state_hgrn = jnp.stack([s.transpose(1, 0, 2, 3, 4) for s in ctx_states], axis=1)

    xs = x_sample + grid_pos_embed(x_sample.shape[1], D_MODEL, x_sample.dtype)[None]
    cached = [state_hgrn[:, j].transpose(1, 0, 2, 3, 4) for j in range(N_HGRN)]
    y_sample, _ = trunk(xs, c, cached, *weights)
    return (y_prompt, y_sample, new_state_hgrn)
```

```python
import functools
import math

import numpy as np
import jax
import jax.numpy as jnp
from jax import lax
from jax.experimental import pallas as pl
from jax.experimental.pallas import tpu as pltpu

F32 = jnp.float32
BF16 = jnp.bfloat16

D = 1024
N_MOD = 6
D_FF = 4 * D
EPS = 1e-6
GROUPS = 4
GD = D // GROUPS
HEADS = 8
DK = 128
GRID_W = 64
POS_BASE = 10000.0
PROJ = 5

SEQ_SLABS = 16
SEQ_PTS = 256
CHUNK = 32
EXP_CLAMP = 80.0

VMEM_LIMIT = 56 * 1024 * 1024


def _dft_tables():
    n = np.arange(GD)
    ang = 2.0 * np.pi * ((n[:, None] * n[None, :]) % GD) / GD
    cs_ch = np.concatenate([np.cos(ang), np.sin(ang)], axis=1) / math.sqrt(GD)
    t_ctx = np.concatenate([np.cos(ang), -np.sin(ang)], axis=1) / math.sqrt(GD)
    length = SEQ_PTS * SEQ_SLABS
    k1 = np.arange(SEQ_PTS)[:, None]
    n1 = np.arange(SEQ_PTS)[None, :]
    blocks = []
    for n2 in range(SEQ_SLABS):
        a = 2.0 * np.pi * ((k1 * (SEQ_SLABS * n1 + n2)) % length) / length
        gc, gs = np.cos(a) / math.sqrt(length), np.sin(a) / math.sqrt(length)
        blocks.append(np.block([[gc, -gs], [gs, gc]]))
    g_seq = np.stack(blocks)
    return tuple(jnp.asarray(t, F32).astype(BF16) for t in (cs_ch, t_ctx, g_seq))


def _grid_pos_embed(n_tok, d):
    rows = n_tok // GRID_W
    quarter = d // 4
    omega = 1.0 / (POS_BASE ** (jnp.arange(quarter, dtype=F32) / quarter))
    r = jnp.arange(rows, dtype=F32)[:, None] * omega[None, :]
    cl = jnp.arange(GRID_W, dtype=F32)[:, None] * omega[None, :]
    er = jnp.concatenate([jnp.sin(r), jnp.cos(r)], axis=-1)
    ec = jnp.concatenate([jnp.sin(cl), jnp.cos(cl)], axis=-1)
    emb = jnp.concatenate([jnp.broadcast_to(er[:, None, :], (rows, GRID_W, d // 2)),
                           jnp.broadcast_to(ec[None, :, :], (rows, GRID_W, d // 2))], axis=-1)
    return emb.reshape(n_tok, d)


def _silu(x):
    return x * jax.nn.sigmoid(x)


def _rms(x, g):
    return x * lax.rsqrt(jnp.mean(x * x, axis=-1, keepdims=True) + EPS) * g


def _mod_part(m, idx):
    return m[:, idx * D:(idx + 1) * D]


def _dot(a, b):
    return jnp.dot(a, b, preferred_element_type=F32)


def _dot_nt(a, b):
    return lax.dot_general(a, b, (((1,), (1,)), ((), ())), preferred_element_type=F32)


def _dot_tn(a, b):
    return lax.dot_general(a, b, (((0,), (0,)), ((), ())), preferred_element_type=F32)


def _params(*sem):
    return pltpu.CompilerParams(dimension_semantics=sem, vmem_limit_bytes=VMEM_LIMIT)


def _mod_kernel(c_ref, w_ref, b_ref, o_ref):
    s = _silu(c_ref[...])
    o_ref[0] = _dot(s.astype(BF16), w_ref[0].astype(BF16)) + b_ref[0]


def _modulation(cond8, ada_w, ada_b):
    depth = ada_w.shape[0]
    tn = 1536
    out = pl.pallas_call(
        _mod_kernel,
        grid=(depth, N_MOD * D // tn),
        in_specs=[pl.BlockSpec((8, D), lambda l, j: (0, 0)),
                  pl.BlockSpec((1, D, tn), lambda l, j: (l, 0, j)),
                  pl.BlockSpec((1, 1, tn), lambda l, j: (l, 0, j))],
        out_specs=pl.BlockSpec((1, 8, tn), lambda l, j: (l, 0, j)),
        out_shape=jax.ShapeDtypeStruct((depth, 8, N_MOD * D), F32),
        compiler_params=_params("parallel", "parallel"),
    )(cond8, ada_w, ada_b.reshape(depth, 1, N_MOD * D))
    return out.reshape(depth, 8, 1, N_MOD * D)


def _channel_dft(hb, cs_ref, put):
    for g in range(GROUPS):
        y = _dot(hb[:, g * GD:(g + 1) * GD], cs_ref[...])
        put(0, g, y[:, :GD].astype(BF16))
        put(1, g, y[:, GD:].astype(BF16))


def _fnet_ctx_kernel(x_ref, mod_ref, g_ref, cs_ref, t_ref, o_ref, y_scr):
    m = mod_ref[0, 0]
    h = _rms(x_ref[0], g_ref[...]) * (1.0 + _mod_part(m, 1)) + _mod_part(m, 0)
    seq = x_ref.shape[1]

    def put(kind, g, val):
        y_scr[kind * seq:(kind + 1) * seq, g * GD:(g + 1) * GD] = val

    _channel_dft(h.astype(BF16), cs_ref, put)
    o_ref[0] = _dot(t_ref[...], y_scr[...]).astype(o_ref.dtype)


def _fnet_ctx(x, mod, layer, gain, cs_ch, t_ctx):
    b, seq, _ = x.shape
    return pl.pallas_call(
        _fnet_ctx_kernel,
        grid=(b,),
        in_specs=[pl.BlockSpec((1, seq, D), lambda i: (i, 0, 0)),
                  pl.BlockSpec((1, 1, 1, N_MOD * D), lambda i: (layer, 0, 0, 0)),
                  pl.BlockSpec((1, D), lambda i: (0, 0)),
                  pl.BlockSpec((GD, 2 * GD), lambda i: (0, 0)),
                  pl.BlockSpec((seq, 2 * seq), lambda i: (0, 0))],
        out_specs=pl.BlockSpec((1, seq, D), lambda i: (i, 0, 0)),
        out_shape=jax.ShapeDtypeStruct((b, seq, D), BF16),
        scratch_shapes=[pltpu.VMEM((2 * seq, D), BF16)],
        compiler_params=_params("parallel"),
    )(x, mod, gain, cs_ch, t_ctx)


def _fnet_slab_kernel(x_ref, pos_ref, mod_ref, g_ref, cs_ref, y_ref):
    m = mod_ref[0, 0]
    h = _rms(x_ref[0] + pos_ref[...], g_ref[...]) * (1.0 + _mod_part(m, 1)) + _mod_part(m, 0)

    def put(kind, g, val):
        y_ref[0, 0, kind, :, g * GD:(g + 1) * GD] = val

    _channel_dft(h.astype(BF16), cs_ref, put)


def _fft_real_part(xs):
    n = len(xs)
    if n == 1:
        return xs
    ev = _fft_real_part(xs[0::2])
    od = _fft_real_part(xs[1::2])
    out = [None] * n
    for k in range(n // 2):
        c = math.cos(2.0 * math.pi * k / n)
        s = -math.sin(2.0 * math.pi * k / n)
        orr, oi = od[k]
        if k == 0:
            tr, ti = orr, oi
        elif 4 * k == n:
            tr, ti = oi, -orr
        else:
            tr, ti = orr * c - oi * s, orr * s + oi * c
        er, ei = ev[k]
        out[k] = (er + tr, ei + ti)
        out[k + n // 2] = (er - tr, ei - ti)
    return out


def _fnet_seq_kernel(y_ref, g_ref, o_ref, z_scr):
    tc = o_ref.shape[2]
    for n2 in range(SEQ_SLABS):
        yb = y_ref[0, n2].reshape(2 * SEQ_PTS, tc)
        z_scr[n2] = _dot(g_ref[n2], yb)

    rb = 8
    n_lane = tc // 128

    def body(i, carry):
        r = i // n_lane
        c = i % n_lane
        r0 = pl.multiple_of(r * rb, rb)
        c0 = pl.multiple_of(c * 128, 128)
        xs = []
        for n2 in range(SEQ_SLABS):
            re = z_scr[n2, pl.ds(r0, rb), pl.ds(c0, 128)]
            nim = z_scr[n2, pl.ds(SEQ_PTS + r0, rb), pl.ds(c0, 128)]
            xs.append((re, -nim))
        out = _fft_real_part(xs)
        for k2 in range(SEQ_SLABS):
            o_ref[0, pl.ds(k2 * SEQ_PTS + r0, rb), pl.ds(c0, 128)] = out[k2][0]
        return carry

    lax.fori_loop(0, (SEQ_PTS // rb) * n_lane, body, 0)


def _fnet_sample(x, pos, mod, layer, gain, cs_ch, g_seq):
    b, seq, _ = x.shape
    tc = 256
    y = pl.pallas_call(
        _fnet_slab_kernel,
        grid=(b, SEQ_SLABS),
        in_specs=[pl.BlockSpec((1, SEQ_PTS, D), lambda i, j: (i, 0, j)),
                  pl.BlockSpec((SEQ_PTS, D), lambda i, j: (0, j)),
                  pl.BlockSpec((1, 1, 1, N_MOD * D), lambda i, j: (layer, 1 + i, 0, 0)),
                  pl.BlockSpec((1, D), lambda i, j: (0, 0)),
                  pl.BlockSpec((GD, 2 * GD), lambda i, j: (0, 0))],
        out_specs=pl.BlockSpec((1, 1, 2, SEQ_PTS, D), lambda i, j: (i, j, 0, 0, 0)),
        out_shape=jax.ShapeDtypeStruct((b, SEQ_SLABS, 2, SEQ_PTS, D), BF16),
        compiler_params=_params("parallel", "parallel"),
    )(x.reshape(b, SEQ_PTS, SEQ_SLABS * D), pos.reshape(SEQ_PTS, SEQ_SLABS * D), mod, gain, cs_ch)
    return pl.pallas_call(
        _fnet_seq_kernel,
        grid=(b, D // tc),
        in_specs=[pl.BlockSpec((1, SEQ_SLABS, 2, SEQ_PTS, tc), lambda i, j: (i, 0, 0, 0, j)),
                  pl.BlockSpec((SEQ_SLABS, 2 * SEQ_PTS, 2 * SEQ_PTS), lambda i, j: (0, 0, 0))],
        out_specs=pl.BlockSpec((1, seq, tc), lambda i, j: (i, 0, j)),
        out_shape=jax.ShapeDtypeStruct((b, seq, D), F32),
        scratch_shapes=[pltpu.VMEM((SEQ_SLABS, 2 * SEQ_PTS, tc), F32)],
        compiler_params=_params("parallel", "parallel"),
    )(y, g_seq)


def _post_kernel(*refs, has_pos, final):
    it = iter(refs)
    x_ref = next(it)
    pos_ref = next(it) if has_pos else None
    a_ref, mod_ref, wo_ref, w1_ref, w2_ref, gmlp_ref, gnext_ref = (next(it) for _ in range(7))
    modn_ref = None if final else next(it)
    o_ref = next(it)
    hn_ref = None if final else next(it)
    x1_scr, h_scr, acc_scr = (next(it) for _ in range(3))

    j = pl.program_id(1)
    m = mod_ref[0, 0]

    @pl.when(j == 0)
    def _():
        x = x_ref[...]
        if has_pos:
            x = x + pos_ref[...]
        x1 = x + _mod_part(m, 2) * _dot(a_ref[...].astype(BF16), wo_ref[...])
        x1_scr[...] = x1
        h = _rms(x1, gmlp_ref[...]) * (1.0 + _mod_part(m, 4)) + _mod_part(m, 3)
        h_scr[...] = h.astype(BF16)
        acc_scr[...] = jnp.zeros_like(acc_scr)

    u = jnp.maximum(_dot(h_scr[...], w1_ref[...]), 0.0)
    acc_scr[...] += _dot((u * u).astype(BF16), w2_ref[...])

    @pl.when(j == pl.num_programs(1) - 1)
    def _():
        x2 = x1_scr[...] + _mod_part(m, 5) * acc_scr[...]
        if final:
            o_ref[...] = _rms(x2, gnext_ref[...])
        else:
            o_ref[...] = x2
            mn = modn_ref[0, 0]
            hn = _rms(x2, gnext_ref[...]) * (1.0 + _mod_part(mn, 1)) + _mod_part(mn, 0)
            hn_ref[...] = hn.astype(BF16)


def _post_mixer(x, pos, a, mod, layer, mod_row, wo, w1, w2, g_mlp, g_next, final):
    rows = x.shape[0]
    tm, tf = 512, 1024
    has_pos = pos is not None
    n_pos = pos.shape[0] // tm if has_pos else 1
    row_map = lambda i, j: (i, 0)
    const = lambda i, j: (0, 0)
    in_specs = [pl.BlockSpec((tm, D), row_map)]
    args = [x]
    if has_pos:
        in_specs.append(pl.BlockSpec((tm, D), lambda i, j: (i % n_pos, 0)))
        args.append(pos)
    in_specs += [pl.BlockSpec((tm, D), row_map),
                 pl.BlockSpec((1, 1, 1, N_MOD * D), lambda i, j: (layer, mod_row(i * tm), 0, 0)),
                 pl.BlockSpec((D, D), const),
                 pl.BlockSpec((D, tf), lambda i, j: (0, j)),
                 pl.BlockSpec((tf, D), lambda i, j: (j, 0)),
                 pl.BlockSpec((1, D), const),
                 pl.BlockSpec((1, D), const)]
    args += [a, mod, wo, w1, w2, g_mlp, g_next]
    out_specs = [pl.BlockSpec((tm, D), row_map)]
    out_shape = [jax.ShapeDtypeStruct((rows, D), F32)]
    if not final:
        in_specs.append(pl.BlockSpec((1, 1, 1, N_MOD * D),
                                     lambda i, j: (layer + 1, mod_row(i * tm), 0, 0)))
        args.append(mod)
        out_specs.append(pl.BlockSpec((tm, D), row_map))
        out_shape.append(jax.ShapeDtypeStruct((rows, D), BF16))
    res = pl.pallas_call(
        functools.partial(_post_kernel, has_pos=has_pos, final=final),
        grid=(rows // tm, D_FF // tf),
        in_specs=in_specs,
        out_specs=out_specs,
        out_shape=out_shape,
        scratch_shapes=[pltpu.VMEM((tm, D), F32), pltpu.VMEM((tm, D), BF16), pltpu.VMEM((tm, D), F32)],
        compiler_params=_params("parallel", "arbitrary"),
    )(*args)
    return res[0] if final else (res[0], res[1])


def _scan_step(q, k, g, v, st, tri, mask, mid, last):
    g1 = g.astype(BF16)
    r1 = g - g1.astype(F32)
    g2 = r1.astype(BF16)
    g3 = (r1 - g2.astype(F32)).astype(BF16)
    bs = _dot(tri, jnp.concatenate([g1, g2, g3], axis=1))
    b = bs[:, :DK] + bs[:, DK:2 * DK] + bs[:, 2 * DK:]
    ref = b[mid:mid + 1, :]
    bl = b[last:last + 1, :]
    qd = q * jnp.exp(jnp.minimum(b - ref, EXP_CLAMP))
    kd = k * jnp.exp(jnp.minimum(ref - b, EXP_CLAMP))
    scores = jnp.where(mask, _dot_nt(qd.astype(BF16), kd.astype(BF16)), 0.0)
    o = _dot(scores.astype(BF16), v)
    o = o + _dot_nt((qd * jnp.exp(ref)).astype(BF16), st.astype(BF16))
    kl = kd * jnp.exp(bl - ref)
    st_new = st * jnp.exp(bl) + _dot_tn(v, kl.astype(BF16))
    return o, st_new


def _hgrn_kernel(*refs, has_s0):
    it = iter(refs)
    hn_ref, w_ref, lb_ref, gn_ref = (next(it) for _ in range(4))
    s0_ref = next(it) if has_s0 else None
    o_ref, sf_ref = next(it), next(it)
    q_scr, kf_scr, kb_scr, gf_scr, gb_scr, v_scr, sg_scr, of_scr, ob_scr, st_scr = (
        next(it) for _ in range(10))

    seq = hn_ref.shape[1]
    tp = min(seq, 512)

    lraw = lb_ref[...]
    mx = jnp.max(lraw, axis=1, keepdims=True)
    ex = jnp.exp(lraw - mx)
    sm = ex / jnp.sum(ex, axis=1, keepdims=True)
    lbv = (sm[:, 0, :] + sm[:, 1, :]) - sm[:, 0, :]

    def proj(i, carry):
        r0 = pl.multiple_of(i * tp, tp)
        rows = pl.ds(r0, tp)
        p = _dot(hn_ref[0, rows, :], w_ref[0])
        q_scr[rows, :] = _silu(p[:, 0:DK])
        for d, (k_scr, g_scr) in enumerate(((kf_scr, gf_scr), (kb_scr, gb_scr))):
            lb = lbv[d:d + 1, :]
            f = lb + (1.0 - lb) * jax.nn.sigmoid(p[:, (1 + d) * DK:(2 + d) * DK])
            k_scr[rows, :] = 1.0 - f
            g_scr[rows, :] = jnp.log(f)
        v_scr[rows, :] = p[:, 3 * DK:4 * DK].astype(BF16)
        sg_scr[rows, :] = _silu(p[:, 4 * DK:5 * DK])
        return carry

    lax.fori_loop(0, seq // tp, proj, 0)

    for d in range(2):
        if has_s0:
            st_scr[d] = s0_ref[0, 0, d, 0].T
        else:
            st_scr[d] = jnp.zeros((DK, DK), F32)

    ti = lax.broadcasted_iota(jnp.int32, (CHUNK, CHUNK), 0)
    si = lax.broadcasted_iota(jnp.int32, (CHUNK, CHUNK), 1)
    mask_f = ti >= si
    mask_b = si >= ti
    tri_f = mask_f.astype(BF16)
    tri_b = mask_b.astype(BF16)
    n_chunks = seq // CHUNK

    def scan(i, carry):
        rf = pl.ds(pl.multiple_of(i * CHUNK, CHUNK), CHUNK)
        o, st = _scan_step(q_scr[rf, :], kf_scr[rf, :], gf_scr[rf, :], v_scr[rf, :], st_scr[0],
                           tri_f, mask_f, CHUNK // 2 - 1, CHUNK - 1)
        of_scr[rf, :] = o
        st_scr[0] = st
        rb = pl.ds(pl.multiple_of((n_chunks - 1 - i) * CHUNK, CHUNK), CHUNK)
        o, st = _scan_step(q_scr[rb, :], kb_scr[rb, :], gb_scr[rb, :], v_scr[rb, :], st_scr[1],
                           tri_b, mask_b, CHUNK // 2, 0)
        ob_scr[rb, :] = o
        st_scr[1] = st
        return carry

    lax.fori_loop(0, n_chunks, scan, 0)

    def finish(i, carry):
        rows = pl.ds(pl.multiple_of(i * tp, tp), tp)
        o = of_scr[rows, :] + ob_scr[rows, :]
        o = o * lax.rsqrt(jnp.mean(o * o, axis=-1, keepdims=True) + EPS) * gn_ref[...]
        o_ref[0, rows, :] = (o * sg_scr[rows, :]).astype(o_ref.dtype)
        return carry

    lax.fori_loop(0, seq // tp, finish, 0)
    for d in range(2):
        sf_ref[0, 0, d, 0] = st_scr[d].T


def _hgrn_mixer(hn, w_heads, lb_raw, g_norm, s0):
    b, seq, _ = hn.shape
    has_s0 = s0 is not None
    st_spec = pl.BlockSpec((1, 1, 2, 1, DK, DK), lambda i, h: (i, 0, 0, h, 0, 0))
    in_specs = [pl.BlockSpec((1, seq, D), lambda i, h: (i, 0, 0)),
                pl.BlockSpec((1, D, PROJ * DK), lambda i, h: (h, 0, 0)),
                pl.BlockSpec((2, 2, DK), lambda i, h: (0, 0, h)),
                pl.BlockSpec((1, DK), lambda i, h: (0, 0))]
    args = [hn, w_heads, lb_raw, g_norm]
    if has_s0:
        in_specs.append(st_spec)
        args.append(s0)
    f32_rows = pltpu.VMEM((seq, DK), F32)
    return pl.pallas_call(
        functools.partial(_hgrn_kernel, has_s0=has_s0),
        grid=(b, HEADS),
        in_specs=in_specs,
        out_specs=[pl.BlockSpec((1, seq, DK), lambda i, h: (i, 0, h)), st_spec],
        out_shape=[jax.ShapeDtypeStruct((b, seq, D), BF16),
                   jax.ShapeDtypeStruct((b, 1, 2, HEADS, DK, DK), F32)],
        scratch_shapes=[f32_rows, f32_rows, f32_rows, f32_rows, f32_rows,
                        pltpu.VMEM((seq, DK), BF16), f32_rows, f32_rows, f32_rows,
                        pltpu.VMEM((2, DK, DK), F32)],
        compiler_params=_params("parallel", "arbitrary"),
    )(*args)


def kernel(x_prompt, x_sample, state_hgrn, c, c_ctx, ada_w, ada_b, norm_mix, norm_mlp, fnet_wo,
           hgrn_w_in, hgrn_lb, hgrn_norm, hgrn_wo, mlp_w1, mlp_w2, norm_final):
    bp, lp, _ = x_prompt.shape
    bs, ls, _ = x_sample.shape
    assert ada_w.shape[0] == 2 and ls == SEQ_PTS * SEQ_SLABS and lp == GD

    cs_ch, t_ctx, g_seq = _dft_tables()
    pos = _grid_pos_embed(ls, D)

    cond8 = jnp.zeros((8, D), F32).at[0].set(c_ctx).at[1:1 + bs].set(c)
    mod = _modulation(cond8, ada_w, ada_b)

    wo0 = fnet_wo[0].astype(BF16)
    wo1 = hgrn_wo[0].astype(BF16)
    w1 = mlp_w1.astype(BF16)
    w2 = mlp_w2.astype(BF16)
    w_heads = (hgrn_w_in[0].reshape(D, PROJ, HEADS, DK).transpose(2, 0, 1, 3)
               .reshape(HEADS, D, PROJ * DK).astype(BF16))
    gm = norm_mix.reshape(-1, 1, D)
    gf = norm_mlp.reshape(-1, 1, D)
    g_fin = norm_final.reshape(1, D)
    g_hn = hgrn_norm[0].reshape(1, DK)

    ctx_row = lambda r: 0
    smp_row = lambda r: 1 + r // ls

    mixed = _fnet_ctx(x_prompt, mod, 0, gm[0], cs_ch, t_ctx)
    x2, hn = _post_mixer(x_prompt.reshape(bp * lp, D), None, mixed.reshape(bp * lp, D), mod, 0,
                         ctx_row, wo0, w1[0], w2[0], gf[0], gm[1], final=False)
    o, new_state = _hgrn_mixer(hn.reshape(bp, lp, D), w_heads, hgrn_lb, g_hn, None)
    y_prompt = _post_mixer(x2, None, o.reshape(bp * lp, D), mod, 1, ctx_row, wo1, w1[1], w2[1],
                           gf[1], g_fin, final=True).reshape(bp, lp, D)

    mixed = _fnet_sample(x_sample, pos, mod, 0, gm[0], cs_ch, g_seq)
    x2, hn = _post_mixer(x_sample.reshape(bs * ls, D), pos, mixed.reshape(bs * ls, D), mod, 0,
                         smp_row, wo0, w1[0], w2[0], gf[0], gm[1], final=False)
    o, _ = _hgrn_mixer(hn.reshape(bs, ls, D), w_heads, hgrn_lb, g_hn, state_hgrn)
    y_sample = _post_mixer(x2, None, o.reshape(bs * ls, D), mod, 1, smp_row, wo1, w1[1], w2[1],
                           gf[1], g_fin, final=True).reshape(bs, ls, D)
    return (y_prompt, y_sample, new_state)
```

```python
import functools
import math

import numpy as np
import jax
import jax.numpy as jnp
from jax import lax
from jax.experimental import pallas as pl
from jax.experimental.pallas import tpu as pltpu

F32 = jnp.float32
BF16 = jnp.bfloat16

D = 1024
N_MOD = 6
D_FF = 4 * D
EPS = 1e-6
GROUPS = 4
GD = D // GROUPS
HEADS = 8
DK = 128
GRID_W = 64
POS_BASE = 10000.0
PROJ = 5

SEQ_SLABS = 16
SEQ_PTS = 256
CHUNK = 32
SCAN_TILE = 256
EXP_CLAMP = 80.0

VMEM_LIMIT = 56 * 1024 * 1024


def _dft_tables():
    n = np.arange(GD)
    ang = 2.0 * np.pi * ((n[:, None] * n[None, :]) % GD) / GD
    cs_ch = np.concatenate([np.cos(ang), np.sin(ang)], axis=1) / math.sqrt(GD)
    t_ctx = np.concatenate([np.cos(ang), -np.sin(ang)], axis=1) / math.sqrt(GD)
    length = SEQ_PTS * SEQ_SLABS
    k1 = np.arange(SEQ_PTS)[:, None]
    n1 = np.arange(SEQ_PTS)[None, :]
    blocks = []
    for n2 in range(SEQ_SLABS):
        a = 2.0 * np.pi * ((k1 * (SEQ_SLABS * n1 + n2)) % length) / length
        gc, gs = np.cos(a) / math.sqrt(length), np.sin(a) / math.sqrt(length)
        blocks.append(np.block([[gc, -gs], [gs, gc]]))
    g_seq = np.stack(blocks)
    return tuple(jnp.asarray(t, F32).astype(BF16) for t in (cs_ch, t_ctx, g_seq))


def _grid_pos_embed(n_tok, d):
    rows = n_tok // GRID_W
    quarter = d // 4
    omega = 1.0 / (POS_BASE ** (jnp.arange(quarter, dtype=F32) / quarter))
    r = jnp.arange(rows, dtype=F32)[:, None] * omega[None, :]
    cl = jnp.arange(GRID_W, dtype=F32)[:, None] * omega[None, :]
    er = jnp.concatenate([jnp.sin(r), jnp.cos(r)], axis=-1)
    ec = jnp.concatenate([jnp.sin(cl), jnp.cos(cl)], axis=-1)
    emb = jnp.concatenate([jnp.broadcast_to(er[:, None, :], (rows, GRID_W, d // 2)),
                           jnp.broadcast_to(ec[None, :, :], (rows, GRID_W, d // 2))], axis=-1)
    return emb.reshape(n_tok, d)


def _silu(x):
    return x * jax.nn.sigmoid(x)


def _rms(x, g):
    return x * lax.rsqrt(jnp.mean(x * x, axis=-1, keepdims=True) + EPS) * g


def _mod_part(m, idx):
    return m[:, idx * D:(idx + 1) * D]


def _dot(a, b):
    return jnp.dot(a, b, preferred_element_type=F32)


def _dot_nt(a, b):
    return lax.dot_general(a, b, (((1,), (1,)), ((), ())), preferred_element_type=F32)


def _dot_tn(a, b):
    return lax.dot_general(a, b, (((0,), (0,)), ((), ())), preferred_element_type=F32)


def _params(*sem):
    return pltpu.CompilerParams(dimension_semantics=sem, vmem_limit_bytes=VMEM_LIMIT)


def _mod_kernel(c_ref, w_ref, b_ref, o_ref):
    s = _silu(c_ref[...])
    o_ref[0] = _dot(s.astype(BF16), w_ref[0].astype(BF16)) + b_ref[0]


def _modulation(cond8, ada_w, ada_b):
    depth = ada_w.shape[0]
    tn = 1536
    out = pl.pallas_call(
        _mod_kernel,
        grid=(depth, N_MOD * D // tn),
        in_specs=[pl.BlockSpec((8, D), lambda l, j: (0, 0)),
                  pl.BlockSpec((1, D, tn), lambda l, j: (l, 0, j)),
                  pl.BlockSpec((1, 1, tn), lambda l, j: (l, 0, j))],
        out_specs=pl.BlockSpec((1, 8, tn), lambda l, j: (l, 0, j)),
        out_shape=jax.ShapeDtypeStruct((depth, 8, N_MOD * D), F32),
        compiler_params=_params("parallel", "parallel"),
        name="adaln_mod",
    )(cond8, ada_w, ada_b.reshape(depth, 1, N_MOD * D))
    return out.reshape(depth, 8, 1, N_MOD * D)


def _channel_dft(hb, cs_ref, put):
    for g in range(GROUPS):
        y = _dot(hb[:, g * GD:(g + 1) * GD], cs_ref[...])
        put(0, g, y[:, :GD].astype(BF16))
        put(1, g, y[:, GD:].astype(BF16))


def _fnet_ctx_kernel(x_ref, mod_ref, g_ref, cs_ref, t_ref, o_ref, y_scr):
    m = mod_ref[0, 0]
    h = _rms(x_ref[0], g_ref[...]) * (1.0 + _mod_part(m, 1)) + _mod_part(m, 0)
    seq = x_ref.shape[1]

    def put(kind, g, val):
        y_scr[kind * seq:(kind + 1) * seq, g * GD:(g + 1) * GD] = val

    _channel_dft(h.astype(BF16), cs_ref, put)
    o_ref[0] = _dot(t_ref[...], y_scr[...]).astype(o_ref.dtype)


def _fnet_ctx(x, mod, layer, gain, cs_ch, t_ctx):
    b, seq, _ = x.shape
    return pl.pallas_call(
        _fnet_ctx_kernel,
        grid=(b,),
        in_specs=[pl.BlockSpec((1, seq, D), lambda i: (i, 0, 0)),
                  pl.BlockSpec((1, 1, 1, N_MOD * D), lambda i: (layer, 0, 0, 0)),
                  pl.BlockSpec((1, D), lambda i: (0, 0)),
                  pl.BlockSpec((GD, 2 * GD), lambda i: (0, 0)),
                  pl.BlockSpec((seq, 2 * seq), lambda i: (0, 0))],
        out_specs=pl.BlockSpec((1, seq, D), lambda i: (i, 0, 0)),
        out_shape=jax.ShapeDtypeStruct((b, seq, D), BF16),
        scratch_shapes=[pltpu.VMEM((2 * seq, D), BF16)],
        compiler_params=_params("parallel"),
        name="fnet_ctx",
    )(x, mod, gain, cs_ch, t_ctx)


def _fnet_slab_kernel(x_ref, pos_ref, mod_ref, g_ref, cs_ref, y_ref):
    m = mod_ref[0, 0]
    h = _rms(x_ref[0] + pos_ref[...], g_ref[...]) * (1.0 + _mod_part(m, 1)) + _mod_part(m, 0)

    def put(kind, g, val):
        y_ref[0, 0, kind, :, g * GD:(g + 1) * GD] = val

    _channel_dft(h.astype(BF16), cs_ref, put)


def _fft_real_part(xs):
    n = len(xs)
    if n == 1:
        return xs
    ev = _fft_real_part(xs[0::2])
    od = _fft_real_part(xs[1::2])
    out = [None] * n
    for k in range(n // 2):
        c = math.cos(2.0 * math.pi * k / n)
        s = -math.sin(2.0 * math.pi * k / n)
        orr, oi = od[k]
        if k == 0:
            tr, ti = orr, oi
        elif 4 * k == n:
            tr, ti = oi, -orr
        else:
            tr, ti = orr * c - oi * s, orr * s + oi * c
        er, ei = ev[k]
        out[k] = (er + tr, ei + ti)
        out[k + n // 2] = (er - tr, ei - ti)
    return out


def _fnet_seq_kernel(y_ref, g_ref, o_ref, z_scr):
    tc = o_ref.shape[2]
    for n2 in range(SEQ_SLABS):
        yb = y_ref[0, n2].reshape(2 * SEQ_PTS, tc)
        z_scr[n2] = _dot(g_ref[n2], yb)

    rb = 8
    n_lane = tc // 128

    def body(i, carry):
        r = i // n_lane
        c = i % n_lane
        r0 = pl.multiple_of(r * rb, rb)
        c0 = pl.multiple_of(c * 128, 128)
        xs = []
        for n2 in range(SEQ_SLABS):
            re = z_scr[n2, pl.ds(r0, rb), pl.ds(c0, 128)]
            nim = z_scr[n2, pl.ds(SEQ_PTS + r0, rb), pl.ds(c0, 128)]
            xs.append((re, -nim))
        out = _fft_real_part(xs)
        for k2 in range(SEQ_SLABS):
            o_ref[0, pl.ds(k2 * SEQ_PTS + r0, rb), pl.ds(c0, 128)] = out[k2][0]
        return carry

    lax.fori_loop(0, (SEQ_PTS // rb) * n_lane, body, 0)


def _fnet_sample(x, pos, mod, layer, gain, cs_ch, g_seq):
    b, seq, _ = x.shape
    tc = 256
    y = pl.pallas_call(
        _fnet_slab_kernel,
        grid=(b, SEQ_SLABS),
        in_specs=[pl.BlockSpec((1, SEQ_PTS, D), lambda i, j: (i, 0, j)),
                  pl.BlockSpec((SEQ_PTS, D), lambda i, j: (0, j)),
                  pl.BlockSpec((1, 1, 1, N_MOD * D), lambda i, j: (layer, 1 + i, 0, 0)),
                  pl.BlockSpec((1, D), lambda i, j: (0, 0)),
                  pl.BlockSpec((GD, 2 * GD), lambda i, j: (0, 0))],
        out_specs=pl.BlockSpec((1, 1, 2, SEQ_PTS, D), lambda i, j: (i, j, 0, 0, 0)),
        out_shape=jax.ShapeDtypeStruct((b, SEQ_SLABS, 2, SEQ_PTS, D), BF16),
        compiler_params=_params("parallel", "parallel"),
        name="fnet_slab",
    )(x.reshape(b, SEQ_PTS, SEQ_SLABS * D), pos.reshape(SEQ_PTS, SEQ_SLABS * D), mod, gain, cs_ch)
    return pl.pallas_call(
        _fnet_seq_kernel,
        grid=(b, D // tc),
        in_specs=[pl.BlockSpec((1, SEQ_SLABS, 2, SEQ_PTS, tc), lambda i, j: (i, 0, 0, 0, j)),
                  pl.BlockSpec((SEQ_SLABS, 2 * SEQ_PTS, 2 * SEQ_PTS), lambda i, j: (0, 0, 0))],
        out_specs=pl.BlockSpec((1, seq, tc), lambda i, j: (i, 0, j)),
        out_shape=jax.ShapeDtypeStruct((b, seq, D), F32),
        scratch_shapes=[pltpu.VMEM((SEQ_SLABS, 2 * SEQ_PTS, tc), F32)],
        compiler_params=_params("parallel", "parallel"),
        name="fnet_seq",
    )(y, g_seq)


def _post_kernel(*refs, has_pos, final):
    it = iter(refs)
    x_ref = next(it)
    pos_ref = next(it) if has_pos else None
    a_ref, mod_ref, wo_ref, w1_ref, w2_ref, gmlp_ref, gnext_ref = (next(it) for _ in range(7))
    modn_ref = None if final else next(it)
    o_ref = next(it)
    hn_ref = None if final else next(it)
    x1_scr, h_scr, acc_scr = (next(it) for _ in range(3))

    j = pl.program_id(1)
    m = mod_ref[0, 0]

    @pl.when(j == 0)
    def _():
        x = x_ref[...]
        if has_pos:
            x = x + pos_ref[...]
        x1 = x + _mod_part(m, 2) * _dot(a_ref[...].astype(BF16), wo_ref[...])
        x1_scr[...] = x1
        h = _rms(x1, gmlp_ref[...]) * (1.0 + _mod_part(m, 4)) + _mod_part(m, 3)
        h_scr[...] = h.astype(BF16)
        acc_scr[...] = jnp.zeros_like(acc_scr)

    u = jnp.maximum(_dot(h_scr[...], w1_ref[...]), 0.0)
    acc_scr[...] += _dot((u * u).astype(BF16), w2_ref[...])

    @pl.when(j == pl.num_programs(1) - 1)
    def _():
        x2 = x1_scr[...] + _mod_part(m, 5) * acc_scr[...]
        if final:
            o_ref[...] = _rms(x2, gnext_ref[...])
        else:
            o_ref[...] = x2
            mn = modn_ref[0, 0]
            hn = _rms(x2, gnext_ref[...]) * (1.0 + _mod_part(mn, 1)) + _mod_part(mn, 0)
            hn_ref[...] = hn.astype(BF16)


def _post_mixer(x, pos, a, mod, layer, mod_row, wo, w1, w2, g_mlp, g_next, final):
    rows = x.shape[0]
    tm, tf = 512, 1024
    has_pos = pos is not None
    n_pos = pos.shape[0] // tm if has_pos else 1
    row_map = lambda i, j: (i, 0)
    const = lambda i, j: (0, 0)
    in_specs = [pl.BlockSpec((tm, D), row_map)]
    args = [x]
    if has_pos:
        in_specs.append(pl.BlockSpec((tm, D), lambda i, j: (i % n_pos, 0)))
        args.append(pos)
    in_specs += [pl.BlockSpec((tm, D), row_map),
                 pl.BlockSpec((1, 1, 1, N_MOD * D), lambda i, j: (layer, mod_row(i * tm), 0, 0)),
                 pl.BlockSpec((D, D), const),
                 pl.BlockSpec((D, tf), lambda i, j: (0, j)),
                 pl.BlockSpec((tf, D), lambda i, j: (j, 0)),
                 pl.BlockSpec((1, D), const),
                 pl.BlockSpec((1, D), const)]
    args += [a, mod, wo, w1, w2, g_mlp, g_next]
    out_specs = [pl.BlockSpec((tm, D), row_map)]
    out_shape = [jax.ShapeDtypeStruct((rows, D), F32)]
    if not final:
        in_specs.append(pl.BlockSpec((1, 1, 1, N_MOD * D),
                                     lambda i, j: (layer + 1, mod_row(i * tm), 0, 0)))
        args.append(mod)
        out_specs.append(pl.BlockSpec((tm, D), row_map))
        out_shape.append(jax.ShapeDtypeStruct((rows, D), BF16))
    res = pl.pallas_call(
        functools.partial(_post_kernel, has_pos=has_pos, final=final),
        grid=(rows // tm, D_FF // tf),
        in_specs=in_specs,
        out_specs=out_specs,
        out_shape=out_shape,
        scratch_shapes=[pltpu.VMEM((tm, D), F32), pltpu.VMEM((tm, D), BF16), pltpu.VMEM((tm, D), F32)],
        compiler_params=_params("parallel", "arbitrary"),
        name="post_mlp_final" if final else "post_mlp",
    )(*args)
    return res[0] if final else (res[0], res[1])


def _decay_matrices():
    t = np.arange(SCAN_TILE)
    start = (t // CHUNK) * CHUNK
    same = (t[:, None] // CHUNK) == (t[None, :] // CHUNK)
    out = []
    for incl, mid, last in ((t[None, :] <= t[:, None], start + CHUNK // 2 - 1, start + CHUNK - 1),
                            (t[None, :] >= t[:, None], start + CHUNK // 2, start)):
        bd = (same & incl).astype(np.float32)
        out.append(np.concatenate([bd - bd[mid], bd[mid], bd[last] - bd[mid]], axis=0))
    return jnp.asarray(np.stack(out), BF16)


def _hgrn_kernel(*refs, has_s0):
    it = iter(refs)
    hn_ref, w_ref, lb_ref, gn_ref, dm_ref = (next(it) for _ in range(5))
    s0_ref = next(it) if has_s0 else None
    o_ref, sf_ref = next(it), next(it)
    (qd_scr, kd_scr, qb_scr, kl_scr, bl_scr, v_scr, sg_scr, o_scr, st_scr, u_scr,
     sb_scr) = (next(it) for _ in range(11))

    seq = hn_ref.shape[1]
    tp = SCAN_TILE
    n_tiles = seq // tp
    per_tile = tp // CHUNK

    lraw = lb_ref[...]
    mx = jnp.max(lraw, axis=1, keepdims=True)
    ex = jnp.exp(lraw - mx)
    sm = ex / jnp.sum(ex, axis=1, keepdims=True)
    lbv = (sm[:, 0, :] + sm[:, 1, :]) - sm[:, 0, :]

    def proj(i, carry):
        rows = pl.ds(pl.multiple_of(i * tp, tp), tp)
        p = _dot(hn_ref[0, rows, :], w_ref[0])
        q = _silu(p[:, 0:DK])
        v_scr[rows, :] = p[:, 3 * DK:4 * DK].astype(BF16)
        sg_scr[rows, :] = _silu(p[:, 4 * DK:5 * DK])
        for d in range(2):
            lb = lbv[d:d + 1, :]
            f = lb + (1.0 - lb) * jax.nn.sigmoid(p[:, (1 + d) * DK:(2 + d) * DK])
            k = 1.0 - f
            g = jnp.log(f)
            g1 = g.astype(BF16)
            r1 = g - g1.astype(F32)
            g2 = r1.astype(BF16)
            g3 = (r1 - g2.astype(F32)).astype(BF16)
            r = _dot(dm_ref[d], jnp.concatenate([g1, g2, g3], axis=1))
            r = r[:, :DK] + r[:, DK:2 * DK] + r[:, 2 * DK:]
            a, ref, lr = r[:tp], r[tp:2 * tp], r[2 * tp:]
            qd = q * jnp.exp(jnp.minimum(a, EXP_CLAMP))
            kd = k * jnp.exp(jnp.minimum(-a, EXP_CLAMP))
            qd_scr[d, rows, :] = qd.astype(BF16)
            kd_scr[d, rows, :] = kd.astype(BF16)
            qb_scr[d, rows, :] = (qd * jnp.exp(ref)).astype(BF16)
            kl_scr[d, rows, :] = (kd * jnp.exp(lr)).astype(BF16)
            bl_scr[d, rows, :] = ref + lr
        return carry

    lax.fori_loop(0, n_tiles, proj, 0)

    for d in range(2):
        if has_s0:
            st_scr[d] = s0_ref[0, 0, d, 0].T
        else:
            st_scr[d] = jnp.zeros((DK, DK), F32)

    ti = lax.broadcasted_iota(jnp.int32, (tp, tp), 0)
    si = lax.broadcasted_iota(jnp.int32, (tp, tp), 1)
    same = (ti // CHUNK) == (si // CHUNK)
    masks = (same & (ti >= si), same & (si >= ti))

    def tile_dir(d, tile, order):
        r0 = pl.multiple_of(tile * tp, tp)
        rows = pl.ds(r0, tp)
        vv = v_scr[rows, :]
        sc = jnp.where(masks[d], _dot_nt(qd_scr[d, rows, :], kd_scr[d, rows, :]), 0.0)
        o_scr[d, rows, :] = _dot(sc.astype(BF16), vv)
        for c in range(per_tile):
            cr = pl.ds(pl.multiple_of(r0 + c * CHUNK, CHUNK), CHUNK)
            u_scr[d, c] = _dot_tn(vv[c * CHUNK:(c + 1) * CHUNK, :], kl_scr[d, cr, :])
        st = st_scr[d]
        for c in order:
            sb_scr[d, c] = st.astype(BF16)
            decay = jnp.exp(bl_scr[d, pl.ds(pl.multiple_of(r0 + c * CHUNK, CHUNK), 1), :])
            st = st * decay + u_scr[d, c]
        st_scr[d] = st
        for c in range(per_tile):
            cr = pl.ds(pl.multiple_of(r0 + c * CHUNK, CHUNK), CHUNK)
            o_scr[d, cr, :] += _dot_nt(qb_scr[d, cr, :], sb_scr[d, c])

    def scan(i, carry):
        tile_dir(0, i, range(per_tile))
        tile_dir(1, n_tiles - 1 - i, range(per_tile - 1, -1, -1))
        return carry

    lax.fori_loop(0, n_tiles, scan, 0)

    def finish(i, carry):
        rows = pl.ds(pl.multiple_of(i * tp, tp), tp)
        o = o_scr[0, rows, :] + o_scr[1, rows, :]
        o = o * lax.rsqrt(jnp.mean(o * o, axis=-1, keepdims=True) + EPS) * gn_ref[...]
        o_ref[0, rows, :] = (o * sg_scr[rows, :]).astype(o_ref.dtype)
        return carry

    lax.fori_loop(0, n_tiles, finish, 0)
    for d in range(2):
        sf_ref[0, 0, d, 0] = st_scr[d].T


def _hgrn_mixer(hn, w_heads, lb_raw, g_norm, s0):
    b, seq, _ = hn.shape
    has_s0 = s0 is not None
    st_spec = pl.BlockSpec((1, 1, 2, 1, DK, DK), lambda i, h: (i, 0, 0, h, 0, 0))
    in_specs = [pl.BlockSpec((1, seq, D), lambda i, h: (i, 0, 0)),
                pl.BlockSpec((1, D, PROJ * DK), lambda i, h: (h, 0, 0)),
                pl.BlockSpec((2, 2, DK), lambda i, h: (0, 0, h)),
                pl.BlockSpec((1, DK), lambda i, h: (0, 0)),
                pl.BlockSpec((2, 3 * SCAN_TILE, SCAN_TILE), lambda i, h: (0, 0, 0))]
    args = [hn, w_heads, lb_raw, g_norm, _decay_matrices()]
    if has_s0:
        in_specs.append(st_spec)
        args.append(s0)
    bf16_dir = pltpu.VMEM((2, seq, DK), BF16)
    f32_dir = pltpu.VMEM((2, seq, DK), F32)
    return pl.pallas_call(
        functools.partial(_hgrn_kernel, has_s0=has_s0),
        grid=(b, HEADS),
        in_specs=in_specs,
        out_specs=[pl.BlockSpec((1, seq, DK), lambda i, h: (i, 0, h)), st_spec],
        out_shape=[jax.ShapeDtypeStruct((b, seq, D), BF16),
                   jax.ShapeDtypeStruct((b, 1, 2, HEADS, DK, DK), F32)],
        scratch_shapes=[bf16_dir, bf16_dir, bf16_dir, bf16_dir, f32_dir,
                        pltpu.VMEM((seq, DK), BF16), pltpu.VMEM((seq, DK), F32), f32_dir,
                        pltpu.VMEM((2, DK, DK), F32),
                        pltpu.VMEM((2, SCAN_TILE // CHUNK, DK, DK), F32),
                        pltpu.VMEM((2, SCAN_TILE // CHUNK, DK, DK), BF16)],
        compiler_params=_params("parallel", "arbitrary"),
        name="hgrn_scan",
    )(*args)


def kernel(x_prompt, x_sample, state_hgrn, c, c_ctx, ada_w, ada_b, norm_mix, norm_mlp, fnet_wo,
           hgrn_w_in, hgrn_lb, hgrn_norm, hgrn_wo, mlp_w1, mlp_w2, norm_final):
    bp, lp, _ = x_prompt.shape
    bs, ls, _ = x_sample.shape
    assert ada_w.shape[0] == 2 and ls == SEQ_PTS * SEQ_SLABS and lp == GD

    cs_ch, t_ctx, g_seq = _dft_tables()
    pos = _grid_pos_embed(ls, D)

    cond8 = jnp.zeros((8, D), F32).at[0].set(c_ctx).at[1:1 + bs].set(c)
    mod = _modulation(cond8, ada_w, ada_b)

    wo0 = fnet_wo[0].astype(BF16)
    wo1 = hgrn_wo[0].astype(BF16)
    w1 = mlp_w1.astype(BF16)
    w2 = mlp_w2.astype(BF16)
    w_heads = (hgrn_w_in[0].reshape(D, PROJ, HEADS, DK).transpose(2, 0, 1, 3)
               .reshape(HEADS, D, PROJ * DK).astype(BF16))
    gm = norm_mix.reshape(-1, 1, D)
    gf = norm_mlp.reshape(-1, 1, D)
    g_fin = norm_final.reshape(1, D)
    g_hn = hgrn_norm[0].reshape(1, DK)

    ctx_row = lambda r: 0
    smp_row = lambda r: 1 + r // ls

    mixed = _fnet_ctx(x_prompt, mod, 0, gm[0], cs_ch, t_ctx)
    x2, hn = _post_mixer(x_prompt.reshape(bp * lp, D), None, mixed.reshape(bp * lp, D), mod, 0,
                         ctx_row, wo0, w1[0], w2[0], gf[0], gm[1], final=False)
    o, new_state = _hgrn_mixer(hn.reshape(bp, lp, D), w_heads, hgrn_lb, g_hn, None)
    y_prompt = _post_mixer(x2, None, o.reshape(bp * lp, D), mod, 1, ctx_row, wo1, w1[1], w2[1],
                           gf[1], g_fin, final=True).reshape(bp, lp, D)

    mixed = _fnet_sample(x_sample, pos, mod, 0, gm[0], cs_ch, g_seq)
    x2, hn = _post_mixer(x_sample.reshape(bs * ls, D), pos, mixed.reshape(bs * ls, D), mod, 0,
                         smp_row, wo0, w1[0], w2[0], gf[0], gm[1], final=False)
    o, _ = _hgrn_mixer(hn.reshape(bs, ls, D), w_heads, hgrn_lb, g_hn, state_hgrn)
    y_sample = _post_mixer(x2, None, o.reshape(bs * ls, D), mod, 1, smp_row, wo1, w1[1], w2[1],
                           gf[1], g_fin, final=True).reshape(bs, ls, D)
    return (y_prompt, y_sample, new_state)
```

```python
import functools
import math

import numpy as np
import jax
import jax.numpy as jnp
from jax import lax
from jax.experimental import pallas as pl
from jax.experimental.pallas import tpu as pltpu

F32 = jnp.float32
BF16 = jnp.bfloat16

D = 1024
N_MOD = 6
D_FF = 4 * D
EPS = 1e-6
GROUPS = 4
GD = D // GROUPS
HEADS = 8
DK = 128
GRID_W = 64
POS_BASE = 10000.0
PROJ = 5

SEQ_SLABS = 16
SEQ_PTS = 256
LANES = 128
BF16_ROWS = 16
CHUNK = 64
SCAN_TILE = 256
EXP_SAFE = 80.0

VMEM_LIMIT = 56 * 1024 * 1024


def _dft_tables():
    n = np.arange(GD)
    ang = 2.0 * np.pi * ((n[:, None] * n[None, :]) % GD) / GD
    cs_ch = np.concatenate([np.cos(ang), np.sin(ang)], axis=1) / math.sqrt(GD)
    t_ctx = np.concatenate([np.cos(ang), -np.sin(ang)], axis=1) / math.sqrt(GD)
    length = SEQ_PTS * SEQ_SLABS
    k1 = np.arange(SEQ_PTS)[:, None]
    n1 = np.arange(SEQ_PTS)[None, :]
    blocks = []
    for n2 in range(SEQ_SLABS):
        a = 2.0 * np.pi * ((k1 * (SEQ_SLABS * n1 + n2)) % length) / length
        gc, gs = np.cos(a) / math.sqrt(length), np.sin(a) / math.sqrt(length)
        blocks.append(np.block([[gc, -gs], [gs, gc]]))
    g_seq = np.stack(blocks)
    return tuple(jnp.asarray(t, F32).astype(BF16) for t in (cs_ch, t_ctx, g_seq))


def _grid_pos_embed(n_tok, d):
    rows = n_tok // GRID_W
    quarter = d // 4
    omega = 1.0 / (POS_BASE ** (jnp.arange(quarter, dtype=F32) / quarter))
    r = jnp.arange(rows, dtype=F32)[:, None] * omega[None, :]
    cl = jnp.arange(GRID_W, dtype=F32)[:, None] * omega[None, :]
    er = jnp.concatenate([jnp.sin(r), jnp.cos(r)], axis=-1)
    ec = jnp.concatenate([jnp.sin(cl), jnp.cos(cl)], axis=-1)
    emb = jnp.concatenate([jnp.broadcast_to(er[:, None, :], (rows, GRID_W, d // 2)),
                           jnp.broadcast_to(ec[None, :, :], (rows, GRID_W, d // 2))], axis=-1)
    return emb.reshape(n_tok, d)


def _silu(x):
    return x * jax.nn.sigmoid(x)


def _rms(x, g):
    return x * lax.rsqrt(jnp.mean(x * x, axis=-1, keepdims=True) + EPS) * g


def _mod_part(m, idx):
    return m[:, idx * D:(idx + 1) * D]


def _dot(a, b):
    return jnp.dot(a, b, preferred_element_type=F32)


def _dot_nt(a, b):
    return lax.dot_general(a, b, (((1,), (1,)), ((), ())), preferred_element_type=F32)


def _dot_tn(a, b):
    return lax.dot_general(a, b, (((0,), (0,)), ((), ())), preferred_element_type=F32)


def _params(*sem):
    return pltpu.CompilerParams(dimension_semantics=sem, vmem_limit_bytes=VMEM_LIMIT)


def _mod_kernel(c_ref, w_ref, b_ref, o_ref):
    s = _silu(c_ref[...])
    o_ref[0] = _dot(s.astype(BF16), w_ref[0].astype(BF16)) + b_ref[0]


def _modulation(cond8, ada_w, ada_b):
    depth = ada_w.shape[0]
    tn = 1536
    out = pl.pallas_call(
        _mod_kernel,
        grid=(depth, N_MOD * D // tn),
        in_specs=[pl.BlockSpec((8, D), lambda l, j: (0, 0)),
                  pl.BlockSpec((1, D, tn), lambda l, j: (l, 0, j)),
                  pl.BlockSpec((1, 1, tn), lambda l, j: (l, 0, j))],
        out_specs=pl.BlockSpec((1, 8, tn), lambda l, j: (l, 0, j)),
        out_shape=jax.ShapeDtypeStruct((depth, 8, N_MOD * D), F32),
        compiler_params=_params("parallel", "parallel"),
        name="adaln_mod",
    )(cond8, ada_w, ada_b.reshape(depth, 1, N_MOD * D))
    return out.reshape(depth, 8, 1, N_MOD * D)


def _channel_dft(hb, cs_ref, put):
    for g in range(GROUPS):
        y = _dot(hb[:, g * GD:(g + 1) * GD], cs_ref[...])
        put(0, g, y[:, :GD].astype(BF16))
        put(1, g, y[:, GD:].astype(BF16))


def _fnet_ctx_kernel(x_ref, mod_ref, g_ref, cs_ref, t_ref, o_ref, y_scr):
    m = mod_ref[0, 0]
    h = _rms(x_ref[0], g_ref[...]) * (1.0 + _mod_part(m, 1)) + _mod_part(m, 0)
    seq = x_ref.shape[1]

    def put(kind, g, val):
        y_scr[kind * seq:(kind + 1) * seq, g * GD:(g + 1) * GD] = val

    _channel_dft(h.astype(BF16), cs_ref, put)
    o_ref[0] = _dot(t_ref[...], y_scr[...]).astype(o_ref.dtype)


def _fnet_ctx(x, mod, layer, gain, cs_ch, t_ctx):
    b, seq, _ = x.shape
    return pl.pallas_call(
        _fnet_ctx_kernel,
        grid=(b,),
        in_specs=[pl.BlockSpec((1, seq, D), lambda i: (i, 0, 0)),
                  pl.BlockSpec((1, 1, 1, N_MOD * D), lambda i: (layer, 0, 0, 0)),
                  pl.BlockSpec((1, D), lambda i: (0, 0)),
                  pl.BlockSpec((GD, 2 * GD), lambda i: (0, 0)),
                  pl.BlockSpec((seq, 2 * seq), lambda i: (0, 0))],
        out_specs=pl.BlockSpec((1, seq, D), lambda i: (i, 0, 0)),
        out_shape=jax.ShapeDtypeStruct((b, seq, D), BF16),
        scratch_shapes=[pltpu.VMEM((2 * seq, D), BF16)],
        compiler_params=_params("parallel"),
        name="fnet_ctx",
    )(x, mod, gain, cs_ch, t_ctx)


def _fnet_slab_kernel(x_ref, pos_ref, mod_ref, g_ref, cs_ref, y_ref, col_scr, xp_scr):
    tm = x_ref.shape[1]
    rows = tm // SEQ_SLABS
    n_col = D // LANES
    for j in range(n_col):
        cols = slice(j * LANES, (j + 1) * LANES)
        col_scr[j] = x_ref[0, :, cols] + pos_ref[:, cols]
    for n2 in range(SEQ_SLABS):
        for j in range(n_col):
            xp_scr[n2 * rows:(n2 + 1) * rows, j * LANES:(j + 1) * LANES] = (
                col_scr[j, pl.ds(n2, rows, stride=SEQ_SLABS), :])
    m = mod_ref[0, 0]
    h = _rms(xp_scr[...], g_ref[...]) * (1.0 + _mod_part(m, 1)) + _mod_part(m, 0)

    def put(kind, g, val):
        for n2 in range(SEQ_SLABS):
            y_ref[0, n2, kind, :, g * GD:(g + 1) * GD] = val[n2 * rows:(n2 + 1) * rows, :]

    _channel_dft(h.astype(BF16), cs_ref, put)


def _fft_real_part(xs):
    n = len(xs)
    if n == 1:
        return xs
    ev = _fft_real_part(xs[0::2])
    od = _fft_real_part(xs[1::2])
    out = [None] * n
    for k in range(n // 2):
        c = math.cos(2.0 * math.pi * k / n)
        s = -math.sin(2.0 * math.pi * k / n)
        orr, oi = od[k]
        if k == 0:
            tr, ti = orr, oi
        elif 4 * k == n:
            tr, ti = oi, -orr
        else:
            tr, ti = orr * c - oi * s, orr * s + oi * c
        er, ei = ev[k]
        out[k] = (er + tr, ei + ti)
        out[k + n // 2] = (er - tr, ei - ti)
    return out


def _fnet_seq_kernel(y_ref, g_ref, o_ref, z_scr):
    tc = o_ref.shape[2]
    for n2 in range(SEQ_SLABS):
        yb = y_ref[0, n2].reshape(2 * SEQ_PTS, tc)
        z_scr[n2] = _dot(g_ref[n2], yb)

    rb = 8
    n_lane = tc // 128

    def body(i, carry):
        r = i // n_lane
        c = i % n_lane
        r0 = pl.multiple_of(r * rb, rb)
        c0 = pl.multiple_of(c * 128, 128)
        xs = []
        for n2 in range(SEQ_SLABS):
            re = z_scr[n2, pl.ds(r0, rb), pl.ds(c0, 128)]
            nim = z_scr[n2, pl.ds(SEQ_PTS + r0, rb), pl.ds(c0, 128)]
            xs.append((re, -nim))
        out = _fft_real_part(xs)
        for k2 in range(SEQ_SLABS):
            o_ref[0, pl.ds(k2 * SEQ_PTS + r0, rb), pl.ds(c0, 128)] = out[k2][0]
        return carry

    lax.fori_loop(0, (SEQ_PTS // rb) * n_lane, body, 0)


def _fnet_sample(x, pos, mod, layer, gain, cs_ch, g_seq):
    b, seq, _ = x.shape
    tm = 512
    tc = 256
    y = pl.pallas_call(
        _fnet_slab_kernel,
        grid=(b, seq // tm),
        in_specs=[pl.BlockSpec((1, tm, D), lambda i, j: (i, j, 0)),
                  pl.BlockSpec((tm, D), lambda i, j: (j, 0)),
                  pl.BlockSpec((1, 1, 1, N_MOD * D), lambda i, j: (layer, 1 + i, 0, 0)),
                  pl.BlockSpec((1, D), lambda i, j: (0, 0)),
                  pl.BlockSpec((GD, 2 * GD), lambda i, j: (0, 0))],
        out_specs=pl.BlockSpec((1, SEQ_SLABS, 2, tm // SEQ_SLABS, D), lambda i, j: (i, 0, 0, j, 0)),
        out_shape=jax.ShapeDtypeStruct((b, SEQ_SLABS, 2, SEQ_PTS, D), BF16),
        scratch_shapes=[pltpu.VMEM((D // LANES, tm, LANES), F32), pltpu.VMEM((tm, D), F32)],
        compiler_params=_params("parallel", "parallel"),
        name="fnet_slab",
    )(x, pos, mod, gain, cs_ch)
    return pl.pallas_call(
        _fnet_seq_kernel,
        grid=(b, D // tc),
        in_specs=[pl.BlockSpec((1, SEQ_SLABS, 2, SEQ_PTS, tc), lambda i, j: (i, 0, 0, 0, j)),
                  pl.BlockSpec((SEQ_SLABS, 2 * SEQ_PTS, 2 * SEQ_PTS), lambda i, j: (0, 0, 0))],
        out_specs=pl.BlockSpec((1, seq, tc), lambda i, j: (i, 0, j)),
        out_shape=jax.ShapeDtypeStruct((b, seq, D), F32),
        scratch_shapes=[pltpu.VMEM((SEQ_SLABS, 2 * SEQ_PTS, tc), F32)],
        compiler_params=_params("parallel", "parallel"),
        name="fnet_seq",
    )(y, g_seq)


def _post_kernel(*refs, has_pos, final):
    it = iter(refs)
    x_ref = next(it)
    pos_ref = next(it) if has_pos else None
    a_ref, mod_ref, wo_ref, w1_ref, w2_ref, gmlp_ref, gnext_ref = (next(it) for _ in range(7))
    modn_ref = None if final else next(it)
    o_ref = next(it)
    hn_ref = None if final else next(it)
    x1_scr, h_scr, acc_scr = (next(it) for _ in range(3))

    j = pl.program_id(1)
    m = mod_ref[0, 0]

    @pl.when(j == 0)
    def _():
        x = x_ref[...]
        if has_pos:
            x = x + pos_ref[...]
        x1 = x + _mod_part(m, 2) * _dot(a_ref[...].astype(BF16), wo_ref[...])
        x1_scr[...] = x1
        h = _rms(x1, gmlp_ref[...]) * (1.0 + _mod_part(m, 4)) + _mod_part(m, 3)
        h_scr[...] = h.astype(BF16)
        acc_scr[...] = jnp.zeros_like(acc_scr)

    u = jnp.maximum(_dot(h_scr[...], w1_ref[...]), 0.0)
    acc_scr[...] += _dot((u * u).astype(BF16), w2_ref[...])

    @pl.when(j == pl.num_programs(1) - 1)
    def _():
        x2 = x1_scr[...] + _mod_part(m, 5) * acc_scr[...]
        if final:
            o_ref[...] = _rms(x2, gnext_ref[...])
        else:
            o_ref[...] = x2
            mn = modn_ref[0, 0]
            hn = _rms(x2, gnext_ref[...]) * (1.0 + _mod_part(mn, 1)) + _mod_part(mn, 0)
            hn_ref[...] = hn.astype(BF16)


def _post_mixer(x, pos, a, mod, layer, mod_row, wo, w1, w2, g_mlp, g_next, final):
    rows = x.shape[0]
    tm, tf = 512, 1024
    has_pos = pos is not None
    n_pos = pos.shape[0] // tm if has_pos else 1
    row_map = lambda i, j: (i, 0)
    const = lambda i, j: (0, 0)
    in_specs = [pl.BlockSpec((tm, D), row_map)]
    args = [x]
    if has_pos:
        in_specs.append(pl.BlockSpec((tm, D), lambda i, j: (i % n_pos, 0)))
        args.append(pos)
    in_specs += [pl.BlockSpec((tm, D), row_map),
                 pl.BlockSpec((1, 1, 1, N_MOD * D), lambda i, j: (layer, mod_row(i * tm), 0, 0)),
                 pl.BlockSpec((D, D), const),
                 pl.BlockSpec((D, tf), lambda i, j: (0, j)),
                 pl.BlockSpec((tf, D), lambda i, j: (j, 0)),
                 pl.BlockSpec((1, D), const),
                 pl.BlockSpec((1, D), const)]
    args += [a, mod, wo, w1, w2, g_mlp, g_next]
    out_specs = [pl.BlockSpec((tm, D), row_map)]
    out_shape = [jax.ShapeDtypeStruct((rows, D), F32)]
    if not final:
        in_specs.append(pl.BlockSpec((1, 1, 1, N_MOD * D),
                                     lambda i, j: (layer + 1, mod_row(i * tm), 0, 0)))
        args.append(mod)
        out_specs.append(pl.BlockSpec((tm, D), row_map))
        out_shape.append(jax.ShapeDtypeStruct((rows, D), BF16))
    res = pl.pallas_call(
        functools.partial(_post_kernel, has_pos=has_pos, final=final),
        grid=(rows // tm, D_FF // tf),
        in_specs=in_specs,
        out_specs=out_specs,
        out_shape=out_shape,
        scratch_shapes=[pltpu.VMEM((tm, D), F32), pltpu.VMEM((tm, D), BF16), pltpu.VMEM((tm, D), F32)],
        compiler_params=_params("parallel", "arbitrary"),
        name="post_mlp_final" if final else "post_mlp",
    )(*args)
    return res[0] if final else (res[0], res[1])


PER_TILE = SCAN_TILE // CHUNK


def _decay_matrices():
    t = np.arange(SCAN_TILE)
    start = (t // CHUNK) * CHUNK
    first = np.arange(PER_TILE) * CHUNK
    same = (t[:, None] // CHUNK) == (t[None, :] // CHUNK)
    out = []
    for incl, mid, last in ((t[None, :] <= t[:, None], CHUNK // 2 - 1, CHUNK - 1),
                            (t[None, :] >= t[:, None], CHUNK // 2, 0)):
        bd = (same & incl).astype(np.float32)
        ref = bd[first + mid]
        pad = np.zeros((BF16_ROWS - 2 * PER_TILE, SCAN_TILE), np.float32)
        out.append(np.concatenate([bd - bd[start + mid], ref, bd[first + last] - ref, pad], axis=0))
    return jnp.asarray(np.stack(out), BF16)


def _hgrn_kernel(*refs, has_s0):
    it = iter(refs)
    hn_ref, w_ref, lb_ref, gn_ref, dm_ref = (next(it) for _ in range(5))
    s0_ref = next(it) if has_s0 else None
    o_ref, sf_ref = next(it), next(it)
    qd_scr, kd_scr, qb_scr, kl_scr, g_scr, o_scr = ((next(it), next(it)) for _ in range(6))
    ext_scr, q_scr, v_scr, sg_scr, st_scr, u_scr, sb_scr, flag_scr = (next(it) for _ in range(8))

    seq = hn_ref.shape[1]
    tp = SCAN_TILE
    n_tiles = seq // tp
    per_tile = PER_TILE

    lraw = lb_ref[...]
    mx = jnp.max(lraw, axis=1, keepdims=True)
    ex = jnp.exp(lraw - mx)
    sm = ex / jnp.sum(ex, axis=1, keepdims=True)
    lbv = (sm[:, 0, :] + sm[:, 1, :]) - sm[:, 0, :]

    def proj(i, carry):
        rows = pl.ds(pl.multiple_of(i * tp, tp), tp)
        p = _dot(hn_ref[0, rows, :], w_ref[0])
        q = _silu(p[:, 0:DK])
        q_scr[rows, :] = q
        v_scr[rows, :] = p[:, 3 * DK:4 * DK].astype(BF16)
        sg_scr[rows, :] = _silu(p[:, 4 * DK:5 * DK])
        unsafe = jnp.int32(0)
        for d in range(2):
            lb = lbv[d:d + 1, :]
            f = lb + (1.0 - lb) * jax.nn.sigmoid(p[:, (1 + d) * DK:(2 + d) * DK])
            k = 1.0 - f
            g = jnp.log(f)
            g_scr[d][rows, :] = g
            g1 = g.astype(BF16)
            g2 = (g - g1.astype(F32)).astype(BF16)
            r = _dot(dm_ref[d], jnp.concatenate([g1, g2], axis=1))
            r = r[:, :DK] + r[:, DK:]
            a = r[:tp]
            ext = r[tp:]
            ext_scr[d, i] = ext
            unsafe = unsafe | (jnp.max(jnp.abs(a)) > EXP_SAFE).astype(jnp.int32)
            qd = q * jnp.exp(a)
            kd = k * jnp.exp(-a)
            qd_scr[d][rows, :] = qd.astype(BF16)
            kd_scr[d][rows, :] = kd.astype(BF16)
            e_ref = jnp.exp(ext[0:per_tile])
            e_lr = jnp.exp(ext[per_tile:2 * per_tile])
            for c in range(per_tile):
                cr = pl.ds(pl.multiple_of(i * tp + c * CHUNK, CHUNK), CHUNK)
                cs = slice(c * CHUNK, (c + 1) * CHUNK)
                qb_scr[d][cr, :] = (qd[cs] * e_ref[c:c + 1]).astype(BF16)
                kl_scr[d][cr, :] = (kd[cs] * e_lr[c:c + 1]).astype(BF16)
        flag_scr[i] = unsafe
        return carry

    lax.fori_loop(0, n_tiles, proj, 0)

    for d in range(2):
        if has_s0:
            st_scr[d] = s0_ref[0, 0, d, 0].T
        else:
            st_scr[d] = jnp.zeros((DK, DK), F32)

    ti = lax.broadcasted_iota(jnp.int32, (tp, tp), 0)
    si = lax.broadcasted_iota(jnp.int32, (tp, tp), 1)
    same = (ti // CHUNK) == (si // CHUNK)
    masks = (same & (ti >= si), same & (si >= ti))

    def tile_pair(tiles):
        orders = (range(per_tile), range(per_tile - 1, -1, -1))
        r0 = [pl.multiple_of(t * tp, tp) for t in tiles]
        rows = [pl.ds(r, tp) for r in r0]
        chunk = lambda d, c: pl.ds(pl.multiple_of(r0[d] + c * CHUNK, CHUNK), CHUNK)
        vv = [v_scr[rows[d], :] for d in range(2)]
        for d in range(2):
            for c in range(per_tile):
                u_scr[d, c] = _dot_tn(vv[d][c * CHUNK:(c + 1) * CHUNK, :], kl_scr[d][chunk(d, c), :])
        sc = [jnp.where(masks[d], _dot_nt(qd_scr[d][rows[d], :], kd_scr[d][rows[d], :]), 0.0)
              for d in range(2)]
        for d in range(2):
            ext = ext_scr[d, tiles[d]]
            decay = jnp.exp(ext[0:per_tile] + ext[per_tile:2 * per_tile])
            st = st_scr[d]
            for c in orders[d]:
                sb_scr[d, c] = st.astype(BF16)
                st = st * decay[c:c + 1] + u_scr[d, c]
            st_scr[d] = st
        for d in range(2):
            o_scr[d][rows[d], :] = _dot(sc[d].astype(BF16), vv[d])
        for d in range(2):
            for c in range(per_tile):
                o_scr[d][chunk(d, c), :] += _dot_nt(qb_scr[d][chunk(d, c), :], sb_scr[d, c])

    rid = lax.broadcasted_iota(jnp.int32, (BF16_ROWS, DK), 0)

    def tile_dir_exact(d, tile, reverse):
        r0 = tile * tp

        def step(j, st):
            t = r0 + ((tp - 1 - j) if reverse else j)
            t0 = pl.multiple_of((t // BF16_ROWS) * BF16_ROWS, BF16_ROWS)
            grp = pl.ds(t0, BF16_ROWS)
            sel = rid == (t - t0)
            g = g_scr[d][grp, :]
            f = jnp.exp(jnp.sum(jnp.where(sel, g, 0.0), axis=0, keepdims=True))
            k = jnp.where(sel, 1.0 - jnp.exp(g), 0.0)
            st = st * f + _dot_tn(v_scr[grp, :], k.astype(BF16))
            o = _dot_nt(q_scr[grp, :].astype(BF16), st.astype(BF16))
            o_scr[d][grp, :] = jnp.where(sel, o, o_scr[d][grp, :])
            return st

        st_scr[d] = lax.fori_loop(0, tp, step, st_scr[d])

    def scan(i, carry):
        tf, tb = i, n_tiles - 1 - i
        unsafe = flag_scr[tf] | flag_scr[tb]

        @pl.when(unsafe == 0)
        def _():
            tile_pair((tf, tb))

        @pl.when(unsafe != 0)
        def _():
            tile_dir_exact(0, tf, False)
            tile_dir_exact(1, tb, True)

        return carry

    lax.fori_loop(0, n_tiles, scan, 0)

    def finish(i, carry):
        rows = pl.ds(pl.multiple_of(i * tp, tp), tp)
        o = o_scr[0][rows, :] + o_scr[1][rows, :]
        o = o * lax.rsqrt(jnp.mean(o * o, axis=-1, keepdims=True) + EPS) * gn_ref[...]
        o_ref[0, rows, :] = (o * sg_scr[rows, :]).astype(o_ref.dtype)
        return carry

    lax.fori_loop(0, n_tiles, finish, 0)
    for d in range(2):
        sf_ref[0, 0, d, 0] = st_scr[d].T


def _hgrn_mixer(hn, w_heads, lb_raw, g_norm, s0):
    b, seq, _ = hn.shape
    has_s0 = s0 is not None
    st_spec = pl.BlockSpec((1, 1, 2, 1, DK, DK), lambda i, h: (i, 0, 0, h, 0, 0))
    in_specs = [pl.BlockSpec((1, seq, D), lambda i, h: (i, 0, 0)),
                pl.BlockSpec((1, D, PROJ * DK), lambda i, h: (h, 0, 0)),
                pl.BlockSpec((2, 2, DK), lambda i, h: (0, 0, h)),
                pl.BlockSpec((1, DK), lambda i, h: (0, 0)),
                pl.BlockSpec((2, SCAN_TILE + BF16_ROWS, SCAN_TILE), lambda i, h: (0, 0, 0))]
    args = [hn, w_heads, lb_raw, g_norm, _decay_matrices()]
    if has_s0:
        in_specs.append(st_spec)
        args.append(s0)
    n_tiles = seq // SCAN_TILE
    bf16_rows = pltpu.VMEM((seq, DK), BF16)
    f32_rows = pltpu.VMEM((seq, DK), F32)
    return pl.pallas_call(
        functools.partial(_hgrn_kernel, has_s0=has_s0),
        grid=(b, HEADS),
        in_specs=in_specs,
        out_specs=[pl.BlockSpec((1, seq, DK), lambda i, h: (i, 0, h)), st_spec],
        out_shape=[jax.ShapeDtypeStruct((b, seq, D), BF16),
                   jax.ShapeDtypeStruct((b, 1, 2, HEADS, DK, DK), F32)],
        scratch_shapes=[bf16_rows] * 8 + [f32_rows] * 4 + [
                        pltpu.VMEM((2, n_tiles, BF16_ROWS, DK), F32),
                        f32_rows, bf16_rows, f32_rows,
                        pltpu.VMEM((2, DK, DK), F32),
                        pltpu.VMEM((2, PER_TILE, DK, DK), F32),
                        pltpu.VMEM((2, PER_TILE, DK, DK), BF16),
                        pltpu.SMEM((n_tiles,), jnp.int32)],
        compiler_params=_params("parallel", "arbitrary"),
        name="hgrn_scan",
    )(*args)


def kernel(x_prompt, x_sample, state_hgrn, c, c_ctx, ada_w, ada_b, norm_mix, norm_mlp, fnet_wo,
           hgrn_w_in, hgrn_lb, hgrn_norm, hgrn_wo, mlp_w1, mlp_w2, norm_final):
    bp, lp, _ = x_prompt.shape
    bs, ls, _ = x_sample.shape
    assert ada_w.shape[0] == 2 and ls == SEQ_PTS * SEQ_SLABS and lp == GD

    cs_ch, t_ctx, g_seq = _dft_tables()
    pos = _grid_pos_embed(ls, D)

    cond8 = jnp.zeros((8, D), F32).at[0].set(c_ctx).at[1:1 + bs].set(c)
    mod = _modulation(cond8, ada_w, ada_b)

    wo0 = fnet_wo[0].astype(BF16)
    wo1 = hgrn_wo[0].astype(BF16)
    w1 = mlp_w1.astype(BF16)
    w2 = mlp_w2.astype(BF16)
    w_heads = (hgrn_w_in[0].reshape(D, PROJ, HEADS, DK).transpose(2, 0, 1, 3)
               .reshape(HEADS, D, PROJ * DK).astype(BF16))
    gm = norm_mix.reshape(-1, 1, D)
    gf = norm_mlp.reshape(-1, 1, D)
    g_fin = norm_final.reshape(1, D)
    g_hn = hgrn_norm[0].reshape(1, DK)

    ctx_row = lambda r: 0
    smp_row = lambda r: 1 + r // ls

    mixed = _fnet_ctx(x_prompt, mod, 0, gm[0], cs_ch, t_ctx)
    x2, hn = _post_mixer(x_prompt.reshape(bp * lp, D), None, mixed.reshape(bp * lp, D), mod, 0,
                         ctx_row, wo0, w1[0], w2[0], gf[0], gm[1], final=False)
    o, new_state = _hgrn_mixer(hn.reshape(bp, lp, D), w_heads, hgrn_lb, g_hn, None)
    y_prompt = _post_mixer(x2, None, o.reshape(bp * lp, D), mod, 1, ctx_row, wo1, w1[1], w2[1],
                           gf[1], g_fin, final=True).reshape(bp, lp, D)

    mixed = _fnet_sample(x_sample, pos, mod, 0, gm[0], cs_ch, g_seq)
    x2, hn = _post_mixer(x_sample.reshape(bs * ls, D), pos, mixed.reshape(bs * ls, D), mod, 0,
                         smp_row, wo0, w1[0], w2[0], gf[0], gm[1], final=False)
    o, _ = _hgrn_mixer(hn.reshape(bs, ls, D), w_heads, hgrn_lb, g_hn, state_hgrn)
    y_sample = _post_mixer(x2, None, o.reshape(bs * ls, D), mod, 1, smp_row, wo1, w1[1], w2[1],
                           gf[1], g_fin, final=True).reshape(bs, ls, D)
    return (y_prompt, y_sample, new_state)
```

```python
import functools
import math

import numpy as np
import jax
import jax.numpy as jnp
from jax import lax
from jax.experimental import pallas as pl
from jax.experimental.pallas import tpu as pltpu

F32 = jnp.float32
BF16 = jnp.bfloat16

D = 1024
N_MOD = 6
D_FF = 4 * D
EPS = 1e-6
GROUPS = 4
GD = D // GROUPS
HEADS = 8
DK = 128
GRID_W = 64
POS_BASE = 10000.0
PROJ = 5

SEQ_SLABS = 16
SEQ_PTS = 256
LANES = 128
BF16_ROWS = 16
CHUNK = 64
SCAN_TILE = 256
EXP_SAFE = 80.0
POST_TM = 512
POST_SUB = 256
POST_TF = 1024

VMEM_LIMIT = 56 * 1024 * 1024


def _dft_tables():
    n = np.arange(GD)
    ang = 2.0 * np.pi * ((n[:, None] * n[None, :]) % GD) / GD
    cs_ch = np.concatenate([np.cos(ang), np.sin(ang)], axis=1) / math.sqrt(GD)
    t_ctx = np.concatenate([np.cos(ang), -np.sin(ang)], axis=1) / math.sqrt(GD)
    length = SEQ_PTS * SEQ_SLABS
    k1 = np.arange(SEQ_PTS)[:, None]
    n1 = np.arange(SEQ_PTS)[None, :]
    blocks = []
    for n2 in range(SEQ_SLABS):
        a = 2.0 * np.pi * ((k1 * (SEQ_SLABS * n1 + n2)) % length) / length
        gc, gs = np.cos(a) / math.sqrt(length), np.sin(a) / math.sqrt(length)
        blocks.append(np.block([[gc, -gs], [gs, gc]]))
    g_seq = np.stack(blocks)
    return tuple(jnp.asarray(t, F32).astype(BF16) for t in (cs_ch, t_ctx, g_seq))


def _grid_pos_embed(n_tok, d):
    rows = n_tok // GRID_W
    quarter = d // 4
    omega = 1.0 / (POS_BASE ** (jnp.arange(quarter, dtype=F32) / quarter))
    r = jnp.arange(rows, dtype=F32)[:, None] * omega[None, :]
    cl = jnp.arange(GRID_W, dtype=F32)[:, None] * omega[None, :]
    er = jnp.concatenate([jnp.sin(r), jnp.cos(r)], axis=-1)
    ec = jnp.concatenate([jnp.sin(cl), jnp.cos(cl)], axis=-1)
    emb = jnp.concatenate([jnp.broadcast_to(er[:, None, :], (rows, GRID_W, d // 2)),
                           jnp.broadcast_to(ec[None, :, :], (rows, GRID_W, d // 2))], axis=-1)
    return emb.reshape(n_tok, d)


def _silu(x):
    return x * jax.nn.sigmoid(x)


def _rms(x, g):
    return x * lax.rsqrt(jnp.mean(x * x, axis=-1, keepdims=True) + EPS) * g


def _mod_part(m, idx):
    return m[:, idx * D:(idx + 1) * D]


def _dot(a, b):
    return jnp.dot(a, b, preferred_element_type=F32)


def _dot_nt(a, b):
    return lax.dot_general(a, b, (((1,), (1,)), ((), ())), preferred_element_type=F32)


def _dot_tn(a, b):
    return lax.dot_general(a, b, (((0,), (0,)), ((), ())), preferred_element_type=F32)


def _params(*sem):
    return pltpu.CompilerParams(dimension_semantics=sem, vmem_limit_bytes=VMEM_LIMIT)


def _mod_kernel(c_ref, w_ref, b_ref, o_ref):
    s = _silu(c_ref[...])
    o_ref[0] = _dot(s.astype(BF16), w_ref[0].astype(BF16)) + b_ref[0]


def _modulation(cond8, ada_w, ada_b):
    depth = ada_w.shape[0]
    tn = 1536
    out = pl.pallas_call(
        _mod_kernel,
        grid=(depth, N_MOD * D // tn),
        in_specs=[pl.BlockSpec((8, D), lambda l, j: (0, 0)),
                  pl.BlockSpec((1, D, tn), lambda l, j: (l, 0, j)),
                  pl.BlockSpec((1, 1, tn), lambda l, j: (l, 0, j))],
        out_specs=pl.BlockSpec((1, 8, tn), lambda l, j: (l, 0, j)),
        out_shape=jax.ShapeDtypeStruct((depth, 8, N_MOD * D), F32),
        compiler_params=_params("parallel", "parallel"),
        name="adaln_mod",
    )(cond8, ada_w, ada_b.reshape(depth, 1, N_MOD * D))
    return out.reshape(depth, 8, 1, N_MOD * D)


def _channel_dft(hb, cs_ref, put):
    for g in range(GROUPS):
        y = _dot(hb[:, g * GD:(g + 1) * GD], cs_ref[...])
        put(0, g, y[:, :GD].astype(BF16))
        put(1, g, y[:, GD:].astype(BF16))


def _fnet_ctx_kernel(x_ref, mod_ref, g_ref, cs_ref, t_ref, o_ref, y_scr):
    m = mod_ref[0, 0]
    h = _rms(x_ref[0], g_ref[...]) * (1.0 + _mod_part(m, 1)) + _mod_part(m, 0)
    seq = x_ref.shape[1]

    def put(kind, g, val):
        y_scr[kind * seq:(kind + 1) * seq, g * GD:(g + 1) * GD] = val

    _channel_dft(h.astype(BF16), cs_ref, put)
    o_ref[0] = _dot(t_ref[...], y_scr[...]).astype(o_ref.dtype)


def _fnet_ctx(x, mod, layer, gain, cs_ch, t_ctx):
    b, seq, _ = x.shape
    return pl.pallas_call(
        _fnet_ctx_kernel,
        grid=(b,),
        in_specs=[pl.BlockSpec((1, seq, D), lambda i: (i, 0, 0)),
                  pl.BlockSpec((1, 1, 1, N_MOD * D), lambda i: (layer, 0, 0, 0)),
                  pl.BlockSpec((1, D), lambda i: (0, 0)),
                  pl.BlockSpec((GD, 2 * GD), lambda i: (0, 0)),
                  pl.BlockSpec((seq, 2 * seq), lambda i: (0, 0))],
        out_specs=pl.BlockSpec((1, seq, D), lambda i: (i, 0, 0)),
        out_shape=jax.ShapeDtypeStruct((b, seq, D), BF16),
        scratch_shapes=[pltpu.VMEM((2 * seq, D), BF16)],
        compiler_params=_params("parallel"),
        name="fnet_ctx",
    )(x, mod, gain, cs_ch, t_ctx)


def _fnet_slab_kernel(x_ref, pos_ref, mod_ref, g_ref, cs_ref, y_ref, col_scr, xp_scr):
    tm = x_ref.shape[1]
    rows = tm // SEQ_SLABS
    n_col = D // LANES
    for j in range(n_col):
        cols = slice(j * LANES, (j + 1) * LANES)
        col_scr[j] = x_ref[0, :, cols] + pos_ref[:, cols]
    for n2 in range(SEQ_SLABS):
        for j in range(n_col):
            xp_scr[n2 * rows:(n2 + 1) * rows, j * LANES:(j + 1) * LANES] = (
                col_scr[j, pl.ds(n2, rows, stride=SEQ_SLABS), :])
    m = mod_ref[0, 0]
    h = _rms(xp_scr[...], g_ref[...]) * (1.0 + _mod_part(m, 1)) + _mod_part(m, 0)

    def put(kind, g, val):
        for n2 in range(SEQ_SLABS):
            y_ref[0, n2, kind, :, g * GD:(g + 1) * GD] = val[n2 * rows:(n2 + 1) * rows, :]

    _channel_dft(h.astype(BF16), cs_ref, put)


def _fft_real_part(xs):
    n = len(xs)
    if n == 1:
        return xs
    ev = _fft_real_part(xs[0::2])
    od = _fft_real_part(xs[1::2])
    out = [None] * n
    for k in range(n // 2):
        c = math.cos(2.0 * math.pi * k / n)
        s = -math.sin(2.0 * math.pi * k / n)
        orr, oi = od[k]
        if k == 0:
            tr, ti = orr, oi
        elif 4 * k == n:
            tr, ti = oi, -orr
        else:
            tr, ti = orr * c - oi * s, orr * s + oi * c
        er, ei = ev[k]
        out[k] = (er + tr, ei + ti)
        out[k + n // 2] = (er - tr, ei - ti)
    return out


def _fnet_seq_kernel(y_ref, g_ref, o_ref, z_scr):
    tc = o_ref.shape[2]
    for n2 in range(SEQ_SLABS):
        yb = y_ref[0, n2].reshape(2 * SEQ_PTS, tc)
        z_scr[n2] = _dot(g_ref[n2], yb)

    rb = 8
    n_lane = tc // 128

    def body(i, carry):
        r = i // n_lane
        c = i % n_lane
        r0 = pl.multiple_of(r * rb, rb)
        c0 = pl.multiple_of(c * 128, 128)
        xs = []
        for n2 in range(SEQ_SLABS):
            re = z_scr[n2, pl.ds(r0, rb), pl.ds(c0, 128)]
            nim = z_scr[n2, pl.ds(SEQ_PTS + r0, rb), pl.ds(c0, 128)]
            xs.append((re, -nim))
        out = _fft_real_part(xs)
        for k2 in range(SEQ_SLABS):
            o_ref[0, pl.ds(k2 * SEQ_PTS + r0, rb), pl.ds(c0, 128)] = out[k2][0]
        return carry

    lax.fori_loop(0, (SEQ_PTS // rb) * n_lane, body, 0)


def _fnet_sample(x, pos, mod, layer, gain, cs_ch, g_seq):
    b, seq, _ = x.shape
    tm = 512
    tc = 256
    y = pl.pallas_call(
        _fnet_slab_kernel,
        grid=(b, seq // tm),
        in_specs=[pl.BlockSpec((1, tm, D), lambda i, j: (i, j, 0)),
                  pl.BlockSpec((tm, D), lambda i, j: (j, 0)),
                  pl.BlockSpec((1, 1, 1, N_MOD * D), lambda i, j: (layer, 1 + i, 0, 0)),
                  pl.BlockSpec((1, D), lambda i, j: (0, 0)),
                  pl.BlockSpec((GD, 2 * GD), lambda i, j: (0, 0))],
        out_specs=pl.BlockSpec((1, SEQ_SLABS, 2, tm // SEQ_SLABS, D), lambda i, j: (i, 0, 0, j, 0)),
        out_shape=jax.ShapeDtypeStruct((b, SEQ_SLABS, 2, SEQ_PTS, D), BF16),
        scratch_shapes=[pltpu.VMEM((D // LANES, tm, LANES), F32), pltpu.VMEM((tm, D), F32)],
        compiler_params=_params("parallel", "parallel"),
        name="fnet_slab",
    )(x, pos, mod, gain, cs_ch)
    return pl.pallas_call(
        _fnet_seq_kernel,
        grid=(b, D // tc),
        in_specs=[pl.BlockSpec((1, SEQ_SLABS, 2, SEQ_PTS, tc), lambda i, j: (i, 0, 0, 0, j)),
                  pl.BlockSpec((SEQ_SLABS, 2 * SEQ_PTS, 2 * SEQ_PTS), lambda i, j: (0, 0, 0))],
        out_specs=pl.BlockSpec((1, seq, tc), lambda i, j: (i, 0, j)),
        out_shape=jax.ShapeDtypeStruct((b, seq, D), F32),
        scratch_shapes=[pltpu.VMEM((SEQ_SLABS, 2 * SEQ_PTS, tc), F32)],
        compiler_params=_params("parallel", "parallel"),
        name="fnet_seq",
    )(y, g_seq)


def _post_kernel(*refs, has_pos, final):
    it = iter(refs)
    x_ref = next(it)
    pos_ref = next(it) if has_pos else None
    a_ref, mod_ref, wo_ref, w1_ref, w2_ref, gmlp_ref, gnext_ref = (next(it) for _ in range(7))
    modn_ref = None if final else next(it)
    o_ref = next(it)
    hn_ref = None if final else next(it)
    x1_scr, h_scr, u_scr = (next(it) for _ in range(3))

    m = mod_ref[0, 0]
    subs = [slice(s * POST_SUB, (s + 1) * POST_SUB) for s in range(x_ref.shape[0] // POST_SUB)]

    for r in subs:
        x = x_ref[r, :]
        if has_pos:
            x = x + pos_ref[r, :]
        x1 = x + _mod_part(m, 2) * _dot(a_ref[r, :].astype(BF16), wo_ref[0])
        x1_scr[r, :] = x1
        h = _rms(x1, gmlp_ref[...]) * (1.0 + _mod_part(m, 4)) + _mod_part(m, 3)
        h_scr[r, :] = h.astype(BF16)

    for s, r in enumerate(subs):
        for j in range(D_FF // POST_TF):
            cols = slice(j * POST_TF, (j + 1) * POST_TF)
            u = jnp.maximum(_dot(h_scr[r, :], w1_ref[0, :, cols]), 0.0)
            u_scr[s, :, cols] = (u * u).astype(BF16)
        x2 = x1_scr[r, :] + _mod_part(m, 5) * _dot(u_scr[s], w2_ref[0])
        if final:
            o_ref[r, :] = _rms(x2, gnext_ref[...])
        else:
            o_ref[r, :] = x2
            mn = modn_ref[0, 0]
            hn = _rms(x2, gnext_ref[...]) * (1.0 + _mod_part(mn, 1)) + _mod_part(mn, 0)
            hn_ref[r, :] = hn.astype(BF16)


def _post_mixer(x, pos, a, mod, layer, mod_row, wo, w1, w2, g_mlp, g_next, final):
    rows = x.shape[0]
    tm = POST_TM
    has_pos = pos is not None
    n_pos = pos.shape[0] // tm if has_pos else 1
    row_map = lambda i: (i, 0)
    const = lambda i: (0, 0)
    resident = lambda shape: pl.BlockSpec(shape, lambda i: (layer, 0, 0), pipeline_mode=pl.Buffered(1))
    in_specs = [pl.BlockSpec((tm, D), row_map)]
    args = [x]
    if has_pos:
        in_specs.append(pl.BlockSpec((tm, D), lambda i: (i % n_pos, 0)))
        args.append(pos)
    in_specs += [pl.BlockSpec((tm, D), row_map),
                 pl.BlockSpec((1, 1, 1, N_MOD * D), lambda i: (layer, mod_row(i * tm), 0, 0)),
                 pl.BlockSpec((1, D, D), lambda i: (0, 0, 0), pipeline_mode=pl.Buffered(1)),
                 resident((1, D, D_FF)),
                 resident((1, D_FF, D)),
                 pl.BlockSpec((1, D), const),
                 pl.BlockSpec((1, D), const)]
    args += [a, mod, wo, w1, w2, g_mlp, g_next]
    out_specs = [pl.BlockSpec((tm, D), row_map)]
    out_shape = [jax.ShapeDtypeStruct((rows, D), F32)]
    if not final:
        in_specs.append(pl.BlockSpec((1, 1, 1, N_MOD * D),
                                     lambda i: (layer + 1, mod_row(i * tm), 0, 0)))
        args.append(mod)
        out_specs.append(pl.BlockSpec((tm, D), row_map))
        out_shape.append(jax.ShapeDtypeStruct((rows, D), BF16))
    res = pl.pallas_call(
        functools.partial(_post_kernel, has_pos=has_pos, final=final),
        grid=(rows // tm,),
        in_specs=in_specs,
        out_specs=out_specs,
        out_shape=out_shape,
        scratch_shapes=[pltpu.VMEM((tm, D), F32), pltpu.VMEM((tm, D), BF16),
                        pltpu.VMEM((tm // POST_SUB, POST_SUB, D_FF), BF16)],
        compiler_params=_params("parallel"),
        name="post_mlp_final" if final else "post_mlp",
    )(*args)
    return res[0] if final else (res[0], res[1])


PER_TILE = SCAN_TILE // CHUNK


def _decay_matrices():
    t = np.arange(SCAN_TILE)
    start = (t // CHUNK) * CHUNK
    first = np.arange(PER_TILE) * CHUNK
    same = (t[:, None] // CHUNK) == (t[None, :] // CHUNK)
    out = []
    for incl, mid, last in ((t[None, :] <= t[:, None], CHUNK // 2 - 1, CHUNK - 1),
                            (t[None, :] >= t[:, None], CHUNK // 2, 0)):
        bd = (same & incl).astype(np.float32)
        ref = bd[first + mid]
        pad = np.zeros((BF16_ROWS - 2 * PER_TILE, SCAN_TILE), np.float32)
        out.append(np.concatenate([bd - bd[start + mid], ref, bd[first + last] - ref, pad], axis=0))
    return jnp.asarray(np.stack(out), BF16)


def _hgrn_kernel(*refs, has_s0):
    it = iter(refs)
    hn_ref, w_ref, lb_ref, gn_ref, dm_ref = (next(it) for _ in range(5))
    s0_ref = next(it) if has_s0 else None
    o_ref, sf_ref = next(it), next(it)
    qd_scr, kd_scr, qb_scr, kl_scr, g_scr, o_scr = ((next(it), next(it)) for _ in range(6))
    ext_scr, q_scr, v_scr, sg_scr, st_scr, u_scr, sb_scr, flag_scr = (next(it) for _ in range(8))

    seq = hn_ref.shape[1]
    tp = SCAN_TILE
    n_tiles = seq // tp
    per_tile = PER_TILE

    lraw = lb_ref[...]
    mx = jnp.max(lraw, axis=1, keepdims=True)
    ex = jnp.exp(lraw - mx)
    sm = ex / jnp.sum(ex, axis=1, keepdims=True)
    lbv = (sm[:, 0, :] + sm[:, 1, :]) - sm[:, 0, :]

    def proj(i, carry):
        rows = pl.ds(pl.multiple_of(i * tp, tp), tp)
        p = _dot(hn_ref[0, rows, :], w_ref[0])
        q = _silu(p[:, 0:DK])
        q_scr[rows, :] = q
        v_scr[rows, :] = p[:, 3 * DK:4 * DK].astype(BF16)
        sg_scr[rows, :] = _silu(p[:, 4 * DK:5 * DK])
        unsafe = jnp.int32(0)
        for d in range(2):
            lb = lbv[d:d + 1, :]
            f = lb + (1.0 - lb) * jax.nn.sigmoid(p[:, (1 + d) * DK:(2 + d) * DK])
            k = 1.0 - f
            g = jnp.log(f)
            g_scr[d][rows, :] = g
            g1 = g.astype(BF16)
            g2 = (g - g1.astype(F32)).astype(BF16)
            r = _dot(dm_ref[d], jnp.concatenate([g1, g2], axis=1))
            r = r[:, :DK] + r[:, DK:]
            a = r[:tp]
            ext = r[tp:]
            ext_scr[d, i] = ext
            unsafe = unsafe | (jnp.max(jnp.abs(a)) > EXP_SAFE).astype(jnp.int32)
            qd = q * jnp.exp(a)
            kd = k * jnp.exp(-a)
            qd_scr[d][rows, :] = qd.astype(BF16)
            kd_scr[d][rows, :] = kd.astype(BF16)
            e_ref = jnp.exp(ext[0:per_tile])
            e_lr = jnp.exp(ext[per_tile:2 * per_tile])
            for c in range(per_tile):
                cr = pl.ds(pl.multiple_of(i * tp + c * CHUNK, CHUNK), CHUNK)
                cs = slice(c * CHUNK, (c + 1) * CHUNK)
                qb_scr[d][cr, :] = (qd[cs] * e_ref[c:c + 1]).astype(BF16)
                kl_scr[d][cr, :] = (kd[cs] * e_lr[c:c + 1]).astype(BF16)
        flag_scr[i] = unsafe
        return carry

    lax.fori_loop(0, n_tiles, proj, 0)

    for d in range(2):
        if has_s0:
            st_scr[d] = s0_ref[0, 0, d, 0].T
        else:
            st_scr[d] = jnp.zeros((DK, DK), F32)

    ti = lax.broadcasted_iota(jnp.int32, (tp, tp), 0)
    si = lax.broadcasted_iota(jnp.int32, (tp, tp), 1)
    same = (ti // CHUNK) == (si // CHUNK)
    masks = (same & (ti >= si), same & (si >= ti))

    def tile_pair(tiles):
        orders = (range(per_tile), range(per_tile - 1, -1, -1))
        r0 = [pl.multiple_of(t * tp, tp) for t in tiles]
        rows = [pl.ds(r, tp) for r in r0]
        chunk = lambda d, c: pl.ds(pl.multiple_of(r0[d] + c * CHUNK, CHUNK), CHUNK)
        vv = [v_scr[rows[d], :] for d in range(2)]
        for d in range(2):
            for c in range(per_tile):
                u_scr[d, c] = _dot_tn(vv[d][c * CHUNK:(c + 1) * CHUNK, :], kl_scr[d][chunk(d, c), :])
        sc = [jnp.where(masks[d], _dot_nt(qd_scr[d][rows[d], :], kd_scr[d][rows[d], :]), 0.0)
              for d in range(2)]
        for d in range(2):
            ext = ext_scr[d, tiles[d]]
            decay = jnp.exp(ext[0:per_tile] + ext[per_tile:2 * per_tile])
            st = st_scr[d]
            for c in orders[d]:
                sb_scr[d, c] = st.astype(BF16)
                st = st * decay[c:c + 1] + u_scr[d, c]
            st_scr[d] = st
        for d in range(2):
            o_scr[d][rows[d], :] = _dot(sc[d].astype(BF16), vv[d])
        for d in range(2):
            for c in range(per_tile):
                o_scr[d][chunk(d, c), :] += _dot_nt(qb_scr[d][chunk(d, c), :], sb_scr[d, c])

    rid = lax.broadcasted_iota(jnp.int32, (BF16_ROWS, DK), 0)

    def tile_dir_exact(d, tile, reverse):
        r0 = tile * tp

        def step(j, st):
            t = r0 + ((tp - 1 - j) if reverse else j)
            t0 = pl.multiple_of((t // BF16_ROWS) * BF16_ROWS, BF16_ROWS)
            grp = pl.ds(t0, BF16_ROWS)
            sel = rid == (t - t0)
            g = g_scr[d][grp, :]
            f = jnp.exp(jnp.sum(jnp.where(sel, g, 0.0), axis=0, keepdims=True))
            k = jnp.where(sel, 1.0 - jnp.exp(g), 0.0)
            st = st * f + _dot_tn(v_scr[grp, :], k.astype(BF16))
            o = _dot_nt(q_scr[grp, :].astype(BF16), st.astype(BF16))
            o_scr[d][grp, :] = jnp.where(sel, o, o_scr[d][grp, :])
            return st

        st_scr[d] = lax.fori_loop(0, tp, step, st_scr[d])

    def scan(i, carry):
        tf, tb = i, n_tiles - 1 - i
        unsafe = flag_scr[tf] | flag_scr[tb]

        @pl.when(unsafe == 0)
        def _():
            tile_pair((tf, tb))

        @pl.when(unsafe != 0)
        def _():
            tile_dir_exact(0, tf, False)
            tile_dir_exact(1, tb, True)

        return carry

    lax.fori_loop(0, n_tiles, scan, 0)

    def finish(i, carry):
        rows = pl.ds(pl.multiple_of(i * tp, tp), tp)
        o = o_scr[0][rows, :] + o_scr[1][rows, :]
        o = o * lax.rsqrt(jnp.mean(o * o, axis=-1, keepdims=True) + EPS) * gn_ref[...]
        o_ref[0, rows, :] = (o * sg_scr[rows, :]).astype(o_ref.dtype)
        return carry

    lax.fori_loop(0, n_tiles, finish, 0)
    for d in range(2):
        sf_ref[0, 0, d, 0] = st_scr[d].T


def _hgrn_mixer(hn, w_heads, lb_raw, g_norm, s0):
    b, seq, _ = hn.shape
    has_s0 = s0 is not None
    st_spec = pl.BlockSpec((1, 1, 2, 1, DK, DK), lambda i, h: (i, 0, 0, h, 0, 0))
    in_specs = [pl.BlockSpec((1, seq, D), lambda i, h: (i, 0, 0)),
                pl.BlockSpec((1, D, PROJ * DK), lambda i, h: (h, 0, 0)),
                pl.BlockSpec((2, 2, DK), lambda i, h: (0, 0, h)),
                pl.BlockSpec((1, DK), lambda i, h: (0, 0)),
                pl.BlockSpec((2, SCAN_TILE + BF16_ROWS, SCAN_TILE), lambda i, h: (0, 0, 0))]
    args = [hn, w_heads, lb_raw, g_norm, _decay_matrices()]
    if has_s0:
        in_specs.append(st_spec)
        args.append(s0)
    n_tiles = seq // SCAN_TILE
    bf16_rows = pltpu.VMEM((seq, DK), BF16)
    f32_rows = pltpu.VMEM((seq, DK), F32)
    return pl.pallas_call(
        functools.partial(_hgrn_kernel, has_s0=has_s0),
        grid=(b, HEADS),
        in_specs=in_specs,
        out_specs=[pl.BlockSpec((1, seq, DK), lambda i, h: (i, 0, h)), st_spec],
        out_shape=[jax.ShapeDtypeStruct((b, seq, D), BF16),
                   jax.ShapeDtypeStruct((b, 1, 2, HEADS, DK, DK), F32)],
        scratch_shapes=[bf16_rows] * 8 + [f32_rows] * 4 + [
                        pltpu.VMEM((2, n_tiles, BF16_ROWS, DK), F32),
                        f32_rows, bf16_rows, f32_rows,
                        pltpu.VMEM((2, DK, DK), F32),
                        pltpu.VMEM((2, PER_TILE, DK, DK), F32),
                        pltpu.VMEM((2, PER_TILE, DK, DK), BF16),
                        pltpu.SMEM((n_tiles,), jnp.int32)],
        compiler_params=_params("parallel", "arbitrary"),
        name="hgrn_scan",
    )(*args)


def kernel(x_prompt, x_sample, state_hgrn, c, c_ctx, ada_w, ada_b, norm_mix, norm_mlp, fnet_wo,
           hgrn_w_in, hgrn_lb, hgrn_norm, hgrn_wo, mlp_w1, mlp_w2, norm_final):
    bp, lp, _ = x_prompt.shape
    bs, ls, _ = x_sample.shape
    assert ada_w.shape[0] == 2 and ls == SEQ_PTS * SEQ_SLABS and lp == GD

    cs_ch, t_ctx, g_seq = _dft_tables()
    pos = _grid_pos_embed(ls, D)

    cond8 = jnp.zeros((8, D), F32).at[0].set(c_ctx).at[1:1 + bs].set(c)
    mod = _modulation(cond8, ada_w, ada_b)

    wo0 = fnet_wo.astype(BF16)
    wo1 = hgrn_wo.astype(BF16)
    w1 = mlp_w1.astype(BF16)
    w2 = mlp_w2.astype(BF16)
    w_heads = (hgrn_w_in[0].reshape(D, PROJ, HEADS, DK).transpose(2, 0, 1, 3)
               .reshape(HEADS, D, PROJ * DK).astype(BF16))
    gm = norm_mix.reshape(-1, 1, D)
    gf = norm_mlp.reshape(-1, 1, D)
    g_fin = norm_final.reshape(1, D)
    g_hn = hgrn_norm[0].reshape(1, DK)

    ctx_row = lambda r: 0
    smp_row = lambda r: 1 + r // ls

    mixed = _fnet_ctx(x_prompt, mod, 0, gm[0], cs_ch, t_ctx)
    x2, hn = _post_mixer(x_prompt.reshape(bp * lp, D), None, mixed.reshape(bp * lp, D), mod, 0,
                         ctx_row, wo0, w1, w2, gf[0], gm[1], final=False)
    o, new_state = _hgrn_mixer(hn.reshape(bp, lp, D), w_heads, hgrn_lb, g_hn, None)
    y_prompt = _post_mixer(x2, None, o.reshape(bp * lp, D), mod, 1, ctx_row, wo1, w1, w2,
                           gf[1], g_fin, final=True).reshape(bp, lp, D)

    mixed = _fnet_sample(x_sample, pos, mod, 0, gm[0], cs_ch, g_seq)
    x2, hn = _post_mixer(x_sample.reshape(bs * ls, D), pos, mixed.reshape(bs * ls, D), mod, 0,
                         smp_row, wo0, w1, w2, gf[0], gm[1], final=False)
    o, _ = _hgrn_mixer(hn.reshape(bs, ls, D), w_heads, hgrn_lb, g_hn, state_hgrn)
    y_sample = _post_mixer(x2, None, o.reshape(bs * ls, D), mod, 1, smp_row, wo1, w1, w2,
                           gf[1], g_fin, final=True).reshape(bs, ls, D)
    return (y_prompt, y_sample, new_state)
```

```python
import functools
import math

import numpy as np
import jax
import jax.numpy as jnp
from jax import lax
from jax.experimental import pallas as pl
from jax.experimental.pallas import tpu as pltpu

F32 = jnp.float32
BF16 = jnp.bfloat16

D = 1024
N_MOD = 6
D_FF = 4 * D
EPS = 1e-6
GROUPS = 4
GD = D // GROUPS
HEADS = 8
DK = 128
GRID_W = 64
POS_BASE = 10000.0
PROJ = 5

SEQ_SLABS = 16
SEQ_PTS = 256
LANES = 128
BF16_ROWS = 16
CHUNK = 64
SCAN_TILE = 256
EXP_SAFE = 80.0
POST_TM = 512
POST_SUB = 256
POST_TF = 1024

VMEM_LIMIT = 56 * 1024 * 1024


def _dft_tables():
    n = np.arange(GD)
    ang = 2.0 * np.pi * ((n[:, None] * n[None, :]) % GD) / GD
    cs_ch = np.concatenate([np.cos(ang), np.sin(ang)], axis=1) / math.sqrt(GD)
    t_ctx = np.concatenate([np.cos(ang), -np.sin(ang)], axis=1) / math.sqrt(GD)
    length = SEQ_PTS * SEQ_SLABS
    k1 = np.arange(SEQ_PTS)[:, None]
    n1 = np.arange(SEQ_PTS)[None, :]
    blocks = []
    for n2 in range(SEQ_SLABS):
        a = 2.0 * np.pi * ((k1 * (SEQ_SLABS * n1 + n2)) % length) / length
        gc, gs = np.cos(a) / math.sqrt(length), np.sin(a) / math.sqrt(length)
        blocks.append(np.block([[gc, -gs], [gs, gc]]))
    g_seq = np.stack(blocks)
    return tuple(jnp.asarray(t, F32).astype(BF16) for t in (cs_ch, t_ctx, g_seq))


def _grid_pos_embed(n_tok, d):
    rows = n_tok // GRID_W
    quarter = d // 4
    omega = 1.0 / (POS_BASE ** (jnp.arange(quarter, dtype=F32) / quarter))
    r = jnp.arange(rows, dtype=F32)[:, None] * omega[None, :]
    cl = jnp.arange(GRID_W, dtype=F32)[:, None] * omega[None, :]
    er = jnp.concatenate([jnp.sin(r), jnp.cos(r)], axis=-1)
    ec = jnp.concatenate([jnp.sin(cl), jnp.cos(cl)], axis=-1)
    emb = jnp.concatenate([jnp.broadcast_to(er[:, None, :], (rows, GRID_W, d // 2)),
                           jnp.broadcast_to(ec[None, :, :], (rows, GRID_W, d // 2))], axis=-1)
    return emb.reshape(n_tok, d)


def _silu(x):
    return x * jax.nn.sigmoid(x)


def _rms(x, g):
    return x * lax.rsqrt(jnp.mean(x * x, axis=-1, keepdims=True) + EPS) * g


def _mod_part(m, idx):
    return m[:, idx * D:(idx + 1) * D]


def _dot(a, b):
    return jnp.dot(a, b, preferred_element_type=F32)


def _dot_nt(a, b):
    return lax.dot_general(a, b, (((1,), (1,)), ((), ())), preferred_element_type=F32)


def _dot_tn(a, b):
    return lax.dot_general(a, b, (((0,), (0,)), ((), ())), preferred_element_type=F32)


def _params(*sem):
    return pltpu.CompilerParams(dimension_semantics=sem, vmem_limit_bytes=VMEM_LIMIT)


def _mod_kernel(c_ref, w_ref, b_ref, o_ref):
    s = _silu(c_ref[...])
    o_ref[0] = _dot(s.astype(BF16), w_ref[0].astype(BF16)) + b_ref[0]


def _modulation(cond8, ada_w, ada_b):
    depth = ada_w.shape[0]
    tn = 1536
    out = pl.pallas_call(
        _mod_kernel,
        grid=(depth, N_MOD * D // tn),
        in_specs=[pl.BlockSpec((8, D), lambda l, j: (0, 0)),
                  pl.BlockSpec((1, D, tn), lambda l, j: (l, 0, j)),
                  pl.BlockSpec((1, 1, tn), lambda l, j: (l, 0, j))],
        out_specs=pl.BlockSpec((1, 8, tn), lambda l, j: (l, 0, j)),
        out_shape=jax.ShapeDtypeStruct((depth, 8, N_MOD * D), F32),
        compiler_params=_params("parallel", "parallel"),
        name="adaln_mod",
    )(cond8, ada_w, ada_b.reshape(depth, 1, N_MOD * D))
    return out.reshape(depth, 8, 1, N_MOD * D)


def _channel_dft(hb, cs_ref, put):
    for g in range(GROUPS):
        y = _dot(hb[:, g * GD:(g + 1) * GD], cs_ref[...])
        put(0, g, y[:, :GD].astype(BF16))
        put(1, g, y[:, GD:].astype(BF16))


def _fnet_ctx_kernel(x_ref, mod_ref, g_ref, cs_ref, t_ref, o_ref, y_scr):
    m = mod_ref[0, 0]
    h = _rms(x_ref[0], g_ref[...]) * (1.0 + _mod_part(m, 1)) + _mod_part(m, 0)
    seq = x_ref.shape[1]

    def put(kind, g, val):
        y_scr[kind * seq:(kind + 1) * seq, g * GD:(g + 1) * GD] = val

    _channel_dft(h.astype(BF16), cs_ref, put)
    o_ref[0] = _dot(t_ref[...], y_scr[...]).astype(o_ref.dtype)


def _fnet_ctx(x, mod, layer, gain, cs_ch, t_ctx):
    b, seq, _ = x.shape
    return pl.pallas_call(
        _fnet_ctx_kernel,
        grid=(b,),
        in_specs=[pl.BlockSpec((1, seq, D), lambda i: (i, 0, 0)),
                  pl.BlockSpec((1, 1, 1, N_MOD * D), lambda i: (layer, 0, 0, 0)),
                  pl.BlockSpec((1, D), lambda i: (0, 0)),
                  pl.BlockSpec((GD, 2 * GD), lambda i: (0, 0)),
                  pl.BlockSpec((seq, 2 * seq), lambda i: (0, 0))],
        out_specs=pl.BlockSpec((1, seq, D), lambda i: (i, 0, 0)),
        out_shape=jax.ShapeDtypeStruct((b, seq, D), BF16),
        scratch_shapes=[pltpu.VMEM((2 * seq, D), BF16)],
        compiler_params=_params("parallel"),
        name="fnet_ctx",
    )(x, mod, gain, cs_ch, t_ctx)


def _fnet_slab_kernel(x_ref, pos_ref, mod_ref, g_ref, cs_ref, y_ref, col_scr, xp_scr):
    tm = x_ref.shape[1]
    rows = tm // SEQ_SLABS
    n_col = D // LANES
    for j in range(n_col):
        cols = slice(j * LANES, (j + 1) * LANES)
        col_scr[j] = x_ref[0, :, cols] + pos_ref[:, cols]
    for n2 in range(SEQ_SLABS):
        for j in range(n_col):
            xp_scr[n2 * rows:(n2 + 1) * rows, j * LANES:(j + 1) * LANES] = (
                col_scr[j, pl.ds(n2, rows, stride=SEQ_SLABS), :])
    m = mod_ref[0, 0]
    h = _rms(xp_scr[...], g_ref[...]) * (1.0 + _mod_part(m, 1)) + _mod_part(m, 0)

    def put(kind, g, val):
        for n2 in range(SEQ_SLABS):
            y_ref[0, n2, kind, :, g * GD:(g + 1) * GD] = val[n2 * rows:(n2 + 1) * rows, :]

    _channel_dft(h.astype(BF16), cs_ref, put)


def _fft_real_part(xs):
    n = len(xs)
    if n == 1:
        return xs
    ev = _fft_real_part(xs[0::2])
    od = _fft_real_part(xs[1::2])
    out = [None] * n
    for k in range(n // 2):
        c = math.cos(2.0 * math.pi * k / n)
        s = -math.sin(2.0 * math.pi * k / n)
        orr, oi = od[k]
        if k == 0:
            tr, ti = orr, oi
        elif 4 * k == n:
            tr, ti = oi, -orr
        else:
            tr, ti = orr * c - oi * s, orr * s + oi * c
        er, ei = ev[k]
        out[k] = (er + tr, ei + ti)
        out[k + n // 2] = (er - tr, ei - ti)
    return out


def _fnet_seq_kernel(y_ref, g_ref, o_ref, z_scr):
    tc = o_ref.shape[2]
    for n2 in range(SEQ_SLABS):
        yb = y_ref[0, n2].reshape(2 * SEQ_PTS, tc)
        z_scr[n2] = _dot(g_ref[n2], yb)

    rb = 8
    n_lane = tc // 128

    def body(i, carry):
        r = i // n_lane
        c = i % n_lane
        r0 = pl.multiple_of(r * rb, rb)
        c0 = pl.multiple_of(c * 128, 128)
        xs = []
        for n2 in range(SEQ_SLABS):
            re = z_scr[n2, pl.ds(r0, rb), pl.ds(c0, 128)]
            nim = z_scr[n2, pl.ds(SEQ_PTS + r0, rb), pl.ds(c0, 128)]
            xs.append((re, -nim))
        out = _fft_real_part(xs)
        for k2 in range(SEQ_SLABS):
            o_ref[0, pl.ds(k2 * SEQ_PTS + r0, rb), pl.ds(c0, 128)] = out[k2][0]
        return carry

    lax.fori_loop(0, (SEQ_PTS // rb) * n_lane, body, 0)


def _fnet_sample(x, pos, mod, layer, gain, cs_ch, g_seq):
    b, seq, _ = x.shape
    tm = 512
    tc = 256
    y = pl.pallas_call(
        _fnet_slab_kernel,
        grid=(b, seq // tm),
        in_specs=[pl.BlockSpec((1, tm, D), lambda i, j: (i, j, 0)),
                  pl.BlockSpec((tm, D), lambda i, j: (j, 0)),
                  pl.BlockSpec((1, 1, 1, N_MOD * D), lambda i, j: (layer, 1 + i, 0, 0)),
                  pl.BlockSpec((1, D), lambda i, j: (0, 0)),
                  pl.BlockSpec((GD, 2 * GD), lambda i, j: (0, 0))],
        out_specs=pl.BlockSpec((1, SEQ_SLABS, 2, tm // SEQ_SLABS, D), lambda i, j: (i, 0, 0, j, 0)),
        out_shape=jax.ShapeDtypeStruct((b, SEQ_SLABS, 2, SEQ_PTS, D), BF16),
        scratch_shapes=[pltpu.VMEM((D // LANES, tm, LANES), F32), pltpu.VMEM((tm, D), F32)],
        compiler_params=_params("parallel", "parallel"),
        name="fnet_slab",
    )(x, pos, mod, gain, cs_ch)
    return pl.pallas_call(
        _fnet_seq_kernel,
        grid=(b, D // tc),
        in_specs=[pl.BlockSpec((1, SEQ_SLABS, 2, SEQ_PTS, tc), lambda i, j: (i, 0, 0, 0, j)),
                  pl.BlockSpec((SEQ_SLABS, 2 * SEQ_PTS, 2 * SEQ_PTS), lambda i, j: (0, 0, 0))],
        out_specs=pl.BlockSpec((1, seq, tc), lambda i, j: (i, 0, j)),
        out_shape=jax.ShapeDtypeStruct((b, seq, D), F32),
        scratch_shapes=[pltpu.VMEM((SEQ_SLABS, 2 * SEQ_PTS, tc), F32)],
        compiler_params=_params("parallel", "parallel"),
        name="fnet_seq",
    )(y, g_seq)


def _post_kernel(*refs, has_pos, final):
    it = iter(refs)
    x_ref = next(it)
    pos_ref = next(it) if has_pos else None
    a_ref, mod_ref, wo_ref, w1_ref, w2_ref, gmlp_ref, gnext_ref = (next(it) for _ in range(7))
    modn_ref = None if final else next(it)
    o_ref = next(it)
    hn_ref = None if final else next(it)
    x1_scr, h_scr, u_scr = (next(it) for _ in range(3))

    m = mod_ref[0, 0]
    subs = [slice(s * POST_SUB, (s + 1) * POST_SUB) for s in range(x_ref.shape[0] // POST_SUB)]

    for r in subs:
        x = x_ref[r, :]
        if has_pos:
            x = x + pos_ref[r, :]
        x1 = x + _mod_part(m, 2) * _dot(a_ref[r, :].astype(BF16), wo_ref[0])
        x1_scr[r, :] = x1
        h = _rms(x1, gmlp_ref[...]) * (1.0 + _mod_part(m, 4)) + _mod_part(m, 3)
        h_scr[r, :] = h.astype(BF16)

    for s, r in enumerate(subs):
        for j in range(D_FF // POST_TF):
            cols = slice(j * POST_TF, (j + 1) * POST_TF)
            u = jnp.maximum(_dot(h_scr[r, :], w1_ref[0, :, cols]), 0.0)
            u_scr[s, :, cols] = (u * u).astype(BF16)
        x2 = x1_scr[r, :] + _mod_part(m, 5) * _dot(u_scr[s], w2_ref[0])
        if final:
            o_ref[r, :] = _rms(x2, gnext_ref[...])
        else:
            o_ref[r, :] = x2
            mn = modn_ref[0, 0]
            hn = _rms(x2, gnext_ref[...]) * (1.0 + _mod_part(mn, 1)) + _mod_part(mn, 0)
            hn_ref[r, :] = hn.astype(BF16)


def _post_mixer(x, pos, a, mod, layer, mod_row, wo, w1, w2, g_mlp, g_next, final):
    rows = x.shape[0]
    tm = POST_TM
    has_pos = pos is not None
    n_pos = pos.shape[0] // tm if has_pos else 1
    row_map = lambda i: (i, 0)
    const = lambda i: (0, 0)
    resident = lambda shape: pl.BlockSpec(shape, lambda i: (layer, 0, 0), pipeline_mode=pl.Buffered(1))
    in_specs = [pl.BlockSpec((tm, D), row_map)]
    args = [x]
    if has_pos:
        in_specs.append(pl.BlockSpec((tm, D), lambda i: (i % n_pos, 0)))
        args.append(pos)
    in_specs += [pl.BlockSpec((tm, D), row_map),
                 pl.BlockSpec((1, 1, 1, N_MOD * D), lambda i: (layer, mod_row(i * tm), 0, 0)),
                 pl.BlockSpec((1, D, D), lambda i: (0, 0, 0), pipeline_mode=pl.Buffered(1)),
                 resident((1, D, D_FF)),
                 resident((1, D_FF, D)),
                 pl.BlockSpec((1, D), const),
                 pl.BlockSpec((1, D), const)]
    args += [a, mod, wo, w1, w2, g_mlp, g_next]
    out_specs = [pl.BlockSpec((tm, D), row_map)]
    out_shape = [jax.ShapeDtypeStruct((rows, D), F32)]
    if not final:
        in_specs.append(pl.BlockSpec((1, 1, 1, N_MOD * D),
                                     lambda i: (layer + 1, mod_row(i * tm), 0, 0)))
        args.append(mod)
        out_specs.append(pl.BlockSpec((tm, D), row_map))
        out_shape.append(jax.ShapeDtypeStruct((rows, D), BF16))
    res = pl.pallas_call(
        functools.partial(_post_kernel, has_pos=has_pos, final=final),
        grid=(rows // tm,),
        in_specs=in_specs,
        out_specs=out_specs,
        out_shape=out_shape,
        scratch_shapes=[pltpu.VMEM((tm, D), F32), pltpu.VMEM((tm, D), BF16),
                        pltpu.VMEM((tm // POST_SUB, POST_SUB, D_FF), BF16)],
        compiler_params=_params("parallel"),
        name="post_mlp_final" if final else "post_mlp",
    )(*args)
    return res[0] if final else (res[0], res[1])


PER_TILE = SCAN_TILE // CHUNK


def _decay_matrices():
    t = np.arange(SCAN_TILE)
    start = (t // CHUNK) * CHUNK
    first = np.arange(PER_TILE) * CHUNK
    same = (t[:, None] // CHUNK) == (t[None, :] // CHUNK)
    out = []
    for incl, mid, last in ((t[None, :] <= t[:, None], CHUNK // 2 - 1, CHUNK - 1),
                            (t[None, :] >= t[:, None], CHUNK // 2, 0)):
        bd = (same & incl).astype(np.float32)
        ref = bd[first + mid]
        pad = np.zeros((BF16_ROWS - 2 * PER_TILE, SCAN_TILE), np.float32)
        out.append(np.concatenate([bd - bd[start + mid], ref, bd[first + last] - ref, pad], axis=0))
    return jnp.asarray(np.stack(out), BF16)


def _hgrn_kernel(*refs, has_s0):
    it = iter(refs)
    hn_ref = next(it)
    wp_refs = [next(it) for _ in range(PROJ)]
    lb_ref, gn_ref, dm_ref = (next(it) for _ in range(3))
    s0_ref = next(it) if has_s0 else None
    o_ref, sf_ref = next(it), next(it)
    qd_scr, kd_scr, qb_scr, kl_scr, g_scr, o_scr, k_scr = ((next(it), next(it)) for _ in range(7))
    ext_scr, q_scr, v_scr, sg_scr, st_scr, u_scr, sb_scr, gs_scr, w_scr, flag_scr = (
        next(it) for _ in range(10))

    @pl.when(pl.program_id(1) == 0)
    def _():
        for j, wp_ref in enumerate(wp_refs):
            w_scr[:, j * DK:(j + 1) * DK] = wp_ref[...].astype(BF16)

    seq = hn_ref.shape[1]
    tp = SCAN_TILE
    n_tiles = seq // tp
    per_tile = PER_TILE

    lraw = lb_ref[...]
    mx = jnp.max(lraw, axis=1, keepdims=True)
    ex = jnp.exp(lraw - mx)
    sm = ex / jnp.sum(ex, axis=1, keepdims=True)
    lbv = (sm[:, 0, :] + sm[:, 1, :]) - sm[:, 0, :]

    def gates(i):
        rows = pl.ds(pl.multiple_of(i * tp, tp), tp)
        p = _dot(hn_ref[0, rows, :], w_scr[...])
        q_scr[rows, :] = _silu(p[:, 0:DK])
        v_scr[rows, :] = p[:, 3 * DK:4 * DK].astype(BF16)
        sg_scr[rows, :] = _silu(p[:, 4 * DK:5 * DK])
        for d in range(2):
            lb = lbv[d:d + 1, :]
            f = lb + (1.0 - lb) * jax.nn.sigmoid(p[:, (1 + d) * DK:(2 + d) * DK])
            k_scr[d][rows, :] = 1.0 - f
            g = jnp.log(f)
            g_scr[d][rows, :] = g
            g1 = g.astype(BF16)
            gs_scr[d, :, 0:DK] = g1
            gs_scr[d, :, DK:2 * DK] = (g - g1.astype(F32)).astype(BF16)

    def decay_sums():
        return [_dot(dm_ref[d], gs_scr[d]) for d in range(2)]

    def factors(i, sums):
        rows = pl.ds(pl.multiple_of(i * tp, tp), tp)
        q = q_scr[rows, :]
        unsafe = jnp.int32(0)
        for d in range(2):
            r = sums[d][:, :DK] + sums[d][:, DK:]
            a = r[:tp]
            ext = r[tp:]
            ext_scr[d, i] = ext
            unsafe = unsafe | (jnp.max(jnp.abs(a)) > EXP_SAFE).astype(jnp.int32)
            qd = q * jnp.exp(a)
            kd = k_scr[d][rows, :] * jnp.exp(-a)
            qd_scr[d][rows, :] = qd.astype(BF16)
            kd_scr[d][rows, :] = kd.astype(BF16)
            e_ref = jnp.exp(ext[0:per_tile])
            e_lr = jnp.exp(ext[per_tile:2 * per_tile])
            for c in range(per_tile):
                cr = pl.ds(pl.multiple_of(i * tp + c * CHUNK, CHUNK), CHUNK)
                cs = slice(c * CHUNK, (c + 1) * CHUNK)
                qb_scr[d][cr, :] = (qd[cs] * e_ref[c:c + 1]).astype(BF16)
                kl_scr[d][cr, :] = (kd[cs] * e_lr[c:c + 1]).astype(BF16)
        flag_scr[i] = unsafe

    def proj(i, carry):
        sums = decay_sums()
        gates(i + 1)
        factors(i, sums)
        return carry

    gates(0)
    lax.fori_loop(0, n_tiles - 1, proj, 0)
    factors(n_tiles - 1, decay_sums())

    for d in range(2):
        if has_s0:
            st_scr[d] = s0_ref[0, 0, d, 0].T
        else:
            st_scr[d] = jnp.zeros((DK, DK), F32)

    ti = lax.broadcasted_iota(jnp.int32, (tp, tp), 0)
    si = lax.broadcasted_iota(jnp.int32, (tp, tp), 1)
    same = (ti // CHUNK) == (si // CHUNK)
    masks = (same & (ti >= si), same & (si >= ti))

    def tile_pair(tiles):
        orders = (range(per_tile), range(per_tile - 1, -1, -1))
        r0 = [pl.multiple_of(t * tp, tp) for t in tiles]
        rows = [pl.ds(r, tp) for r in r0]
        chunk = lambda d, c: pl.ds(pl.multiple_of(r0[d] + c * CHUNK, CHUNK), CHUNK)
        vv = [v_scr[rows[d], :] for d in range(2)]
        for d in range(2):
            for c in range(per_tile):
                u_scr[d, c] = _dot_tn(vv[d][c * CHUNK:(c + 1) * CHUNK, :], kl_scr[d][chunk(d, c), :])
        sc = [jnp.where(masks[d], _dot_nt(qd_scr[d][rows[d], :], kd_scr[d][rows[d], :]), 0.0)
              for d in range(2)]
        for d in range(2):
            ext = ext_scr[d, tiles[d]]
            decay = jnp.exp(ext[0:per_tile] + ext[per_tile:2 * per_tile])
            st = st_scr[d]
            for c in orders[d]:
                sb_scr[d, c] = st.astype(BF16)
                st = st * decay[c:c + 1] + u_scr[d, c]
            st_scr[d] = st
        for d in range(2):
            o_scr[d][rows[d], :] = _dot(sc[d].astype(BF16), vv[d])
        for d in range(2):
            for c in range(per_tile):
                o_scr[d][chunk(d, c), :] += _dot_nt(qb_scr[d][chunk(d, c), :], sb_scr[d, c])

    rid = lax.broadcasted_iota(jnp.int32, (BF16_ROWS, DK), 0)

    def tile_dir_exact(d, tile, reverse):
        r0 = tile * tp

        def step(j, st):
            t = r0 + ((tp - 1 - j) if reverse else j)
            t0 = pl.multiple_of((t // BF16_ROWS) * BF16_ROWS, BF16_ROWS)
            grp = pl.ds(t0, BF16_ROWS)
            sel = rid == (t - t0)
            g = g_scr[d][grp, :]
            f = jnp.exp(jnp.sum(jnp.where(sel, g, 0.0), axis=0, keepdims=True))
            k = jnp.where(sel, 1.0 - jnp.exp(g), 0.0)
            st = st * f + _dot_tn(v_scr[grp, :], k.astype(BF16))
            o = _dot_nt(q_scr[grp, :].astype(BF16), st.astype(BF16))
            o_scr[d][grp, :] = jnp.where(sel, o, o_scr[d][grp, :])
            return st

        st_scr[d] = lax.fori_loop(0, tp, step, st_scr[d])

    def scan(i, carry):
        tf, tb = i, n_tiles - 1 - i
        unsafe = flag_scr[tf] | flag_scr[tb]

        @pl.when(unsafe == 0)
        def _():
            tile_pair((tf, tb))

        @pl.when(unsafe != 0)
        def _():
            tile_dir_exact(0, tf, False)
            tile_dir_exact(1, tb, True)

        return carry

    lax.fori_loop(0, n_tiles, scan, 0)

    def finish(i, carry):
        rows = pl.ds(pl.multiple_of(i * tp, tp), tp)
        o = o_scr[0][rows, :] + o_scr[1][rows, :]
        o = o * lax.rsqrt(jnp.mean(o * o, axis=-1, keepdims=True) + EPS) * gn_ref[...]
        o_ref[0, rows, :] = (o * sg_scr[rows, :]).astype(o_ref.dtype)
        return carry

    lax.fori_loop(0, n_tiles, finish, 0)
    for d in range(2):
        sf_ref[0, 0, d, 0] = st_scr[d].T


def _hgrn_mixer(hn, w_in, lb_raw, g_norm, s0):
    b, seq, _ = hn.shape
    has_s0 = s0 is not None
    st_spec = pl.BlockSpec((1, 1, 2, 1, DK, DK), lambda h, i: (i, 0, 0, h, 0, 0))
    in_specs = [pl.BlockSpec((1, seq, D), lambda h, i: (i, 0, 0))]
    in_specs += [pl.BlockSpec((D, DK), functools.partial(lambda j, h, i: (0, j * HEADS + h), j))
                 for j in range(PROJ)]
    in_specs += [pl.BlockSpec((2, 2, DK), lambda h, i: (0, 0, h)),
                 pl.BlockSpec((1, DK), lambda h, i: (0, 0)),
                 pl.BlockSpec((2, SCAN_TILE + BF16_ROWS, SCAN_TILE), lambda h, i: (0, 0, 0))]
    args = [hn] + [w_in] * PROJ + [lb_raw, g_norm, _decay_matrices()]
    if has_s0:
        in_specs.append(st_spec)
        args.append(s0)
    n_tiles = seq // SCAN_TILE
    bf16_rows = pltpu.VMEM((seq, DK), BF16)
    f32_rows = pltpu.VMEM((seq, DK), F32)
    return pl.pallas_call(
        functools.partial(_hgrn_kernel, has_s0=has_s0),
        grid=(HEADS, b),
        in_specs=in_specs,
        out_specs=[pl.BlockSpec((1, seq, DK), lambda h, i: (i, 0, h)), st_spec],
        out_shape=[jax.ShapeDtypeStruct((b, seq, D), BF16),
                   jax.ShapeDtypeStruct((b, 1, 2, HEADS, DK, DK), F32)],
        scratch_shapes=[bf16_rows] * 8 + [f32_rows] * 6 + [
                        pltpu.VMEM((2, n_tiles, BF16_ROWS, DK), F32),
                        f32_rows, bf16_rows, f32_rows,
                        pltpu.VMEM((2, DK, DK), F32),
                        pltpu.VMEM((2, PER_TILE, DK, DK), F32),
                        pltpu.VMEM((2, PER_TILE, DK, DK), BF16),
                        pltpu.VMEM((2, SCAN_TILE, 2 * DK), BF16),
                        pltpu.VMEM((D, PROJ * DK), BF16),
                        pltpu.SMEM((n_tiles,), jnp.int32)],
        compiler_params=_params("parallel", "arbitrary"),
        name="hgrn_scan",
    )(*args)


def kernel(x_prompt, x_sample, state_hgrn, c, c_ctx, ada_w, ada_b, norm_mix, norm_mlp, fnet_wo,
           hgrn_w_in, hgrn_lb, hgrn_norm, hgrn_wo, mlp_w1, mlp_w2, norm_final):
    bp, lp, _ = x_prompt.shape
    bs, ls, _ = x_sample.shape
    assert ada_w.shape[0] == 2 and ls == SEQ_PTS * SEQ_SLABS and lp == GD

    cs_ch, t_ctx, g_seq = _dft_tables()
    pos = _grid_pos_embed(ls, D)

    cond8 = jnp.zeros((8, D), F32).at[0].set(c_ctx).at[1:1 + bs].set(c)
    mod = _modulation(cond8, ada_w, ada_b)

    wo0 = fnet_wo.astype(BF16)
    wo1 = hgrn_wo.astype(BF16)
    w1 = mlp_w1.astype(BF16)
    w2 = mlp_w2.astype(BF16)
    w_heads = hgrn_w_in[0]
    gm = norm_mix.reshape(-1, 1, D)
    gf = norm_mlp.reshape(-1, 1, D)
    g_fin = norm_final.reshape(1, D)
    g_hn = hgrn_norm[0].reshape(1, DK)

    ctx_row = lambda r: 0
    smp_row = lambda r: 1 + r // ls

    mixed = _fnet_ctx(x_prompt, mod, 0, gm[0], cs_ch, t_ctx)
    x2, hn = _post_mixer(x_prompt.reshape(bp * lp, D), None, mixed.reshape(bp * lp, D), mod, 0,
                         ctx_row, wo0, w1, w2, gf[0], gm[1], final=False)
    o, new_state = _hgrn_mixer(hn.reshape(bp, lp, D), w_heads, hgrn_lb, g_hn, None)
    y_prompt = _post_mixer(x2, None, o.reshape(bp * lp, D), mod, 1, ctx_row, wo1, w1, w2,
                           gf[1], g_fin, final=True).reshape(bp, lp, D)

    mixed = _fnet_sample(x_sample, pos, mod, 0, gm[0], cs_ch, g_seq)
    x2, hn = _post_mixer(x_sample.reshape(bs * ls, D), pos, mixed.reshape(bs * ls, D), mod, 0,
                         smp_row, wo0, w1, w2, gf[0], gm[1], final=False)
    o, _ = _hgrn_mixer(hn.reshape(bs, ls, D), w_heads, hgrn_lb, g_hn, state_hgrn)
    y_sample = _post_mixer(x2, None, o.reshape(bs * ls, D), mod, 1, smp_row, wo1, w1, w2,
                           gf[1], g_fin, final=True).reshape(bs, ls, D)
    return (y_prompt, y_sample, new_state)
```

```python
import functools
import math

import numpy as np
import jax
import jax.numpy as jnp
from jax import lax
from jax.experimental import pallas as pl
from jax.experimental.pallas import tpu as pltpu

F32 = jnp.float32
BF16 = jnp.bfloat16

D = 1024
N_MOD = 6
D_FF = 4 * D
EPS = 1e-6
GROUPS = 4
GD = D // GROUPS
HEADS = 8
DK = 128
GRID_W = 64
POS_BASE = 10000.0
PROJ = 5

SEQ_SLABS = 16
SEQ_PTS = 256
LANES = 128
BF16_ROWS = 16
CHUNK = 64
SCAN_TILE = 256
EXP_SAFE = 80.0
POST_TM = 512
POST_SUB = 256
POST_TF = 1024

VMEM_LIMIT = 56 * 1024 * 1024


def _dft_tables():
    n = np.arange(GD)
    ang = 2.0 * np.pi * ((n[:, None] * n[None, :]) % GD) / GD
    cs_ch = np.concatenate([np.cos(ang), np.sin(ang)], axis=1) / math.sqrt(GD)
    t_ctx = np.concatenate([np.cos(ang), -np.sin(ang)], axis=1) / math.sqrt(GD)
    length = SEQ_PTS * SEQ_SLABS
    k1 = np.arange(SEQ_PTS)[:, None]
    n1 = np.arange(SEQ_PTS)[None, :]
    blocks = []
    for n2 in range(SEQ_SLABS):
        a = 2.0 * np.pi * ((k1 * (SEQ_SLABS * n1 + n2)) % length) / length
        gc, gs = np.cos(a) / math.sqrt(length), np.sin(a) / math.sqrt(length)
        blocks.append(np.block([[gc, -gs], [gs, gc]]))
    g_seq = np.stack(blocks)
    return tuple(jnp.asarray(t, F32).astype(BF16) for t in (cs_ch, t_ctx, g_seq))


def _grid_pos_embed(n_tok, d):
    rows = n_tok // GRID_W
    quarter = d // 4
    omega = 1.0 / (POS_BASE ** (jnp.arange(quarter, dtype=F32) / quarter))
    r = jnp.arange(rows, dtype=F32)[:, None] * omega[None, :]
    cl = jnp.arange(GRID_W, dtype=F32)[:, None] * omega[None, :]
    er = jnp.concatenate([jnp.sin(r), jnp.cos(r)], axis=-1)
    ec = jnp.concatenate([jnp.sin(cl), jnp.cos(cl)], axis=-1)
    emb = jnp.concatenate([jnp.broadcast_to(er[:, None, :], (rows, GRID_W, d // 2)),
                           jnp.broadcast_to(ec[None, :, :], (rows, GRID_W, d // 2))], axis=-1)
    return emb.reshape(n_tok, d)


def _silu(x):
    return x * jax.nn.sigmoid(x)


def _rms(x, g):
    return x * lax.rsqrt(jnp.mean(x * x, axis=-1, keepdims=True) + EPS) * g


def _mod_part(m, idx):
    return m[:, idx * D:(idx + 1) * D]


def _dot(a, b):
    return jnp.dot(a, b, preferred_element_type=F32)


def _dot_nt(a, b):
    return lax.dot_general(a, b, (((1,), (1,)), ((), ())), preferred_element_type=F32)


def _dot_tn(a, b):
    return lax.dot_general(a, b, (((0,), (0,)), ((), ())), preferred_element_type=F32)


def _params(*sem):
    return pltpu.CompilerParams(dimension_semantics=sem, vmem_limit_bytes=VMEM_LIMIT)


def _mod_kernel(c_ref, w_ref, b_ref, o_ref):
    s = _silu(c_ref[...])
    o_ref[0] = _dot(s.astype(BF16), w_ref[0].astype(BF16)) + b_ref[0]


def _modulation(cond8, ada_w, ada_b):
    depth = ada_w.shape[0]
    tn = 1536
    out = pl.pallas_call(
        _mod_kernel,
        grid=(depth, N_MOD * D // tn),
        in_specs=[pl.BlockSpec((8, D), lambda l, j: (0, 0)),
                  pl.BlockSpec((1, D, tn), lambda l, j: (l, 0, j)),
                  pl.BlockSpec((1, 1, tn), lambda l, j: (l, 0, j))],
        out_specs=pl.BlockSpec((1, 8, tn), lambda l, j: (l, 0, j)),
        out_shape=jax.ShapeDtypeStruct((depth, 8, N_MOD * D), F32),
        compiler_params=_params("parallel", "parallel"),
        name="adaln_mod",
    )(cond8, ada_w, ada_b.reshape(depth, 1, N_MOD * D))
    return out.reshape(depth, 8, 1, N_MOD * D)


def _channel_dft(hb, cs_ref, put):
    for g in range(GROUPS):
        y = _dot(hb[:, g * GD:(g + 1) * GD], cs_ref[...])
        put(0, g, y[:, :GD].astype(BF16))
        put(1, g, y[:, GD:].astype(BF16))


def _fnet_ctx_kernel(x_ref, mod_ref, g_ref, cs_ref, t_ref, o_ref, y_scr):
    m = mod_ref[0, 0]
    h = _rms(x_ref[0], g_ref[...]) * (1.0 + _mod_part(m, 1)) + _mod_part(m, 0)
    seq = x_ref.shape[1]

    def put(kind, g, val):
        y_scr[kind * seq:(kind + 1) * seq, g * GD:(g + 1) * GD] = val

    _channel_dft(h.astype(BF16), cs_ref, put)
    o_ref[0] = _dot(t_ref[...], y_scr[...]).astype(o_ref.dtype)


def _fnet_ctx(x, mod, layer, gain, cs_ch, t_ctx):
    b, seq, _ = x.shape
    return pl.pallas_call(
        _fnet_ctx_kernel,
        grid=(b,),
        in_specs=[pl.BlockSpec((1, seq, D), lambda i: (i, 0, 0)),
                  pl.BlockSpec((1, 1, 1, N_MOD * D), lambda i: (layer, 0, 0, 0)),
                  pl.BlockSpec((1, D), lambda i: (0, 0)),
                  pl.BlockSpec((GD, 2 * GD), lambda i: (0, 0)),
                  pl.BlockSpec((seq, 2 * seq), lambda i: (0, 0))],
        out_specs=pl.BlockSpec((1, seq, D), lambda i: (i, 0, 0)),
        out_shape=jax.ShapeDtypeStruct((b, seq, D), BF16),
        scratch_shapes=[pltpu.VMEM((2 * seq, D), BF16)],
        compiler_params=_params("parallel"),
        name="fnet_ctx",
    )(x, mod, gain, cs_ch, t_ctx)


def _fnet_slab_kernel(x_ref, pos_ref, mod_ref, g_ref, cs_ref, y_ref, col_scr, xp_scr):
    tm = x_ref.shape[1]
    rows = tm // SEQ_SLABS
    n_col = D // LANES
    for j in range(n_col):
        cols = slice(j * LANES, (j + 1) * LANES)
        col_scr[j] = x_ref[0, :, cols] + pos_ref[:, cols]
    for n2 in range(SEQ_SLABS):
        for j in range(n_col):
            xp_scr[n2 * rows:(n2 + 1) * rows, j * LANES:(j + 1) * LANES] = (
                col_scr[j, pl.ds(n2, rows, stride=SEQ_SLABS), :])
    m = mod_ref[0, 0]
    h = _rms(xp_scr[...], g_ref[...]) * (1.0 + _mod_part(m, 1)) + _mod_part(m, 0)

    def put(kind, g, val):
        for n2 in range(SEQ_SLABS):
            y_ref[0, n2, kind, :, g * GD:(g + 1) * GD] = val[n2 * rows:(n2 + 1) * rows, :]

    _channel_dft(h.astype(BF16), cs_ref, put)


def _fft_real_part(xs):
    n = len(xs)
    if n == 1:
        return xs
    ev = _fft_real_part(xs[0::2])
    od = _fft_real_part(xs[1::2])
    out = [None] * n
    for k in range(n // 2):
        c = math.cos(2.0 * math.pi * k / n)
        s = -math.sin(2.0 * math.pi * k / n)
        orr, oi = od[k]
        if k == 0:
            tr, ti = orr, oi
        elif 4 * k == n:
            tr, ti = oi, -orr
        else:
            tr, ti = orr * c - oi * s, orr * s + oi * c
        er, ei = ev[k]
        out[k] = (er + tr, ei + ti)
        out[k + n // 2] = (er - tr, ei - ti)
    return out


def _fnet_seq_kernel(y_ref, g_ref, o_ref, z_scr):
    tc = o_ref.shape[2]
    for n2 in range(SEQ_SLABS):
        yb = y_ref[0, n2].reshape(2 * SEQ_PTS, tc)
        z_scr[n2] = _dot(g_ref[n2], yb)

    rb = 8
    n_lane = tc // 128

    def body(i, carry):
        r = i // n_lane
        c = i % n_lane
        r0 = pl.multiple_of(r * rb, rb)
        c0 = pl.multiple_of(c * 128, 128)
        xs = []
        for n2 in range(SEQ_SLABS):
            re = z_scr[n2, pl.ds(r0, rb), pl.ds(c0, 128)]
            nim = z_scr[n2, pl.ds(SEQ_PTS + r0, rb), pl.ds(c0, 128)]
            xs.append((re, -nim))
        out = _fft_real_part(xs)
        for k2 in range(SEQ_SLABS):
            o_ref[0, pl.ds(k2 * SEQ_PTS + r0, rb), pl.ds(c0, 128)] = out[k2][0]
        return carry

    lax.fori_loop(0, (SEQ_PTS // rb) * n_lane, body, 0)


def _fnet_sample(x, pos, mod, layer, gain, cs_ch, g_seq):
    b, seq, _ = x.shape
    tm = 512
    tc = 256
    y = pl.pallas_call(
        _fnet_slab_kernel,
        grid=(b, seq // tm),
        in_specs=[pl.BlockSpec((1, tm, D), lambda i, j: (i, j, 0)),
                  pl.BlockSpec((tm, D), lambda i, j: (j, 0)),
                  pl.BlockSpec((1, 1, 1, N_MOD * D), lambda i, j: (layer, 1 + i, 0, 0)),
                  pl.BlockSpec((1, D), lambda i, j: (0, 0)),
                  pl.BlockSpec((GD, 2 * GD), lambda i, j: (0, 0))],
        out_specs=pl.BlockSpec((1, SEQ_SLABS, 2, tm // SEQ_SLABS, D), lambda i, j: (i, 0, 0, j, 0)),
        out_shape=jax.ShapeDtypeStruct((b, SEQ_SLABS, 2, SEQ_PTS, D), BF16),
        scratch_shapes=[pltpu.VMEM((D // LANES, tm, LANES), F32), pltpu.VMEM((tm, D), F32)],
        compiler_params=_params("parallel", "parallel"),
        name="fnet_slab",
    )(x, pos, mod, gain, cs_ch)
    return pl.pallas_call(
        _fnet_seq_kernel,
        grid=(b, D // tc),
        in_specs=[pl.BlockSpec((1, SEQ_SLABS, 2, SEQ_PTS, tc), lambda i, j: (i, 0, 0, 0, j)),
                  pl.BlockSpec((SEQ_SLABS, 2 * SEQ_PTS, 2 * SEQ_PTS), lambda i, j: (0, 0, 0))],
        out_specs=pl.BlockSpec((1, seq, tc), lambda i, j: (i, 0, j)),
        out_shape=jax.ShapeDtypeStruct((b, seq, D), F32),
        scratch_shapes=[pltpu.VMEM((SEQ_SLABS, 2 * SEQ_PTS, tc), F32)],
        compiler_params=_params("parallel", "parallel"),
        name="fnet_seq",
    )(y, g_seq)


def _post_kernel(*refs, has_pos, final):
    it = iter(refs)
    x_ref = next(it)
    pos_ref = next(it) if has_pos else None
    a_ref, mod_ref, wo_ref, w1_ref, w2_ref, gmlp_ref, gnext_ref = (next(it) for _ in range(7))
    modn_ref = None if final else next(it)
    o_ref = next(it)
    hn_ref = None if final else next(it)
    x1_scr, h_scr, u_scr = (next(it) for _ in range(3))

    m = mod_ref[0, 0]
    subs = [slice(s * POST_SUB, (s + 1) * POST_SUB) for s in range(x_ref.shape[0] // POST_SUB)]

    for r in subs:
        x = x_ref[r, :]
        if has_pos:
            x = x + pos_ref[r, :]
        x1 = x + _mod_part(m, 2) * _dot(a_ref[r, :].astype(BF16), wo_ref[0])
        x1_scr[r, :] = x1
        h = _rms(x1, gmlp_ref[...]) * (1.0 + _mod_part(m, 4)) + _mod_part(m, 3)
        h_scr[r, :] = h.astype(BF16)

    for s, r in enumerate(subs):
        for j in range(D_FF // POST_TF):
            cols = slice(j * POST_TF, (j + 1) * POST_TF)
            u = jnp.maximum(_dot(h_scr[r, :], w1_ref[0, :, cols]), 0.0)
            u_scr[s, :, cols] = (u * u).astype(BF16)
        x2 = x1_scr[r, :] + _mod_part(m, 5) * _dot(u_scr[s], w2_ref[0])
        if final:
            o_ref[r, :] = _rms(x2, gnext_ref[...])
        else:
            o_ref[r, :] = x2
            mn = modn_ref[0, 0]
            hn = _rms(x2, gnext_ref[...]) * (1.0 + _mod_part(mn, 1)) + _mod_part(mn, 0)
            hn_ref[r, :] = hn.astype(BF16)


def _post_mixer(x, pos, a, mod, layer, mod_row, wo, w1, w2, g_mlp, g_next, final):
    rows = x.shape[0]
    tm = POST_TM
    has_pos = pos is not None
    n_pos = pos.shape[0] // tm if has_pos else 1
    row_map = lambda i: (i, 0)
    const = lambda i: (0, 0)
    resident = lambda shape: pl.BlockSpec(shape, lambda i: (layer, 0, 0), pipeline_mode=pl.Buffered(1))
    in_specs = [pl.BlockSpec((tm, D), row_map)]
    args = [x]
    if has_pos:
        in_specs.append(pl.BlockSpec((tm, D), lambda i: (i % n_pos, 0)))
        args.append(pos)
    in_specs += [pl.BlockSpec((tm, D), row_map),
                 pl.BlockSpec((1, 1, 1, N_MOD * D), lambda i: (layer, mod_row(i * tm), 0, 0)),
                 pl.BlockSpec((1, D, D), lambda i: (0, 0, 0), pipeline_mode=pl.Buffered(1)),
                 resident((1, D, D_FF)),
                 resident((1, D_FF, D)),
                 pl.BlockSpec((1, D), const),
                 pl.BlockSpec((1, D), const)]
    args += [a, mod, wo, w1, w2, g_mlp, g_next]
    out_specs = [pl.BlockSpec((tm, D), row_map)]
    out_shape = [jax.ShapeDtypeStruct((rows, D), F32)]
    if not final:
        in_specs.append(pl.BlockSpec((1, 1, 1, N_MOD * D),
                                     lambda i: (layer + 1, mod_row(i * tm), 0, 0)))
        args.append(mod)
        out_specs.append(pl.BlockSpec((tm, D), row_map))
        out_shape.append(jax.ShapeDtypeStruct((rows, D), BF16))
    res = pl.pallas_call(
        functools.partial(_post_kernel, has_pos=has_pos, final=final),
        grid=(rows // tm,),
        in_specs=in_specs,
        out_specs=out_specs,
        out_shape=out_shape,
        scratch_shapes=[pltpu.VMEM((tm, D), F32), pltpu.VMEM((tm, D), BF16),
                        pltpu.VMEM((tm // POST_SUB, POST_SUB, D_FF), BF16)],
        compiler_params=_params("parallel"),
        name="post_mlp_final" if final else "post_mlp",
    )(*args)
    return res[0] if final else (res[0], res[1])


PER_TILE = SCAN_TILE // CHUNK


def _decay_matrices():
    t = np.arange(SCAN_TILE)
    start = (t // CHUNK) * CHUNK
    first = np.arange(PER_TILE) * CHUNK
    same = (t[:, None] // CHUNK) == (t[None, :] // CHUNK)
    out = []
    for incl, mid, last in ((t[None, :] <= t[:, None], CHUNK // 2 - 1, CHUNK - 1),
                            (t[None, :] >= t[:, None], CHUNK // 2, 0)):
        bd = (same & incl).astype(np.float32)
        ref = bd[first + mid]
        pad = np.zeros((BF16_ROWS - 2 * PER_TILE, SCAN_TILE), np.float32)
        out.append(np.concatenate([bd - bd[start + mid], ref, bd[first + last] - ref, pad], axis=0))
    return jnp.asarray(np.stack(out), BF16)


def _hgrn_kernel(*refs, has_s0):
    it = iter(refs)
    hn_ref = next(it)
    wp_refs = [next(it) for _ in range(PROJ)]
    lb_ref, gn_ref, dm_ref = (next(it) for _ in range(3))
    s0_ref = next(it) if has_s0 else None
    o_ref, sf_ref = next(it), next(it)
    qd_scr, kd_scr, qb_scr, kl_scr, g_scr, o_scr, k_scr = ((next(it), next(it)) for _ in range(7))
    ext_scr, q_scr, v_scr, sg_scr, st_scr, sin_scr, u_scr, sb_scr, gs_scr, w_scr, flag_scr = (
        next(it) for _ in range(11))

    @pl.when(pl.program_id(1) == 0)
    def _():
        for j, wp_ref in enumerate(wp_refs):
            w_scr[:, j * DK:(j + 1) * DK] = wp_ref[...].astype(BF16)

    seq = hn_ref.shape[1]
    tp = SCAN_TILE
    n_tiles = seq // tp
    per_tile = PER_TILE

    lraw = lb_ref[...]
    mx = jnp.max(lraw, axis=1, keepdims=True)
    ex = jnp.exp(lraw - mx)
    sm = ex / jnp.sum(ex, axis=1, keepdims=True)
    lbv = (sm[:, 0, :] + sm[:, 1, :]) - sm[:, 0, :]

    def gates(i):
        rows = pl.ds(pl.multiple_of(i * tp, tp), tp)
        p = _dot(hn_ref[0, rows, :], w_scr[...])
        q_scr[rows, :] = _silu(p[:, 0:DK])
        v_scr[rows, :] = p[:, 3 * DK:4 * DK].astype(BF16)
        sg_scr[rows, :] = _silu(p[:, 4 * DK:5 * DK])
        for d in range(2):
            lb = lbv[d:d + 1, :]
            f = lb + (1.0 - lb) * jax.nn.sigmoid(p[:, (1 + d) * DK:(2 + d) * DK])
            k_scr[d][rows, :] = 1.0 - f
            g = jnp.log(f)
            g_scr[d][rows, :] = g
            g1 = g.astype(BF16)
            gs_scr[d, :, 0:DK] = g1
            gs_scr[d, :, DK:2 * DK] = (g - g1.astype(F32)).astype(BF16)

    def decay_sums():
        return [_dot(dm_ref[d], gs_scr[d]) for d in range(2)]

    def factors(i, sums):
        rows = pl.ds(pl.multiple_of(i * tp, tp), tp)
        q = q_scr[rows, :]
        unsafe = jnp.int32(0)
        for d in range(2):
            r = sums[d][:, :DK] + sums[d][:, DK:]
            a = r[:tp]
            ext = r[tp:]
            ext_scr[d, i] = ext
            unsafe = unsafe | (jnp.max(jnp.abs(a)) > EXP_SAFE).astype(jnp.int32)
            qd = q * jnp.exp(a)
            kd = k_scr[d][rows, :] * jnp.exp(-a)
            qd_scr[d][rows, :] = qd.astype(BF16)
            kd_scr[d][rows, :] = kd.astype(BF16)
            e_ref = jnp.exp(ext[0:per_tile])
            e_lr = jnp.exp(ext[per_tile:2 * per_tile])
            for c in range(per_tile):
                cr = pl.ds(pl.multiple_of(i * tp + c * CHUNK, CHUNK), CHUNK)
                cs = slice(c * CHUNK, (c + 1) * CHUNK)
                qb_scr[d][cr, :] = (qd[cs] * e_ref[c:c + 1]).astype(BF16)
                kl_scr[d][cr, :] = (kd[cs] * e_lr[c:c + 1]).astype(BF16)
        flag_scr[i] = unsafe

    for d in range(2):
        if has_s0:
            st_scr[d] = s0_ref[0, 0, d, 0].T
        else:
            st_scr[d] = jnp.zeros((DK, DK), F32)

    ti = lax.broadcasted_iota(jnp.int32, (tp, tp), 0)
    si = lax.broadcasted_iota(jnp.int32, (tp, tp), 1)
    same = (ti // CHUNK) == (si // CHUNK)
    masks = (same & (ti >= si), same & (si >= ti))

    def scan_head(d, tile):
        order = range(per_tile) if d == 0 else range(per_tile - 1, -1, -1)
        r0 = pl.multiple_of(tile * tp, tp)
        vv = v_scr[pl.ds(r0, tp), :]
        for c in range(per_tile):
            cr = pl.ds(pl.multiple_of(r0 + c * CHUNK, CHUNK), CHUNK)
            u_scr[d, c] = _dot_tn(vv[c * CHUNK:(c + 1) * CHUNK, :], kl_scr[d][cr, :])
        ext = ext_scr[d, tile]
        decay = jnp.exp(ext[0:per_tile] + ext[per_tile:2 * per_tile])
        st = st_scr[d]
        sin_scr[d] = st
        for c in order:
            sb_scr[d, c] = st.astype(BF16)
            st = st * decay[c:c + 1] + u_scr[d, c]
        st_scr[d] = st

    def scan_tail(d, tile):
        r0 = pl.multiple_of(tile * tp, tp)
        rows = pl.ds(r0, tp)
        sc = jnp.where(masks[d], _dot_nt(qd_scr[d][rows, :], kd_scr[d][rows, :]), 0.0)
        o_scr[d][rows, :] = _dot(sc.astype(BF16), v_scr[rows, :])
        for c in range(per_tile):
            cr = pl.ds(pl.multiple_of(r0 + c * CHUNK, CHUNK), CHUNK)
            o_scr[d][cr, :] += _dot_nt(qb_scr[d][cr, :], sb_scr[d, c])

    rid = lax.broadcasted_iota(jnp.int32, (BF16_ROWS, DK), 0)

    def redo_exact(d, tile):
        r0 = tile * tp

        def step(j, st):
            t = r0 + ((tp - 1 - j) if d == 1 else j)
            t0 = pl.multiple_of((t // BF16_ROWS) * BF16_ROWS, BF16_ROWS)
            grp = pl.ds(t0, BF16_ROWS)
            sel = rid == (t - t0)
            g = g_scr[d][grp, :]
            f = jnp.exp(jnp.sum(jnp.where(sel, g, 0.0), axis=0, keepdims=True))
            k = jnp.where(sel, 1.0 - jnp.exp(g), 0.0)
            st = st * f + _dot_tn(v_scr[grp, :], k.astype(BF16))
            o = _dot_nt(q_scr[grp, :].astype(BF16), st.astype(BF16))
            o_scr[d][grp, :] = jnp.where(sel, o, o_scr[d][grp, :])
            return st

        @pl.when(flag_scr[tile] != 0)
        def _():
            st_scr[d] = lax.fori_loop(0, tp, step, sin_scr[d])

    def finish(i):
        rows = pl.ds(pl.multiple_of(i * tp, tp), tp)
        o = o_scr[0][rows, :] + o_scr[1][rows, :]
        o = o * lax.rsqrt(jnp.mean(o * o, axis=-1, keepdims=True) + EPS) * gn_ref[...]
        o_ref[0, rows, :] = (o * sg_scr[rows, :]).astype(o_ref.dtype)

    last = n_tiles - 1
    gates(0)
    if n_tiles > 1:
        sums = decay_sums()
        gates(1)
        factors(0, sums)

        def forward(i, carry):
            scan_head(0, i - 1)
            sums = decay_sums()
            gates(i + 1)
            factors(i, sums)
            scan_tail(0, i - 1)
            redo_exact(0, i - 1)
            return carry

        lax.fori_loop(1, last, forward, 0)
        scan_head(0, last - 1)
        factors(last, decay_sums())
        scan_tail(0, last - 1)
        redo_exact(0, last - 1)
    else:
        factors(0, decay_sums())
    scan_head(0, last)
    scan_tail(0, last)
    redo_exact(0, last)

    scan_head(1, last)
    scan_tail(1, last)
    redo_exact(1, last)

    def backward(j, carry):
        t = last - j
        scan_head(1, t)
        finish(t + 1)
        scan_tail(1, t)
        redo_exact(1, t)
        return carry

    lax.fori_loop(1, n_tiles, backward, 0)
    finish(0)
    for d in range(2):
        sf_ref[0, 0, d, 0] = st_scr[d].T


def _hgrn_mixer(hn, w_in, lb_raw, g_norm, s0):
    b, seq, _ = hn.shape
    has_s0 = s0 is not None
    st_spec = pl.BlockSpec((1, 1, 2, 1, DK, DK), lambda h, i: (i, 0, 0, h, 0, 0))
    in_specs = [pl.BlockSpec((1, seq, D), lambda h, i: (i, 0, 0))]
    in_specs += [pl.BlockSpec((D, DK), functools.partial(lambda j, h, i: (0, j * HEADS + h), j))
                 for j in range(PROJ)]
    in_specs += [pl.BlockSpec((2, 2, DK), lambda h, i: (0, 0, h)),
                 pl.BlockSpec((1, DK), lambda h, i: (0, 0)),
                 pl.BlockSpec((2, SCAN_TILE + BF16_ROWS, SCAN_TILE), lambda h, i: (0, 0, 0))]
    args = [hn] + [w_in] * PROJ + [lb_raw, g_norm, _decay_matrices()]
    if has_s0:
        in_specs.append(st_spec)
        args.append(s0)
    n_tiles = seq // SCAN_TILE
    bf16_rows = pltpu.VMEM((seq, DK), BF16)
    f32_rows = pltpu.VMEM((seq, DK), F32)
    return pl.pallas_call(
        functools.partial(_hgrn_kernel, has_s0=has_s0),
        grid=(HEADS, b),
        in_specs=in_specs,
        out_specs=[pl.BlockSpec((1, seq, DK), lambda h, i: (i, 0, h)), st_spec],
        out_shape=[jax.ShapeDtypeStruct((b, seq, D), BF16),
                   jax.ShapeDtypeStruct((b, 1, 2, HEADS, DK, DK), F32)],
        scratch_shapes=[bf16_rows] * 8 + [f32_rows] * 6 + [
                        pltpu.VMEM((2, n_tiles, BF16_ROWS, DK), F32),
                        f32_rows, bf16_rows, f32_rows,
                        pltpu.VMEM((2, DK, DK), F32),
                        pltpu.VMEM((2, DK, DK), F32),
                        pltpu.VMEM((2, PER_TILE, DK, DK), F32),
                        pltpu.VMEM((2, PER_TILE, DK, DK), BF16),
                        pltpu.VMEM((2, SCAN_TILE, 2 * DK), BF16),
                        pltpu.VMEM((D, PROJ * DK), BF16),
                        pltpu.SMEM((n_tiles,), jnp.int32)],
        compiler_params=_params("parallel", "arbitrary"),
        name="hgrn_scan",
    )(*args)


def kernel(x_prompt, x_sample, state_hgrn, c, c_ctx, ada_w, ada_b, norm_mix, norm_mlp, fnet_wo,
           hgrn_w_in, hgrn_lb, hgrn_norm, hgrn_wo, mlp_w1, mlp_w2, norm_final):
    bp, lp, _ = x_prompt.shape
    bs, ls, _ = x_sample.shape
    assert ada_w.shape[0] == 2 and ls == SEQ_PTS * SEQ_SLABS and lp == GD

    cs_ch, t_ctx, g_seq = _dft_tables()
    pos = _grid_pos_embed(ls, D)

    cond8 = jnp.zeros((8, D), F32).at[0].set(c_ctx).at[1:1 + bs].set(c)
    mod = _modulation(cond8, ada_w, ada_b)

    wo0 = fnet_wo.astype(BF16)
    wo1 = hgrn_wo.astype(BF16)
    w1 = mlp_w1.astype(BF16)
    w2 = mlp_w2.astype(BF16)
    w_heads = hgrn_w_in[0]
    gm = norm_mix.reshape(-1, 1, D)
    gf = norm_mlp.reshape(-1, 1, D)
    g_fin = norm_final.reshape(1, D)
    g_hn = hgrn_norm[0].reshape(1, DK)

    ctx_row = lambda r: 0
    smp_row = lambda r: 1 + r // ls

    mixed = _fnet_ctx(x_prompt, mod, 0, gm[0], cs_ch, t_ctx)
    x2, hn = _post_mixer(x_prompt.reshape(bp * lp, D), None, mixed.reshape(bp * lp, D), mod, 0,
                         ctx_row, wo0, w1, w2, gf[0], gm[1], final=False)
    o, new_state = _hgrn_mixer(hn.reshape(bp, lp, D), w_heads, hgrn_lb, g_hn, None)
    y_prompt = _post_mixer(x2, None, o.reshape(bp * lp, D), mod, 1, ctx_row, wo1, w1, w2,
                           gf[1], g_fin, final=True).reshape(bp, lp, D)

    mixed = _fnet_sample(x_sample, pos, mod, 0, gm[0], cs_ch, g_seq)
    x2, hn = _post_mixer(x_sample.reshape(bs * ls, D), pos, mixed.reshape(bs * ls, D), mod, 0,
                         smp_row, wo0, w1, w2, gf[0], gm[1], final=False)
    o, _ = _hgrn_mixer(hn.reshape(bs, ls, D), w_heads, hgrn_lb, g_hn, state_hgrn)
    y_sample = _post_mixer(x2, None, o.reshape(bs * ls, D), mod, 1, smp_row, wo1, w1, w2,
                           gf[1], g_fin, final=True).reshape(bs, ls, D)
    return (y_prompt, y_sample, new_state)
```

```python
import functools
import math

import numpy as np
import jax
import jax.numpy as jnp
from jax import lax
from jax.experimental import pallas as pl
from jax.experimental.pallas import tpu as pltpu

F32 = jnp.float32
BF16 = jnp.bfloat16

D = 1024
N_MOD = 6
D_FF = 4 * D
EPS = 1e-6
GROUPS = 4
GD = D // GROUPS
HEADS = 8
DK = 128
GRID_W = 64
POS_BASE = 10000.0
PROJ = 5

SEQ_SLABS = 16
SEQ_PTS = 256
LANES = 128
BF16_ROWS = 16
CHUNK = 128
SCAN_TILE = 256
EXP_SAFE = 80.0
POST_TM = 512
POST_SUB = 256
POST_TF = 1024

VMEM_LIMIT = 56 * 1024 * 1024


def _dft_tables():
    n = np.arange(GD)
    ang = 2.0 * np.pi * ((n[:, None] * n[None, :]) % GD) / GD
    cs_ch = np.concatenate([np.cos(ang), np.sin(ang)], axis=1) / math.sqrt(GD)
    t_ctx = np.concatenate([np.cos(ang), -np.sin(ang)], axis=1) / math.sqrt(GD)
    length = SEQ_PTS * SEQ_SLABS
    k1 = np.arange(SEQ_PTS)[:, None]
    n1 = np.arange(SEQ_PTS)[None, :]
    blocks = []
    for n2 in range(SEQ_SLABS):
        a = 2.0 * np.pi * ((k1 * (SEQ_SLABS * n1 + n2)) % length) / length
        gc, gs = np.cos(a) / math.sqrt(length), np.sin(a) / math.sqrt(length)
        blocks.append(np.block([[gc, -gs], [gs, gc]]))
    g_seq = np.stack(blocks)
    return tuple(jnp.asarray(t, F32).astype(BF16) for t in (cs_ch, t_ctx, g_seq))


def _grid_pos_embed(n_tok, d):
    rows = n_tok // GRID_W
    quarter = d // 4
    omega = 1.0 / (POS_BASE ** (jnp.arange(quarter, dtype=F32) / quarter))
    r = jnp.arange(rows, dtype=F32)[:, None] * omega[None, :]
    cl = jnp.arange(GRID_W, dtype=F32)[:, None] * omega[None, :]
    er = jnp.concatenate([jnp.sin(r), jnp.cos(r)], axis=-1)
    ec = jnp.concatenate([jnp.sin(cl), jnp.cos(cl)], axis=-1)
    emb = jnp.concatenate([jnp.broadcast_to(er[:, None, :], (rows, GRID_W, d // 2)),
                           jnp.broadcast_to(ec[None, :, :], (rows, GRID_W, d // 2))], axis=-1)
    return emb.reshape(n_tok, d)


def _silu(x):
    return x * jax.nn.sigmoid(x)


def _rms(x, g):
    return x * lax.rsqrt(jnp.mean(x * x, axis=-1, keepdims=True) + EPS) * g


def _mod_part(m, idx):
    return m[:, idx * D:(idx + 1) * D]


def _dot(a, b):
    return jnp.dot(a, b, preferred_element_type=F32)


def _dot_nt(a, b):
    return lax.dot_general(a, b, (((1,), (1,)), ((), ())), preferred_element_type=F32)


def _dot_tn(a, b):
    return lax.dot_general(a, b, (((0,), (0,)), ((), ())), preferred_element_type=F32)


def _params(*sem):
    return pltpu.CompilerParams(dimension_semantics=sem, vmem_limit_bytes=VMEM_LIMIT)


def _mod_kernel(c_ref, w_ref, b_ref, o_ref):
    s = _silu(c_ref[...])
    o_ref[0] = _dot(s.astype(BF16), w_ref[0].astype(BF16)) + b_ref[0]


def _modulation(cond8, ada_w, ada_b):
    depth = ada_w.shape[0]
    tn = 1536
    out = pl.pallas_call(
        _mod_kernel,
        grid=(depth, N_MOD * D // tn),
        in_specs=[pl.BlockSpec((8, D), lambda l, j: (0, 0)),
                  pl.BlockSpec((1, D, tn), lambda l, j: (l, 0, j)),
                  pl.BlockSpec((1, 1, tn), lambda l, j: (l, 0, j))],
        out_specs=pl.BlockSpec((1, 8, tn), lambda l, j: (l, 0, j)),
        out_shape=jax.ShapeDtypeStruct((depth, 8, N_MOD * D), F32),
        compiler_params=_params("parallel", "parallel"),
        name="adaln_mod",
    )(cond8, ada_w, ada_b.reshape(depth, 1, N_MOD * D))
    return out.reshape(depth, 8, 1, N_MOD * D)


def _channel_dft(hb, cs_ref, put):
    for g in range(GROUPS):
        y = _dot(hb[:, g * GD:(g + 1) * GD], cs_ref[...])
        put(0, g, y[:, :GD].astype(BF16))
        put(1, g, y[:, GD:].astype(BF16))


def _fnet_ctx_kernel(x_ref, mod_ref, g_ref, cs_ref, t_ref, o_ref, y_scr):
    m = mod_ref[0, 0]
    h = _rms(x_ref[0], g_ref[...]) * (1.0 + _mod_part(m, 1)) + _mod_part(m, 0)
    seq = x_ref.shape[1]

    def put(kind, g, val):
        y_scr[kind * seq:(kind + 1) * seq, g * GD:(g + 1) * GD] = val

    _channel_dft(h.astype(BF16), cs_ref, put)
    o_ref[0] = _dot(t_ref[...], y_scr[...]).astype(o_ref.dtype)


def _fnet_ctx(x, mod, layer, gain, cs_ch, t_ctx):
    b, seq, _ = x.shape
    return pl.pallas_call(
        _fnet_ctx_kernel,
        grid=(b,),
        in_specs=[pl.BlockSpec((1, seq, D), lambda i: (i, 0, 0)),
                  pl.BlockSpec((1, 1, 1, N_MOD * D), lambda i: (layer, 0, 0, 0)),
                  pl.BlockSpec((1, D), lambda i: (0, 0)),
                  pl.BlockSpec((GD, 2 * GD), lambda i: (0, 0)),
                  pl.BlockSpec((seq, 2 * seq), lambda i: (0, 0))],
        out_specs=pl.BlockSpec((1, seq, D), lambda i: (i, 0, 0)),
        out_shape=jax.ShapeDtypeStruct((b, seq, D), BF16),
        scratch_shapes=[pltpu.VMEM((2 * seq, D), BF16)],
        compiler_params=_params("parallel"),
        name="fnet_ctx",
    )(x, mod, gain, cs_ch, t_ctx)


def _fnet_slab_kernel(x_ref, pos_ref, mod_ref, g_ref, cs_ref, y_ref, col_scr, xp_scr):
    tm = x_ref.shape[1]
    rows = tm // SEQ_SLABS
    n_col = D // LANES
    for j in range(n_col):
        cols = slice(j * LANES, (j + 1) * LANES)
        col_scr[j] = x_ref[0, :, cols] + pos_ref[:, cols]
    for n2 in range(SEQ_SLABS):
        for j in range(n_col):
            xp_scr[n2 * rows:(n2 + 1) * rows, j * LANES:(j + 1) * LANES] = (
                col_scr[j, pl.ds(n2, rows, stride=SEQ_SLABS), :])
    m = mod_ref[0, 0]
    h = _rms(xp_scr[...], g_ref[...]) * (1.0 + _mod_part(m, 1)) + _mod_part(m, 0)

    def put(kind, g, val):
        for n2 in range(SEQ_SLABS):
            y_ref[0, n2, kind, :, g * GD:(g + 1) * GD] = val[n2 * rows:(n2 + 1) * rows, :]

    _channel_dft(h.astype(BF16), cs_ref, put)


def _fft_real_part(xs):
    n = len(xs)
    if n == 1:
        return xs
    ev = _fft_real_part(xs[0::2])
    od = _fft_real_part(xs[1::2])
    out = [None] * n
    for k in range(n // 2):
        c = math.cos(2.0 * math.pi * k / n)
        s = -math.sin(2.0 * math.pi * k / n)
        orr, oi = od[k]
        if k == 0:
            tr, ti = orr, oi
        elif 4 * k == n:
            tr, ti = oi, -orr
        else:
            tr, ti = orr * c - oi * s, orr * s + oi * c
        er, ei = ev[k]
        out[k] = (er + tr, ei + ti)
        out[k + n // 2] = (er - tr, ei - ti)
    return out


def _fnet_seq_kernel(y_ref, g_ref, o_ref, z_scr):
    tc = o_ref.shape[2]
    for n2 in range(SEQ_SLABS):
        yb = y_ref[0, n2].reshape(2 * SEQ_PTS, tc)
        z_scr[n2] = _dot(g_ref[n2], yb)

    rb = 8

    def body(r, carry):
        r0 = pl.multiple_of(r * rb, rb)
        for c in range(tc // LANES):
            cols = slice(c * LANES, (c + 1) * LANES)
            xs = []
            for n2 in range(SEQ_SLABS):
                re = z_scr[n2, pl.ds(r0, rb), cols]
                nim = z_scr[n2, pl.ds(SEQ_PTS + r0, rb), cols]
                xs.append((re, -nim))
            out = _fft_real_part(xs)
            for k2 in range(SEQ_SLABS):
                o_ref[0, pl.ds(k2 * SEQ_PTS + r0, rb), cols] = out[k2][0]
        return carry

    lax.fori_loop(0, SEQ_PTS // rb, body, 0)


def _fnet_sample(x, pos, mod, layer, gain, cs_ch, g_seq):
    b, seq, _ = x.shape
    tm = 512
    tc = 256
    y = pl.pallas_call(
        _fnet_slab_kernel,
        grid=(seq // tm, b),
        in_specs=[pl.BlockSpec((1, tm, D), lambda j, i: (i, j, 0)),
                  pl.BlockSpec((tm, D), lambda j, i: (j, 0)),
                  pl.BlockSpec((1, 1, 1, N_MOD * D), lambda j, i: (layer, 1 + i, 0, 0)),
                  pl.BlockSpec((1, D), lambda j, i: (0, 0)),
                  pl.BlockSpec((GD, 2 * GD), lambda j, i: (0, 0))],
        out_specs=pl.BlockSpec((1, SEQ_SLABS, 2, tm // SEQ_SLABS, D), lambda j, i: (i, 0, 0, j, 0)),
        out_shape=jax.ShapeDtypeStruct((b, SEQ_SLABS, 2, SEQ_PTS, D), BF16),
        scratch_shapes=[pltpu.VMEM((D // LANES, tm, LANES), F32), pltpu.VMEM((tm, D), F32)],
        compiler_params=_params("parallel", "parallel"),
        name="fnet_slab",
    )(x, pos, mod, gain, cs_ch)
    return pl.pallas_call(
        _fnet_seq_kernel,
        grid=(b, D // tc),
        in_specs=[pl.BlockSpec((1, SEQ_SLABS, 2, SEQ_PTS, tc), lambda i, j: (i, 0, 0, 0, j)),
                  pl.BlockSpec((SEQ_SLABS, 2 * SEQ_PTS, 2 * SEQ_PTS), lambda i, j: (0, 0, 0))],
        out_specs=pl.BlockSpec((1, seq, tc), lambda i, j: (i, 0, j)),
        out_shape=jax.ShapeDtypeStruct((b, seq, D), F32),
        scratch_shapes=[pltpu.VMEM((SEQ_SLABS, 2 * SEQ_PTS, tc), F32)],
        compiler_params=_params("parallel", "parallel"),
        name="fnet_seq",
    )(y, g_seq)


def _post_kernel(*refs, has_pos, final):
    it = iter(refs)
    x_ref = next(it)
    pos_ref = next(it) if has_pos else None
    a_ref, mod_ref, wo_ref, w1_ref, w2_ref, gmlp_ref, gnext_ref = (next(it) for _ in range(7))
    modn_ref = None if final else next(it)
    o_ref = next(it)
    hn_ref = None if final else next(it)
    x1_scr, h_scr, u_scr = (next(it) for _ in range(3))

    m = mod_ref[0, 0]
    subs = [slice(s * POST_SUB, (s + 1) * POST_SUB) for s in range(x_ref.shape[0] // POST_SUB)]

    for r in subs:
        x = x_ref[r, :]
        if has_pos:
            x = x + pos_ref[r, :]
        x1 = x + _mod_part(m, 2) * _dot(a_ref[r, :].astype(BF16), wo_ref[0])
        x1_scr[r, :] = x1
        h = _rms(x1, gmlp_ref[...]) * (1.0 + _mod_part(m, 4)) + _mod_part(m, 3)
        h_scr[r, :] = h.astype(BF16)

    for s, r in enumerate(subs):
        for j in range(D_FF // POST_TF):
            cols = slice(j * POST_TF, (j + 1) * POST_TF)
            u = jnp.maximum(_dot(h_scr[r, :], w1_ref[0, :, cols]), 0.0)
            u_scr[s, :, cols] = (u * u).astype(BF16)
        x2 = x1_scr[r, :] + _mod_part(m, 5) * _dot(u_scr[s], w2_ref[0])
        if final:
            o_ref[r, :] = _rms(x2, gnext_ref[...])
        else:
            o_ref[r, :] = x2
            mn = modn_ref[0, 0]
            hn = _rms(x2, gnext_ref[...]) * (1.0 + _mod_part(mn, 1)) + _mod_part(mn, 0)
            hn_ref[r, :] = hn.astype(BF16)


def _post_mixer(x, pos, a, mod, layer, mod_row, wo, w1, w2, g_mlp, g_next, final):
    rows = x.shape[0]
    tm = POST_TM
    has_pos = pos is not None
    n_pos = pos.shape[0] // tm if has_pos else 1
    row_map = lambda i: (i, 0)
    const = lambda i: (0, 0)
    resident = lambda shape: pl.BlockSpec(shape, lambda i: (layer, 0, 0), pipeline_mode=pl.Buffered(1))
    in_specs = [pl.BlockSpec((tm, D), row_map)]
    args = [x]
    if has_pos:
        in_specs.append(pl.BlockSpec((tm, D), lambda i: (i % n_pos, 0)))
        args.append(pos)
    in_specs += [pl.BlockSpec((tm, D), row_map),
                 pl.BlockSpec((1, 1, 1, N_MOD * D), lambda i: (layer, mod_row(i * tm), 0, 0)),
                 pl.BlockSpec((1, D, D), lambda i: (0, 0, 0), pipeline_mode=pl.Buffered(1)),
                 resident((1, D, D_FF)),
                 resident((1, D_FF, D)),
                 pl.BlockSpec((1, D), const),
                 pl.BlockSpec((1, D), const)]
    args += [a, mod, wo, w1, w2, g_mlp, g_next]
    out_specs = [pl.BlockSpec((tm, D), row_map)]
    out_shape = [jax.ShapeDtypeStruct((rows, D), F32)]
    if not final:
        in_specs.append(pl.BlockSpec((1, 1, 1, N_MOD * D),
                                     lambda i: (layer + 1, mod_row(i * tm), 0, 0)))
        args.append(mod)
        out_specs.append(pl.BlockSpec((tm, D), row_map))
        out_shape.append(jax.ShapeDtypeStruct((rows, D), BF16))
    res = pl.pallas_call(
        functools.partial(_post_kernel, has_pos=has_pos, final=final),
        grid=(rows // tm,),
        in_specs=in_specs,
        out_specs=out_specs,
        out_shape=out_shape,
        scratch_shapes=[pltpu.VMEM((tm, D), F32), pltpu.VMEM((tm, D), BF16),
                        pltpu.VMEM((tm // POST_SUB, POST_SUB, D_FF), BF16)],
        compiler_params=_params("parallel"),
        name="post_mlp_final" if final else "post_mlp",
    )(*args)
    return res[0] if final else (res[0], res[1])


PER_TILE = SCAN_TILE // CHUNK


def _decay_matrices():
    t = np.arange(SCAN_TILE)
    start = (t // CHUNK) * CHUNK
    first = np.arange(PER_TILE) * CHUNK
    same = (t[:, None] // CHUNK) == (t[None, :] // CHUNK)
    out = []
    for incl, mid, last in ((t[None, :] <= t[:, None], CHUNK // 2 - 1, CHUNK - 1),
                            (t[None, :] >= t[:, None], CHUNK // 2, 0)):
        bd = (same & incl).astype(np.float32)
        ref = bd[first + mid]
        pad = np.zeros((BF16_ROWS - 2 * PER_TILE, SCAN_TILE), np.float32)
        out.append(np.concatenate([bd - bd[start + mid], ref, bd[first + last] - ref, pad], axis=0))
    return jnp.asarray(np.stack(out), BF16)


def _hgrn_kernel(*refs, has_s0):
    it = iter(refs)
    hn_ref = next(it)
    wp_refs = [next(it) for _ in range(PROJ)]
    lb_ref, gn_ref, dm_ref = (next(it) for _ in range(3))
    s0_ref = next(it) if has_s0 else None
    o_ref, sf_ref = next(it), next(it)
    qd_scr, kd_scr, qb_scr, kl_scr, g_scr, o_scr, k_scr = ((next(it), next(it)) for _ in range(7))
    ext_scr, q_scr, v_scr, sg_scr, st_scr, sin_scr, u_scr, sb_scr, gs_scr, w_scr, flag_scr = (
        next(it) for _ in range(11))

    @pl.when(pl.program_id(1) == 0)
    def _():
        for j, wp_ref in enumerate(wp_refs):
            w_scr[:, j * DK:(j + 1) * DK] = wp_ref[...].astype(BF16)

    seq = hn_ref.shape[1]
    tp = SCAN_TILE
    n_tiles = seq // tp
    per_tile = PER_TILE

    lraw = lb_ref[...]
    mx = jnp.max(lraw, axis=1, keepdims=True)
    ex = jnp.exp(lraw - mx)
    sm = ex / jnp.sum(ex, axis=1, keepdims=True)
    lbv = (sm[:, 0, :] + sm[:, 1, :]) - sm[:, 0, :]

    def gates(i):
        rows = pl.ds(pl.multiple_of(i * tp, tp), tp)
        p = _dot(hn_ref[0, rows, :], w_scr[...])
        q_scr[rows, :] = _silu(p[:, 0:DK])
        v_scr[rows, :] = p[:, 3 * DK:4 * DK].astype(BF16)
        sg_scr[rows, :] = _silu(p[:, 4 * DK:5 * DK])
        for d in range(2):
            lb = lbv[d:d + 1, :]
            f = lb + (1.0 - lb) * jax.nn.sigmoid(p[:, (1 + d) * DK:(2 + d) * DK])
            k_scr[d][rows, :] = 1.0 - f
            g = jnp.log(f)
            g_scr[d][rows, :] = g
            g1 = g.astype(BF16)
            gs_scr[d, :, 0:DK] = g1
            gs_scr[d, :, DK:2 * DK] = (g - g1.astype(F32)).astype(BF16)

    def decay_sums():
        return [_dot(dm_ref[d], gs_scr[d]) for d in range(2)]

    def factors(i, sums):
        rows = pl.ds(pl.multiple_of(i * tp, tp), tp)
        q = q_scr[rows, :]
        unsafe = jnp.int32(0)
        for d in range(2):
            r = sums[d][:, :DK] + sums[d][:, DK:]
            a = r[:tp]
            ext = r[tp:]
            ext_scr[d, i] = ext
            unsafe = unsafe | (jnp.max(jnp.abs(a)) > EXP_SAFE).astype(jnp.int32)
            qd = q * jnp.exp(a)
            kd = k_scr[d][rows, :] * jnp.exp(-a)
            qd_scr[d][rows, :] = qd.astype(BF16)
            kd_scr[d][rows, :] = kd.astype(BF16)
            e_ref = jnp.exp(ext[0:per_tile])
            e_lr = jnp.exp(ext[per_tile:2 * per_tile])
            for c in range(per_tile):
                cr = pl.ds(pl.multiple_of(i * tp + c * CHUNK, CHUNK), CHUNK)
                cs = slice(c * CHUNK, (c + 1) * CHUNK)
                qb_scr[d][cr, :] = (qd[cs] * e_ref[c:c + 1]).astype(BF16)
                kl_scr[d][cr, :] = (kd[cs] * e_lr[c:c + 1]).astype(BF16)
        flag_scr[i] = unsafe

    for d in range(2):
        if has_s0:
            st_scr[d] = s0_ref[0, 0, d, 0].T
        else:
            st_scr[d] = jnp.zeros((DK, DK), F32)

    ti = lax.broadcasted_iota(jnp.int32, (tp, tp), 0)
    si = lax.broadcasted_iota(jnp.int32, (tp, tp), 1)
    same = (ti // CHUNK) == (si // CHUNK)
    masks = (same & (ti >= si), same & (si >= ti))

    def scan_head(d, tile):
        order = range(per_tile) if d == 0 else range(per_tile - 1, -1, -1)
        r0 = pl.multiple_of(tile * tp, tp)
        vv = v_scr[pl.ds(r0, tp), :]
        for c in range(per_tile):
            cr = pl.ds(pl.multiple_of(r0 + c * CHUNK, CHUNK), CHUNK)
            u_scr[d, c] = _dot_tn(vv[c * CHUNK:(c + 1) * CHUNK, :], kl_scr[d][cr, :])
        ext = ext_scr[d, tile]
        decay = jnp.exp(ext[0:per_tile] + ext[per_tile:2 * per_tile])
        st = st_scr[d]
        sin_scr[d] = st
        for c in order:
            sb_scr[d, c] = st.astype(BF16)
            st = st * decay[c:c + 1] + u_scr[d, c]
        st_scr[d] = st

    def scan_tail(d, tile):
        r0 = pl.multiple_of(tile * tp, tp)
        rows = pl.ds(r0, tp)
        sc = jnp.where(masks[d], _dot_nt(qd_scr[d][rows, :], kd_scr[d][rows, :]), 0.0)
        o_scr[d][rows, :] = _dot(sc.astype(BF16), v_scr[rows, :])
        for c in range(per_tile):
            cr = pl.ds(pl.multiple_of(r0 + c * CHUNK, CHUNK), CHUNK)
            o_scr[d][cr, :] += _dot_nt(qb_scr[d][cr, :], sb_scr[d, c])

    rid = lax.broadcasted_iota(jnp.int32, (BF16_ROWS, DK), 0)

    def redo_exact(d, tile):
        r0 = tile * tp

        def step(j, st):
            t = r0 + ((tp - 1 - j) if d == 1 else j)
            t0 = pl.multiple_of((t // BF16_ROWS) * BF16_ROWS, BF16_ROWS)
            grp = pl.ds(t0, BF16_ROWS)
            sel = rid == (t - t0)
            g = g_scr[d][grp, :]
            f = jnp.exp(jnp.sum(jnp.where(sel, g, 0.0), axis=0, keepdims=True))
            k = jnp.where(sel, 1.0 - jnp.exp(g), 0.0)
            st = st * f + _dot_tn(v_scr[grp, :], k.astype(BF16))
            o = _dot_nt(q_scr[grp, :].astype(BF16), st.astype(BF16))
            o_scr[d][grp, :] = jnp.where(sel, o, o_scr[d][grp, :])
            return st

        @pl.when(flag_scr[tile] != 0)
        def _():
            st_scr[d] = lax.fori_loop(0, tp, step, sin_scr[d])

    def finish(i):
        rows = pl.ds(pl.multiple_of(i * tp, tp), tp)
        o = o_scr[0][rows, :] + o_scr[1][rows, :]
        o = o * lax.rsqrt(jnp.mean(o * o, axis=-1, keepdims=True) + EPS) * gn_ref[...]
        o_ref[0, rows, :] = (o * sg_scr[rows, :]).astype(o_ref.dtype)

    last = n_tiles - 1
    gates(0)
    if n_tiles > 1:
        sums = decay_sums()
        gates(1)
        factors(0, sums)

        def forward(i, carry):
            scan_head(0, i - 1)
            sums = decay_sums()
            gates(i + 1)
            factors(i, sums)
            scan_tail(0, i - 1)
            redo_exact(0, i - 1)
            return carry

        lax.fori_loop(1, last, forward, 0)
        scan_head(0, last - 1)
        factors(last, decay_sums())
        scan_tail(0, last - 1)
        redo_exact(0, last - 1)
    else:
        factors(0, decay_sums())
    scan_head(0, last)
    scan_head(1, last)
    scan_tail(0, last)
    scan_tail(1, last)
    redo_exact(0, last)
    redo_exact(1, last)


    def backward(j, carry):
        t = last - j
        scan_head(1, t)
        finish(t + 1)
        scan_tail(1, t)
        redo_exact(1, t)
        return carry

    lax.fori_loop(1, n_tiles, backward, 0)
    finish(0)
    for d in range(2):
        sf_ref[0, 0, d, 0] = st_scr[d].T


def _hgrn_mixer(hn, w_in, lb_raw, g_norm, s0):
    b, seq, _ = hn.shape
    has_s0 = s0 is not None
    st_spec = pl.BlockSpec((1, 1, 2, 1, DK, DK), lambda h, i: (i, 0, 0, h, 0, 0))
    in_specs = [pl.BlockSpec((1, seq, D), lambda h, i: (i, 0, 0))]
    in_specs += [pl.BlockSpec((D, DK), functools.partial(lambda j, h, i: (0, j * HEADS + h), j))
                 for j in range(PROJ)]
    in_specs += [pl.BlockSpec((2, 2, DK), lambda h, i: (0, 0, h)),
                 pl.BlockSpec((1, DK), lambda h, i: (0, 0)),
                 pl.BlockSpec((2, SCAN_TILE + BF16_ROWS, SCAN_TILE), lambda h, i: (0, 0, 0))]
    args = [hn] + [w_in] * PROJ + [lb_raw, g_norm, _decay_matrices()]
    if has_s0:
        in_specs.append(st_spec)
        args.append(s0)
    n_tiles = seq // SCAN_TILE
    bf16_rows = pltpu.VMEM((seq, DK), BF16)
    f32_rows = pltpu.VMEM((seq, DK), F32)
    return pl.pallas_call(
        functools.partial(_hgrn_kernel, has_s0=has_s0),
        grid=(HEADS, b),
        in_specs=in_specs,
        out_specs=[pl.BlockSpec((1, seq, DK), lambda h, i: (i, 0, h)), st_spec],
        out_shape=[jax.ShapeDtypeStruct((b, seq, D), BF16),
                   jax.ShapeDtypeStruct((b, 1, 2, HEADS, DK, DK), F32)],
        scratch_shapes=[bf16_rows] * 8 + [f32_rows] * 6 + [
                        pltpu.VMEM((2, n_tiles, BF16_ROWS, DK), F32),
                        f32_rows, bf16_rows, f32_rows,
                        pltpu.VMEM((2, DK, DK), F32),
                        pltpu.VMEM((2, DK, DK), F32),
                        pltpu.VMEM((2, PER_TILE, DK, DK), F32),
                        pltpu.VMEM((2, PER_TILE, DK, DK), BF16),
                        pltpu.VMEM((2, SCAN_TILE, 2 * DK), BF16),
                        pltpu.VMEM((D, PROJ * DK), BF16),
                        pltpu.SMEM((n_tiles,), jnp.int32)],
        compiler_params=_params("parallel", "arbitrary"),
        name="hgrn_scan",
    )(*args)


def kernel(x_prompt, x_sample, state_hgrn, c, c_ctx, ada_w, ada_b, norm_mix, norm_mlp, fnet_wo,
           hgrn_w_in, hgrn_lb, hgrn_norm, hgrn_wo, mlp_w1, mlp_w2, norm_final):
    bp, lp, _ = x_prompt.shape
    bs, ls, _ = x_sample.shape
    assert ada_w.shape[0] == 2 and ls == SEQ_PTS * SEQ_SLABS and lp == GD

    cs_ch, t_ctx, g_seq = _dft_tables()
    pos = _grid_pos_embed(ls, D)

    cond8 = jnp.zeros((8, D), F32).at[0].set(c_ctx).at[1:1 + bs].set(c)
    mod = _modulation(cond8, ada_w, ada_b)

    wo0 = fnet_wo.astype(BF16)
    wo1 = hgrn_wo.astype(BF16)
    w1 = mlp_w1.astype(BF16)
    w2 = mlp_w2.astype(BF16)
    w_heads = hgrn_w_in[0]
    gm = norm_mix.reshape(-1, 1, D)
    gf = norm_mlp.reshape(-1, 1, D)
    g_fin = norm_final.reshape(1, D)
    g_hn = hgrn_norm[0].reshape(1, DK)

    ctx_row = lambda r: 0
    smp_row = lambda r: 1 + r // ls

    mixed = _fnet_ctx(x_prompt, mod, 0, gm[0], cs_ch, t_ctx)
    x2, hn = _post_mixer(x_prompt.reshape(bp * lp, D), None, mixed.reshape(bp * lp, D), mod, 0,
                         ctx_row, wo0, w1, w2, gf[0], gm[1], final=False)
    o, new_state = _hgrn_mixer(hn.reshape(bp, lp, D), w_heads, hgrn_lb, g_hn, None)
    y_prompt = _post_mixer(x2, None, o.reshape(bp * lp, D), mod, 1, ctx_row, wo1, w1, w2,
                           gf[1], g_fin, final=True).reshape(bp, lp, D)

    mixed = _fnet_sample(x_sample, pos, mod, 0, gm[0], cs_ch, g_seq)
    x2, hn = _post_mixer(x_sample.reshape(bs * ls, D), pos, mixed.reshape(bs * ls, D), mod, 0,
                         smp_row, wo0, w1, w2, gf[0], gm[1], final=False)
    o, _ = _hgrn_mixer(hn.reshape(bs, ls, D), w_heads, hgrn_lb, g_hn, state_hgrn)
    y_sample = _post_mixer(x2, None, o.reshape(bs * ls, D), mod, 1, smp_row, wo1, w1, w2,
                           gf[1], g_fin, final=True).reshape(bs, ls, D)
    return (y_prompt, y_sample, new_state)
```

```python
import functools
import math

import numpy as np
import jax
import jax.numpy as jnp
from jax import lax
from jax.experimental import pallas as pl
from jax.experimental.pallas import tpu as pltpu

F32 = jnp.float32
BF16 = jnp.bfloat16

D = 1024
N_MOD = 6
D_FF = 4 * D
EPS = 1e-6
GROUPS = 4
GD = D // GROUPS
HEADS = 8
DK = 128
GRID_W = 64
POS_BASE = 10000.0
PROJ = 5

SEQ_SLABS = 16
SEQ_PTS = 256
LANES = 128
BF16_ROWS = 16
CHUNK = 64
SCAN_TILE = 256
EXP_SAFE = 85.0
POST_TM = 512
POST_SUB = 256
POST_TF = 1024

VMEM_LIMIT = 56 * 1024 * 1024


def _dft_tables():
    n = np.arange(GD)
    ang = 2.0 * np.pi * ((n[:, None] * n[None, :]) % GD) / GD
    cs_ch = np.concatenate([np.cos(ang), np.sin(ang)], axis=1) / math.sqrt(GD)
    t_ctx = np.concatenate([np.cos(ang), -np.sin(ang)], axis=1) / math.sqrt(GD)
    length = SEQ_PTS * SEQ_SLABS
    k1 = np.arange(SEQ_PTS)[:, None]
    n1 = np.arange(SEQ_PTS)[None, :]
    blocks = []
    for n2 in range(SEQ_SLABS):
        a = 2.0 * np.pi * ((k1 * (SEQ_SLABS * n1 + n2)) % length) / length
        gc, gs = np.cos(a) / math.sqrt(length), np.sin(a) / math.sqrt(length)
        blocks.append(np.block([[gc, -gs], [gs, gc]]))
    g_seq = np.stack(blocks)
    return tuple(jnp.asarray(t, F32).astype(BF16) for t in (cs_ch, t_ctx, g_seq))


def _grid_pos_embed(n_tok, d):
    rows = n_tok // GRID_W
    quarter = d // 4
    omega = 1.0 / (POS_BASE ** (jnp.arange(quarter, dtype=F32) / quarter))
    r = jnp.arange(rows, dtype=F32)[:, None] * omega[None, :]
    cl = jnp.arange(GRID_W, dtype=F32)[:, None] * omega[None, :]
    er = jnp.concatenate([jnp.sin(r), jnp.cos(r)], axis=-1)
    ec = jnp.concatenate([jnp.sin(cl), jnp.cos(cl)], axis=-1)
    emb = jnp.concatenate([jnp.broadcast_to(er[:, None, :], (rows, GRID_W, d // 2)),
                           jnp.broadcast_to(ec[None, :, :], (rows, GRID_W, d // 2))], axis=-1)
    return emb.reshape(n_tok, d)


def _silu(x):
    return x * jax.nn.sigmoid(x)


def _rms(x, g):
    return x * lax.rsqrt(jnp.mean(x * x, axis=-1, keepdims=True) + EPS) * g


def _mod_part(m, idx):
    return m[:, idx * D:(idx + 1) * D]


def _dot(a, b):
    return jnp.dot(a, b, preferred_element_type=F32)


def _dot_nt(a, b):
    return lax.dot_general(a, b, (((1,), (1,)), ((), ())), preferred_element_type=F32)


def _dot_tn(a, b):
    return lax.dot_general(a, b, (((0,), (0,)), ((), ())), preferred_element_type=F32)


def _params(*sem):
    return pltpu.CompilerParams(dimension_semantics=sem, vmem_limit_bytes=VMEM_LIMIT)


def _mod_kernel(c_ref, w_ref, b_ref, o_ref):
    s = _silu(c_ref[...])
    o_ref[0] = _dot(s.astype(BF16), w_ref[0].astype(BF16)) + b_ref[0]


def _modulation(cond8, ada_w, ada_b):
    depth = ada_w.shape[0]
    tn = 1536
    out = pl.pallas_call(
        _mod_kernel,
        grid=(depth, N_MOD * D // tn),
        in_specs=[pl.BlockSpec((8, D), lambda l, j: (0, 0)),
                  pl.BlockSpec((1, D, tn), lambda l, j: (l, 0, j)),
                  pl.BlockSpec((1, 1, tn), lambda l, j: (l, 0, j))],
        out_specs=pl.BlockSpec((1, 8, tn), lambda l, j: (l, 0, j)),
        out_shape=jax.ShapeDtypeStruct((depth, 8, N_MOD * D), F32),
        compiler_params=_params("parallel", "parallel"),
        name="adaln_mod",
    )(cond8, ada_w, ada_b.reshape(depth, 1, N_MOD * D))
    return out.reshape(depth, 8, 1, N_MOD * D)


def _channel_dft(hb, cs_ref, put):
    for g in range(GROUPS):
        y = _dot(hb[:, g * GD:(g + 1) * GD], cs_ref[...])
        put(0, g, y[:, :GD].astype(BF16))
        put(1, g, y[:, GD:].astype(BF16))


def _fnet_ctx_kernel(x_ref, mod_ref, g_ref, cs_ref, t_ref, o_ref, y_scr):
    m = mod_ref[0, 0]
    h = _rms(x_ref[0], g_ref[...]) * (1.0 + _mod_part(m, 1)) + _mod_part(m, 0)
    seq = x_ref.shape[1]

    def put(kind, g, val):
        y_scr[kind * seq:(kind + 1) * seq, g * GD:(g + 1) * GD] = val

    _channel_dft(h.astype(BF16), cs_ref, put)
    o_ref[0] = _dot(t_ref[...], y_scr[...]).astype(o_ref.dtype)


def _fnet_ctx(x, mod, layer, gain, cs_ch, t_ctx):
    b, seq, _ = x.shape
    return pl.pallas_call(
        _fnet_ctx_kernel,
        grid=(b,),
        in_specs=[pl.BlockSpec((1, seq, D), lambda i: (i, 0, 0)),
                  pl.BlockSpec((1, 1, 1, N_MOD * D), lambda i: (layer, 0, 0, 0)),
                  pl.BlockSpec((1, D), lambda i: (0, 0)),
                  pl.BlockSpec((GD, 2 * GD), lambda i: (0, 0)),
                  pl.BlockSpec((seq, 2 * seq), lambda i: (0, 0))],
        out_specs=pl.BlockSpec((1, seq, D), lambda i: (i, 0, 0)),
        out_shape=jax.ShapeDtypeStruct((b, seq, D), BF16),
        scratch_shapes=[pltpu.VMEM((2 * seq, D), BF16)],
        compiler_params=_params("parallel"),
        name="fnet_ctx",
    )(x, mod, gain, cs_ch, t_ctx)


def _fnet_slab_kernel(x_ref, pos_ref, mod_ref, g_ref, cs_ref, y_ref, col_scr, xp_scr):
    tm = x_ref.shape[1]
    rows = tm // SEQ_SLABS
    n_col = D // LANES
    for j in range(n_col):
        cols = slice(j * LANES, (j + 1) * LANES)
        col_scr[j] = x_ref[0, :, cols] + pos_ref[:, cols]
    for n2 in range(SEQ_SLABS):
        for j in range(n_col):
            xp_scr[n2 * rows:(n2 + 1) * rows, j * LANES:(j + 1) * LANES] = (
                col_scr[j, pl.ds(n2, rows, stride=SEQ_SLABS), :])
    m = mod_ref[0, 0]
    h = _rms(xp_scr[...], g_ref[...]) * (1.0 + _mod_part(m, 1)) + _mod_part(m, 0)

    def put(kind, g, val):
        for n2 in range(SEQ_SLABS):
            y_ref[0, n2, kind, :, g * GD:(g + 1) * GD] = val[n2 * rows:(n2 + 1) * rows, :]

    _channel_dft(h.astype(BF16), cs_ref, put)


def _fft_real_part(xs):
    n = len(xs)
    if n == 1:
        return xs
    ev = _fft_real_part(xs[0::2])
    od = _fft_real_part(xs[1::2])
    out = [None] * n
    for k in range(n // 2):
        c = math.cos(2.0 * math.pi * k / n)
        s = -math.sin(2.0 * math.pi * k / n)
        orr, oi = od[k]
        if k == 0:
            tr, ti = orr, oi
        elif 4 * k == n:
            tr, ti = oi, -orr
        else:
            tr, ti = orr * c - oi * s, orr * s + oi * c
        er, ei = ev[k]
        out[k] = (er + tr, ei + ti)
        out[k + n // 2] = (er - tr, ei - ti)
    return out


def _fnet_seq_kernel(y_ref, g_ref, o_ref, z_scr):
    tc = o_ref.shape[2]
    for n2 in range(SEQ_SLABS):
        yb = y_ref[0, n2].reshape(2 * SEQ_PTS, tc)
        z_scr[n2] = _dot(g_ref[n2], yb)

    rb = 8

    def body(r, carry):
        r0 = pl.multiple_of(r * rb, rb)
        for c in range(tc // LANES):
            cols = slice(c * LANES, (c + 1) * LANES)
            xs = []
            for n2 in range(SEQ_SLABS):
                re = z_scr[n2, pl.ds(r0, rb), cols]
                nim = z_scr[n2, pl.ds(SEQ_PTS + r0, rb), cols]
                xs.append((re, -nim))
            out = _fft_real_part(xs)
            for k2 in range(SEQ_SLABS):
                o_ref[0, pl.ds(k2 * SEQ_PTS + r0, rb), cols] = out[k2][0]
        return carry

    lax.fori_loop(0, SEQ_PTS // rb, body, 0)


def _fnet_sample(x, pos, mod, layer, gain, cs_ch, g_seq):
    b, seq, _ = x.shape
    tm = 512
    tc = 256
    y = pl.pallas_call(
        _fnet_slab_kernel,
        grid=(seq // tm, b),
        in_specs=[pl.BlockSpec((1, tm, D), lambda j, i: (i, j, 0)),
                  pl.BlockSpec((tm, D), lambda j, i: (j, 0)),
                  pl.BlockSpec((1, 1, 1, N_MOD * D), lambda j, i: (layer, 1 + i, 0, 0)),
                  pl.BlockSpec((1, D), lambda j, i: (0, 0)),
                  pl.BlockSpec((GD, 2 * GD), lambda j, i: (0, 0))],
        out_specs=pl.BlockSpec((1, SEQ_SLABS, 2, tm // SEQ_SLABS, D), lambda j, i: (i, 0, 0, j, 0)),
        out_shape=jax.ShapeDtypeStruct((b, SEQ_SLABS, 2, SEQ_PTS, D), BF16),
        scratch_shapes=[pltpu.VMEM((D // LANES, tm, LANES), F32), pltpu.VMEM((tm, D), F32)],
        compiler_params=_params("parallel", "parallel"),
        name="fnet_slab",
    )(x, pos, mod, gain, cs_ch)
    return pl.pallas_call(
        _fnet_seq_kernel,
        grid=(b, D // tc),
        in_specs=[pl.BlockSpec((1, SEQ_SLABS, 2, SEQ_PTS, tc), lambda i, j: (i, 0, 0, 0, j)),
                  pl.BlockSpec((SEQ_SLABS, 2 * SEQ_PTS, 2 * SEQ_PTS), lambda i, j: (0, 0, 0))],
        out_specs=pl.BlockSpec((1, seq, tc), lambda i, j: (i, 0, j)),
        out_shape=jax.ShapeDtypeStruct((b, seq, D), F32),
        scratch_shapes=[pltpu.VMEM((SEQ_SLABS, 2 * SEQ_PTS, tc), F32)],
        compiler_params=_params("parallel", "parallel"),
        name="fnet_seq",
    )(y, g_seq)


def _post_kernel(*refs, has_pos, final):
    it = iter(refs)
    x_ref = next(it)
    pos_ref = next(it) if has_pos else None
    a_ref, mod_ref, wo_ref, w1_ref, w2_ref, gmlp_ref, gnext_ref = (next(it) for _ in range(7))
    modn_ref = None if final else next(it)
    o_ref = next(it)
    hn_ref = None if final else next(it)
    x1_scr, h_scr, u_scr = (next(it) for _ in range(3))

    m = mod_ref[0, 0]
    subs = [slice(s * POST_SUB, (s + 1) * POST_SUB) for s in range(x_ref.shape[0] // POST_SUB)]

    for r in subs:
        x = x_ref[r, :]
        if has_pos:
            x = x + pos_ref[r, :]
        x1 = x + _mod_part(m, 2) * _dot(a_ref[r, :].astype(BF16), wo_ref[0])
        x1_scr[r, :] = x1
        h = _rms(x1, gmlp_ref[...]) * (1.0 + _mod_part(m, 4)) + _mod_part(m, 3)
        h_scr[r, :] = h.astype(BF16)

    for s, r in enumerate(subs):
        for j in range(D_FF // POST_TF):
            cols = slice(j * POST_TF, (j + 1) * POST_TF)
            u = jnp.maximum(_dot(h_scr[r, :], w1_ref[0, :, cols]), 0.0)
            u_scr[s, :, cols] = (u * u).astype(BF16)
        x2 = x1_scr[r, :] + _mod_part(m, 5) * _dot(u_scr[s], w2_ref[0])
        if final:
            o_ref[r, :] = _rms(x2, gnext_ref[...])
        else:
            o_ref[r, :] = x2
            mn = modn_ref[0, 0]
            hn = _rms(x2, gnext_ref[...]) * (1.0 + _mod_part(mn, 1)) + _mod_part(mn, 0)
            hn_ref[r, :] = hn.astype(BF16)


def _post_mixer(x, pos, a, mod, layer, mod_row, wo, w1, w2, g_mlp, g_next, final):
    rows = x.shape[0]
    tm = POST_TM
    has_pos = pos is not None
    n_pos = pos.shape[0] // tm if has_pos else 1
    row_map = lambda i: (i, 0)
    const = lambda i: (0, 0)
    resident = lambda shape: pl.BlockSpec(shape, lambda i: (layer, 0, 0), pipeline_mode=pl.Buffered(1))
    in_specs = [pl.BlockSpec((tm, D), row_map)]
    args = [x]
    if has_pos:
        in_specs.append(pl.BlockSpec((tm, D), lambda i: (i % n_pos, 0)))
        args.append(pos)
    in_specs += [pl.BlockSpec((tm, D), row_map),
                 pl.BlockSpec((1, 1, 1, N_MOD * D), lambda i: (layer, mod_row(i * tm), 0, 0)),
                 pl.BlockSpec((1, D, D), lambda i: (0, 0, 0), pipeline_mode=pl.Buffered(1)),
                 resident((1, D, D_FF)),
                 resident((1, D_FF, D)),
                 pl.BlockSpec((1, D), const),
                 pl.BlockSpec((1, D), const)]
    args += [a, mod, wo, w1, w2, g_mlp, g_next]
    out_specs = [pl.BlockSpec((tm, D), row_map)]
    out_shape = [jax.ShapeDtypeStruct((rows, D), F32)]
    if not final:
        in_specs.append(pl.BlockSpec((1, 1, 1, N_MOD * D),
                                     lambda i: (layer + 1, mod_row(i * tm), 0, 0)))
        args.append(mod)
        out_specs.append(pl.BlockSpec((tm, D), row_map))
        out_shape.append(jax.ShapeDtypeStruct((rows, D), BF16))
    res = pl.pallas_call(
        functools.partial(_post_kernel, has_pos=has_pos, final=final),
        grid=(rows // tm,),
        in_specs=in_specs,
        out_specs=out_specs,
        out_shape=out_shape,
        scratch_shapes=[pltpu.VMEM((tm, D), F32), pltpu.VMEM((tm, D), BF16),
                        pltpu.VMEM((tm // POST_SUB, POST_SUB, D_FF), BF16)],
        compiler_params=_params("parallel"),
        name="post_mlp_final" if final else "post_mlp",
    )(*args)
    return res[0] if final else (res[0], res[1])


PER_TILE = SCAN_TILE // CHUNK


def _decay_matrices():
    t = np.arange(SCAN_TILE)
    start = (t // CHUNK) * CHUNK
    first = np.arange(PER_TILE) * CHUNK
    same = (t[:, None] // CHUNK) == (t[None, :] // CHUNK)
    out = []
    for incl, mid, last in ((t[None, :] <= t[:, None], CHUNK // 2 - 1, CHUNK - 1),
                            (t[None, :] >= t[:, None], CHUNK // 2, 0)):
        bd = (same & incl).astype(np.float32)
        ref = bd[first + mid]
        pad = np.zeros((BF16_ROWS - 2 * PER_TILE, SCAN_TILE), np.float32)
        out.append(np.concatenate([bd - bd[start + mid], ref, bd[first + last] - ref, pad], axis=0))
    return jnp.asarray(np.stack(out), BF16)


def _hgrn_kernel(*refs, has_s0):
    it = iter(refs)
    hn_ref = next(it)
    wp_refs = [next(it) for _ in range(PROJ)]
    lb_ref, gn_ref, dm_ref = (next(it) for _ in range(3))
    s0_ref = next(it) if has_s0 else None
    o_ref, sf_ref = next(it), next(it)
    qd_scr, kd_scr, qb_scr, kl_scr, g_scr, o_scr, k_scr = ((next(it), next(it)) for _ in range(7))
    ext_scr, q_scr, v_scr, sg_scr, st_scr, sin_scr, u_scr, sb_scr, gs_scr, w_scr, flag_scr = (
        next(it) for _ in range(11))

    @pl.when(pl.program_id(1) == 0)
    def _():
        for j, wp_ref in enumerate(wp_refs):
            w_scr[:, j * DK:(j + 1) * DK] = wp_ref[...].astype(BF16)

    seq = hn_ref.shape[1]
    tp = SCAN_TILE
    n_tiles = seq // tp
    per_tile = PER_TILE

    lraw = lb_ref[...]
    mx = jnp.max(lraw, axis=1, keepdims=True)
    ex = jnp.exp(lraw - mx)
    sm = ex / jnp.sum(ex, axis=1, keepdims=True)
    lbv = (sm[:, 0, :] + sm[:, 1, :]) - sm[:, 0, :]

    def gates(i):
        rows = pl.ds(pl.multiple_of(i * tp, tp), tp)
        p = _dot(hn_ref[0, rows, :], w_scr[...])
        q_scr[rows, :] = _silu(p[:, 0:DK])
        v_scr[rows, :] = p[:, 3 * DK:4 * DK].astype(BF16)
        sg_scr[rows, :] = _silu(p[:, 4 * DK:5 * DK])
        for d in range(2):
            lb = lbv[d:d + 1, :]
            f = lb + (1.0 - lb) * jax.nn.sigmoid(p[:, (1 + d) * DK:(2 + d) * DK])
            k_scr[d][rows, :] = 1.0 - f
            g = jnp.log(f)
            g_scr[d][rows, :] = g
            g1 = g.astype(BF16)
            gs_scr[d, :, 0:DK] = g1
            gs_scr[d, :, DK:2 * DK] = (g - g1.astype(F32)).astype(BF16)

    def decay_sums():
        return [_dot(dm_ref[d], gs_scr[d]) for d in range(2)]

    def factors(i, sums):
        rows = pl.ds(pl.multiple_of(i * tp, tp), tp)
        q = q_scr[rows, :]
        unsafe = jnp.int32(0)
        for d in range(2):
            r = sums[d][:, :DK] + sums[d][:, DK:]
            a = r[:tp]
            ext = r[tp:]
            ext_scr[d, i] = ext
            unsafe = unsafe | (jnp.max(jnp.abs(a)) > EXP_SAFE).astype(jnp.int32)
            qd = q * jnp.exp(a)
            kd = k_scr[d][rows, :] * jnp.exp(-a)
            qd_scr[d][rows, :] = qd.astype(BF16)
            kd_scr[d][rows, :] = kd.astype(BF16)
            e_ref = jnp.exp(ext[0:per_tile])
            e_lr = jnp.exp(ext[per_tile:2 * per_tile])
            for c in range(per_tile):
                cr = pl.ds(pl.multiple_of(i * tp + c * CHUNK, CHUNK), CHUNK)
                cs = slice(c * CHUNK, (c + 1) * CHUNK)
                qb_scr[d][cr, :] = (qd[cs] * e_ref[c:c + 1]).astype(BF16)
                kl_scr[d][cr, :] = (kd[cs] * e_lr[c:c + 1]).astype(BF16)
        flag_scr[i] = unsafe

    for d in range(2):
        if has_s0:
            st_scr[d] = s0_ref[0, 0, d, 0].T
        else:
            st_scr[d] = jnp.zeros((DK, DK), F32)

    ti = lax.broadcasted_iota(jnp.int32, (tp, tp), 0)
    si = lax.broadcasted_iota(jnp.int32, (tp, tp), 1)
    same = (ti // CHUNK) == (si // CHUNK)
    masks = (same & (ti >= si), same & (si >= ti))

    def scan_head(d, tile):
        order = range(per_tile) if d == 0 else range(per_tile - 1, -1, -1)
        r0 = pl.multiple_of(tile * tp, tp)
        vv = v_scr[pl.ds(r0, tp), :]
        for c in range(per_tile):
            cr = pl.ds(pl.multiple_of(r0 + c * CHUNK, CHUNK), CHUNK)
            u_scr[d, c] = _dot_tn(vv[c * CHUNK:(c + 1) * CHUNK, :], kl_scr[d][cr, :])
        ext = ext_scr[d, tile]
        decay = jnp.exp(ext[0:per_tile] + ext[per_tile:2 * per_tile])
        st = st_scr[d]
        sin_scr[d] = st
        for c in order:
            sb_scr[d, c] = st.astype(BF16)
            st = st * decay[c:c + 1] + u_scr[d, c]
        st_scr[d] = st

    def scan_tail(d, tile):
        r0 = pl.multiple_of(tile * tp, tp)
        rows = pl.ds(r0, tp)
        sc = jnp.where(masks[d], _dot_nt(qd_scr[d][rows, :], kd_scr[d][rows, :]), 0.0)
        o_scr[d][rows, :] = _dot(sc.astype(BF16), v_scr[rows, :])
        for c in range(per_tile):
            cr = pl.ds(pl.multiple_of(r0 + c * CHUNK, CHUNK), CHUNK)
            o_scr[d][cr, :] += _dot_nt(qb_scr[d][cr, :], sb_scr[d, c])

    rid = lax.broadcasted_iota(jnp.int32, (BF16_ROWS, DK), 0)

    def redo_exact(d, tile):
        r0 = tile * tp

        def step(j, st):
            t = r0 + ((tp - 1 - j) if d == 1 else j)
            t0 = pl.multiple_of((t // BF16_ROWS) * BF16_ROWS, BF16_ROWS)
            grp = pl.ds(t0, BF16_ROWS)
            sel = rid == (t - t0)
            g = g_scr[d][grp, :]
            f = jnp.exp(jnp.sum(jnp.where(sel, g, 0.0), axis=0, keepdims=True))
            k = jnp.where(sel, 1.0 - jnp.exp(g), 0.0)
            st = st * f + _dot_tn(v_scr[grp, :], k.astype(BF16))
            o = _dot_nt(q_scr[grp, :].astype(BF16), st.astype(BF16))
            o_scr[d][grp, :] = jnp.where(sel, o, o_scr[d][grp, :])
            return st

        @pl.when(flag_scr[tile] != 0)
        def _():
            st_scr[d] = lax.fori_loop(0, tp, step, sin_scr[d])

    def finish(i):
        rows = pl.ds(pl.multiple_of(i * tp, tp), tp)
        o = o_scr[0][rows, :] + o_scr[1][rows, :]
        o = o * lax.rsqrt(jnp.mean(o * o, axis=-1, keepdims=True) + EPS) * gn_ref[...]
        o_ref[0, rows, :] = (o * sg_scr[rows, :]).astype(o_ref.dtype)

    last = n_tiles - 1
    gates(0)
    if n_tiles > 1:
        sums = decay_sums()
        gates(1)
        factors(0, sums)

        def forward(i, carry):
            scan_head(0, i - 1)
            sums = decay_sums()
            gates(i + 1)
            factors(i, sums)
            scan_tail(0, i - 1)
            redo_exact(0, i - 1)
            return carry

        lax.fori_loop(1, last, forward, 0)
        scan_head(0, last - 1)
        factors(last, decay_sums())
        scan_tail(0, last - 1)
        redo_exact(0, last - 1)
    else:
        factors(0, decay_sums())
    scan_head(0, last)
    scan_head(1, last)
    scan_tail(0, last)
    scan_tail(1, last)
    redo_exact(0, last)
    redo_exact(1, last)


    def backward(j, carry):
        t = last - j
        scan_head(1, t)
        finish(t + 1)
        scan_tail(1, t)
        redo_exact(1, t)
        return carry

    lax.fori_loop(1, n_tiles, backward, 0)
    finish(0)
    for d in range(2):
        sf_ref[0, 0, d, 0] = st_scr[d].T


def _hgrn_mixer(hn, w_in, lb_raw, g_norm, s0):
    b, seq, _ = hn.shape
    has_s0 = s0 is not None
    st_spec = pl.BlockSpec((1, 1, 2, 1, DK, DK), lambda h, i: (i, 0, 0, h, 0, 0))
    in_specs = [pl.BlockSpec((1, seq, D), lambda h, i: (i, 0, 0))]
    in_specs += [pl.BlockSpec((D, DK), functools.partial(lambda j, h, i: (0, j * HEADS + h), j))
                 for j in range(PROJ)]
    in_specs += [pl.BlockSpec((2, 2, DK), lambda h, i: (0, 0, h)),
                 pl.BlockSpec((1, DK), lambda h, i: (0, 0)),
                 pl.BlockSpec((2, SCAN_TILE + BF16_ROWS, SCAN_TILE), lambda h, i: (0, 0, 0))]
    args = [hn] + [w_in] * PROJ + [lb_raw, g_norm, _decay_matrices()]
    if has_s0:
        in_specs.append(st_spec)
        args.append(s0)
    n_tiles = seq // SCAN_TILE
    bf16_rows = pltpu.VMEM((seq, DK), BF16)
    f32_rows = pltpu.VMEM((seq, DK), F32)
    return pl.pallas_call(
        functools.partial(_hgrn_kernel, has_s0=has_s0),
        grid=(HEADS, b),
        in_specs=in_specs,
        out_specs=[pl.BlockSpec((1, seq, DK), lambda h, i: (i, 0, h)), st_spec],
        out_shape=[jax.ShapeDtypeStruct((b, seq, D), BF16),
                   jax.ShapeDtypeStruct((b, 1, 2, HEADS, DK, DK), F32)],
        scratch_shapes=[bf16_rows] * 8 + [f32_rows] * 6 + [
                        pltpu.VMEM((2, n_tiles, BF16_ROWS, DK), F32),
                        f32_rows, bf16_rows, f32_rows,
                        pltpu.VMEM((2, DK, DK), F32),
                        pltpu.VMEM((2, DK, DK), F32),
                        pltpu.VMEM((2, PER_TILE, DK, DK), F32),
                        pltpu.VMEM((2, PER_TILE, DK, DK), BF16),
                        pltpu.VMEM((2, SCAN_TILE, 2 * DK), BF16),
                        pltpu.VMEM((D, PROJ * DK), BF16),
                        pltpu.SMEM((n_tiles,), jnp.int32)],
        compiler_params=_params("parallel", "arbitrary"),
        name="hgrn_scan",
    )(*args)


def kernel(x_prompt, x_sample, state_hgrn, c, c_ctx, ada_w, ada_b, norm_mix, norm_mlp, fnet_wo,
           hgrn_w_in, hgrn_lb, hgrn_norm, hgrn_wo, mlp_w1, mlp_w2, norm_final):
    bp, lp, _ = x_prompt.shape
    bs, ls, _ = x_sample.shape
    assert ada_w.shape[0] == 2 and ls == SEQ_PTS * SEQ_SLABS and lp == GD

    cs_ch, t_ctx, g_seq = _dft_tables()
    pos = _grid_pos_embed(ls, D)

    cond8 = jnp.zeros((8, D), F32).at[0].set(c_ctx).at[1:1 + bs].set(c)
    mod = _modulation(cond8, ada_w, ada_b)

    wo0 = fnet_wo.astype(BF16)
    wo1 = hgrn_wo.astype(BF16)
    w1 = mlp_w1.astype(BF16)
    w2 = mlp_w2.astype(BF16)
    w_heads = hgrn_w_in[0]
    gm = norm_mix.reshape(-1, 1, D)
    gf = norm_mlp.reshape(-1, 1, D)
    g_fin = norm_final.reshape(1, D)
    g_hn = hgrn_norm[0].reshape(1, DK)

    ctx_row = lambda r: 0
    smp_row = lambda r: 1 + r // ls

    mixed = _fnet_ctx(x_prompt, mod, 0, gm[0], cs_ch, t_ctx)
    x2, hn = _post_mixer(x_prompt.reshape(bp * lp, D), None, mixed.reshape(bp * lp, D), mod, 0,
                         ctx_row, wo0, w1, w2, gf[0], gm[1], final=False)
    o, new_state = _hgrn_mixer(hn.reshape(bp, lp, D), w_heads, hgrn_lb, g_hn, None)
    y_prompt = _post_mixer(x2, None, o.reshape(bp * lp, D), mod, 1, ctx_row, wo1, w1, w2,
                           gf[1], g_fin, final=True).reshape(bp, lp, D)

    mixed = _fnet_sample(x_sample, pos, mod, 0, gm[0], cs_ch, g_seq)
    x2, hn = _post_mixer(x_sample.reshape(bs * ls, D), pos, mixed.reshape(bs * ls, D), mod, 0,
                         smp_row, wo0, w1, w2, gf[0], gm[1], final=False)
    o, _ = _hgrn_mixer(hn.reshape(bs, ls, D), w_heads, hgrn_lb, g_hn, state_hgrn)
    y_sample = _post_mixer(x2, None, o.reshape(bs * ls, D), mod, 1, smp_row, wo1, w1, w2,
                           gf[1], g_fin, final=True).reshape(bs, ls, D)
    return (y_prompt, y_sample, new_state)
```

```python
import functools
import math

import numpy as np
import jax
import jax.numpy as jnp
from jax import lax
from jax.experimental import pallas as pl
from jax.experimental.pallas import tpu as pltpu

F32 = jnp.float32
BF16 = jnp.bfloat16

D = 1024
N_MOD = 6
D_FF = 4 * D
EPS = 1e-6
GROUPS = 4
GD = D // GROUPS
HEADS = 8
DK = 128
GRID_W = 64
POS_BASE = 10000.0
PROJ = 5

SEQ_SLABS = 16
SEQ_PTS = 256
LANES = 128
BF16_ROWS = 16
CHUNK = 64
SCAN_TILE = 256
EXP_SAFE = 85.0
POST_TM = 512
POST_SUB = 256
POST_TF = 1024

VMEM_LIMIT = 56 * 1024 * 1024


def _dft_tables():
    n = np.arange(GD)
    ang = 2.0 * np.pi * ((n[:, None] * n[None, :]) % GD) / GD
    cs_ch = np.concatenate([np.cos(ang), np.sin(ang)], axis=1) / math.sqrt(GD)
    t_ctx = np.concatenate([np.cos(ang), -np.sin(ang)], axis=1) / math.sqrt(GD)
    length = SEQ_PTS * SEQ_SLABS
    k1 = np.arange(SEQ_PTS)[:, None]
    n1 = np.arange(SEQ_PTS)[None, :]
    blocks = []
    for n2 in range(SEQ_SLABS):
        a = 2.0 * np.pi * ((k1 * (SEQ_SLABS * n1 + n2)) % length) / length
        gc, gs = np.cos(a) / math.sqrt(length), np.sin(a) / math.sqrt(length)
        blocks.append(np.block([[gc, -gs], [gs, gc]]))
    g_seq = np.stack(blocks)
    return tuple(jnp.asarray(t, F32).astype(BF16) for t in (cs_ch, t_ctx, g_seq))


def _grid_pos_embed(n_tok, d):
    rows = n_tok // GRID_W
    quarter = d // 4
    omega = 1.0 / (POS_BASE ** (jnp.arange(quarter, dtype=F32) / quarter))
    r = jnp.arange(rows, dtype=F32)[:, None] * omega[None, :]
    cl = jnp.arange(GRID_W, dtype=F32)[:, None] * omega[None, :]
    er = jnp.concatenate([jnp.sin(r), jnp.cos(r)], axis=-1)
    ec = jnp.concatenate([jnp.sin(cl), jnp.cos(cl)], axis=-1)
    emb = jnp.concatenate([jnp.broadcast_to(er[:, None, :], (rows, GRID_W, d // 2)),
                           jnp.broadcast_to(ec[None, :, :], (rows, GRID_W, d // 2))], axis=-1)
    return emb.reshape(n_tok, d)


def _silu(x):
    return x * jax.nn.sigmoid(x)


def _rms(x, g):
    return x * lax.rsqrt(jnp.mean(x * x, axis=-1, keepdims=True) + EPS) * g


def _mod_part(m, idx):
    return m[:, idx * D:(idx + 1) * D]


def _dot(a, b):
    return jnp.dot(a, b, preferred_element_type=F32)


def _dot_nt(a, b):
    return lax.dot_general(a, b, (((1,), (1,)), ((), ())), preferred_element_type=F32)


def _dot_tn(a, b):
    return lax.dot_general(a, b, (((0,), (0,)), ((), ())), preferred_element_type=F32)


def _params(*sem, **kw):
    return pltpu.CompilerParams(dimension_semantics=sem, vmem_limit_bytes=VMEM_LIMIT, **kw)


def _mod_kernel(c_ref, w_ref, b_ref, o_ref):
    s = _silu(c_ref[...])
    o_ref[0] = _dot(s.astype(BF16), w_ref[0].astype(BF16)) + b_ref[0]


def _modulation(cond8, ada_w, ada_b):
    depth = ada_w.shape[0]
    tn = 1536
    out = pl.pallas_call(
        _mod_kernel,
        grid=(depth, N_MOD * D // tn),
        in_specs=[pl.BlockSpec((8, D), lambda l, j: (0, 0)),
                  pl.BlockSpec((1, D, tn), lambda l, j: (l, 0, j)),
                  pl.BlockSpec((1, 1, tn), lambda l, j: (l, 0, j))],
        out_specs=pl.BlockSpec((1, 8, tn), lambda l, j: (l, 0, j)),
        out_shape=jax.ShapeDtypeStruct((depth, 8, N_MOD * D), F32),
        compiler_params=_params("parallel", "parallel"),
        name="adaln_mod",
    )(cond8, ada_w, ada_b.reshape(depth, 1, N_MOD * D))
    return out.reshape(depth, 8, 1, N_MOD * D)


def _channel_dft(hb, cs_ref, put):
    for g in range(GROUPS):
        y = _dot(hb[:, g * GD:(g + 1) * GD], cs_ref[...])
        put(0, g, y[:, :GD].astype(BF16))
        put(1, g, y[:, GD:].astype(BF16))


def _fnet_ctx_kernel(x_ref, mod_ref, g_ref, cs_ref, t_ref, o_ref, y_scr):
    m = mod_ref[0, 0]
    h = _rms(x_ref[0], g_ref[...]) * (1.0 + _mod_part(m, 1)) + _mod_part(m, 0)
    seq = x_ref.shape[1]

    def put(kind, g, val):
        y_scr[kind * seq:(kind + 1) * seq, g * GD:(g + 1) * GD] = val

    _channel_dft(h.astype(BF16), cs_ref, put)
    o_ref[0] = _dot(t_ref[...], y_scr[...]).astype(o_ref.dtype)


def _fnet_ctx(x, mod, layer, gain, cs_ch, t_ctx):
    b, seq, _ = x.shape
    return pl.pallas_call(
        _fnet_ctx_kernel,
        grid=(b,),
        in_specs=[pl.BlockSpec((1, seq, D), lambda i: (i, 0, 0)),
                  pl.BlockSpec((1, 1, 1, N_MOD * D), lambda i: (layer, 0, 0, 0)),
                  pl.BlockSpec((1, D), lambda i: (0, 0)),
                  pl.BlockSpec((GD, 2 * GD), lambda i: (0, 0)),
                  pl.BlockSpec((seq, 2 * seq), lambda i: (0, 0))],
        out_specs=pl.BlockSpec((1, seq, D), lambda i: (i, 0, 0)),
        out_shape=jax.ShapeDtypeStruct((b, seq, D), BF16),
        scratch_shapes=[pltpu.VMEM((2 * seq, D), BF16)],
        compiler_params=_params("parallel"),
        name="fnet_ctx",
    )(x, mod, gain, cs_ch, t_ctx)


def _fnet_slab_kernel(x_ref, pos_ref, mod_ref, g_ref, cs_ref, y_ref, col_scr, xp_scr):
    tm = x_ref.shape[1]
    rows = tm // SEQ_SLABS
    n_col = D // LANES
    for j in range(n_col):
        cols = slice(j * LANES, (j + 1) * LANES)
        col_scr[j] = x_ref[0, :, cols] + pos_ref[:, cols]
    for n2 in range(SEQ_SLABS):
        for j in range(n_col):
            xp_scr[n2 * rows:(n2 + 1) * rows, j * LANES:(j + 1) * LANES] = (
                col_scr[j, pl.ds(n2, rows, stride=SEQ_SLABS), :])
    m = mod_ref[0, 0]
    h = _rms(xp_scr[...], g_ref[...]) * (1.0 + _mod_part(m, 1)) + _mod_part(m, 0)

    def put(kind, g, val):
        for n2 in range(SEQ_SLABS):
            y_ref[0, n2, kind, :, g * GD:(g + 1) * GD] = val[n2 * rows:(n2 + 1) * rows, :]

    _channel_dft(h.astype(BF16), cs_ref, put)


def _fft_real_part(xs):
    n = len(xs)
    if n == 1:
        return xs
    ev = _fft_real_part(xs[0::2])
    od = _fft_real_part(xs[1::2])
    out = [None] * n
    for k in range(n // 2):
        c = math.cos(2.0 * math.pi * k / n)
        s = -math.sin(2.0 * math.pi * k / n)
        orr, oi = od[k]
        if k == 0:
            tr, ti = orr, oi
        elif 4 * k == n:
            tr, ti = oi, -orr
        else:
            tr, ti = orr * c - oi * s, orr * s + oi * c
        er, ei = ev[k]
        out[k] = (er + tr, ei + ti)
        out[k + n // 2] = (er - tr, ei - ti)
    return out


def _fnet_seq_kernel(y_ref, g_ref, o_ref, z_scr):
    tc = o_ref.shape[2]
    for n2 in range(SEQ_SLABS):
        yb = y_ref[0, n2].reshape(2 * SEQ_PTS, tc)
        z_scr[n2] = _dot(g_ref[n2], yb)

    rb = 8

    def body(r, carry):
        r0 = pl.multiple_of(r * rb, rb)
        for c in range(tc // LANES):
            cols = slice(c * LANES, (c + 1) * LANES)
            xs = []
            for n2 in range(SEQ_SLABS):
                re = z_scr[n2, pl.ds(r0, rb), cols]
                nim = z_scr[n2, pl.ds(SEQ_PTS + r0, rb), cols]
                xs.append((re, -nim))
            out = _fft_real_part(xs)
            for k2 in range(SEQ_SLABS):
                o_ref[0, pl.ds(k2 * SEQ_PTS + r0, rb), cols] = out[k2][0]
        return carry

    lax.fori_loop(0, SEQ_PTS // rb, body, 0)


def _fnet_sample(x, pos, mod, layer, gain, cs_ch, g_seq):
    b, seq, _ = x.shape
    tm = 512
    tc = 256
    y = pl.pallas_call(
        _fnet_slab_kernel,
        grid=(seq // tm, b),
        in_specs=[pl.BlockSpec((1, tm, D), lambda j, i: (i, j, 0)),
                  pl.BlockSpec((tm, D), lambda j, i: (j, 0)),
                  pl.BlockSpec((1, 1, 1, N_MOD * D), lambda j, i: (layer, 1 + i, 0, 0)),
                  pl.BlockSpec((1, D), lambda j, i: (0, 0)),
                  pl.BlockSpec((GD, 2 * GD), lambda j, i: (0, 0))],
        out_specs=pl.BlockSpec((1, SEQ_SLABS, 2, tm // SEQ_SLABS, D), lambda j, i: (i, 0, 0, j, 0)),
        out_shape=jax.ShapeDtypeStruct((b, SEQ_SLABS, 2, SEQ_PTS, D), BF16),
        scratch_shapes=[pltpu.VMEM((D // LANES, tm, LANES), F32), pltpu.VMEM((tm, D), F32)],
        compiler_params=_params("parallel", "parallel"),
        name="fnet_slab",
    )(x, pos, mod, gain, cs_ch)
    return pl.pallas_call(
        _fnet_seq_kernel,
        grid=(b, D // tc),
        in_specs=[pl.BlockSpec((1, SEQ_SLABS, 2, SEQ_PTS, tc), lambda i, j: (i, 0, 0, 0, j)),
                  pl.BlockSpec((SEQ_SLABS, 2 * SEQ_PTS, 2 * SEQ_PTS), lambda i, j: (0, 0, 0))],
        out_specs=pl.BlockSpec((1, seq, tc), lambda i, j: (i, 0, j)),
        out_shape=jax.ShapeDtypeStruct((b, seq, D), F32),
        scratch_shapes=[pltpu.VMEM((SEQ_SLABS, 2 * SEQ_PTS, tc), F32)],
        compiler_params=_params("parallel", "parallel"),
        name="fnet_seq",
    )(y, g_seq)


def _post_kernel(*refs, has_pos, final):
    it = iter(refs)
    x_ref = next(it)
    pos_ref = next(it) if has_pos else None
    a_ref, mod_ref, wo_ref, w1_ref, w2_ref, gmlp_ref, gnext_ref = (next(it) for _ in range(7))
    modn_ref = None if final else next(it)
    o_ref = next(it)
    hn_ref = None if final else next(it)
    x1_scr, h_scr, u_scr = (next(it) for _ in range(3))

    m = mod_ref[0, 0]
    subs = [slice(s * POST_SUB, (s + 1) * POST_SUB) for s in range(x_ref.shape[0] // POST_SUB)]

    for r in subs:
        x = x_ref[r, :]
        if has_pos:
            x = x + pos_ref[r, :]
        x1 = x + _mod_part(m, 2) * _dot(a_ref[r, :].astype(BF16), wo_ref[0])
        x1_scr[r, :] = x1
        h = _rms(x1, gmlp_ref[...]) * (1.0 + _mod_part(m, 4)) + _mod_part(m, 3)
        h_scr[r, :] = h.astype(BF16)

    for s, r in enumerate(subs):
        for j in range(D_FF // POST_TF):
            cols = slice(j * POST_TF, (j + 1) * POST_TF)
            u = jnp.maximum(_dot(h_scr[r, :], w1_ref[0, :, cols]), 0.0)
            u_scr[s, :, cols] = (u * u).astype(BF16)
        x2 = x1_scr[r, :] + _mod_part(m, 5) * _dot(u_scr[s], w2_ref[0])
        if final:
            o_ref[r, :] = _rms(x2, gnext_ref[...])
        else:
            o_ref[r, :] = x2
            mn = modn_ref[0, 0]
            hn = _rms(x2, gnext_ref[...]) * (1.0 + _mod_part(mn, 1)) + _mod_part(mn, 0)
            hn_ref[r, :] = hn.astype(BF16)


def _post_mixer(x, pos, a, mod, layer, mod_row, wo, w1, w2, g_mlp, g_next, final):
    rows = x.shape[0]
    tm = POST_TM
    has_pos = pos is not None
    n_pos = pos.shape[0] // tm if has_pos else 1
    row_map = lambda i: (i, 0)
    const = lambda i: (0, 0)
    resident = lambda shape: pl.BlockSpec(shape, lambda i: (layer, 0, 0), pipeline_mode=pl.Buffered(1))
    in_specs = [pl.BlockSpec((tm, D), row_map)]
    args = [x]
    if has_pos:
        in_specs.append(pl.BlockSpec((tm, D), lambda i: (i % n_pos, 0)))
        args.append(pos)
    in_specs += [pl.BlockSpec((tm, D), row_map),
                 pl.BlockSpec((1, 1, 1, N_MOD * D), lambda i: (layer, mod_row(i * tm), 0, 0)),
                 pl.BlockSpec((1, D, D), lambda i: (0, 0, 0), pipeline_mode=pl.Buffered(1)),
                 resident((1, D, D_FF)),
                 resident((1, D_FF, D)),
                 pl.BlockSpec((1, D), const),
                 pl.BlockSpec((1, D), const)]
    args += [a, mod, wo, w1, w2, g_mlp, g_next]
    out_specs = [pl.BlockSpec((tm, D), row_map)]
    out_shape = [jax.ShapeDtypeStruct((rows, D), F32)]
    if not final:
        in_specs.append(pl.BlockSpec((1, 1, 1, N_MOD * D),
                                     lambda i: (layer + 1, mod_row(i * tm), 0, 0)))
        args.append(mod)
        out_specs.append(pl.BlockSpec((tm, D), row_map))
        out_shape.append(jax.ShapeDtypeStruct((rows, D), BF16))
    res = pl.pallas_call(
        functools.partial(_post_kernel, has_pos=has_pos, final=final),
        grid=(rows // tm,),
        in_specs=in_specs,
        out_specs=out_specs,
        out_shape=out_shape,
        scratch_shapes=[pltpu.VMEM((tm, D), F32), pltpu.VMEM((tm, D), BF16),
                        pltpu.VMEM((tm // POST_SUB, POST_SUB, D_FF), BF16)],
        compiler_params=_params("parallel"),
        name="post_mlp_final" if final else "post_mlp",
    )(*args)
    return res[0] if final else (res[0], res[1])


PER_TILE = SCAN_TILE // CHUNK


def _decay_matrices():
    t = np.arange(SCAN_TILE)
    start = (t // CHUNK) * CHUNK
    first = np.arange(PER_TILE) * CHUNK
    same = (t[:, None] // CHUNK) == (t[None, :] // CHUNK)
    out = []
    for incl, mid, last in ((t[None, :] <= t[:, None], CHUNK // 2 - 1, CHUNK - 1),
                            (t[None, :] >= t[:, None], CHUNK // 2, 0)):
        bd = (same & incl).astype(np.float32)
        ref = bd[first + mid]
        pad = np.zeros((BF16_ROWS - 2 * PER_TILE, SCAN_TILE), np.float32)
        out.append(np.concatenate([bd - bd[start + mid], ref, bd[first + last] - ref, pad], axis=0))
    return jnp.asarray(np.stack(out), BF16)


def _hgrn_kernel(*refs, has_s0):
    it = iter(refs)
    hn_ref = next(it)
    wp_refs = [next(it) for _ in range(PROJ)]
    lb_ref, gn_ref, dm_ref = (next(it) for _ in range(3))
    s0_ref = next(it) if has_s0 else None
    o_ref, sf_ref = next(it), next(it)
    qd_scr, kd_scr, qb_scr, kl_scr, g_scr, o_scr, k_scr = ((next(it), next(it)) for _ in range(7))
    ext_scr, q_scr, v_scr, sg_scr, st_scr, u_scr, sb_scr, gs_scr, w_scr, flag_scr = (
        next(it) for _ in range(10))

    @pl.when(pl.program_id(1) == 0)
    def _():
        for j, wp_ref in enumerate(wp_refs):
            w_scr[:, j * DK:(j + 1) * DK] = wp_ref[...].astype(BF16)

    seq = hn_ref.shape[1]
    tp = SCAN_TILE
    n_tiles = seq // tp
    per_tile = PER_TILE

    lraw = lb_ref[...]
    mx = jnp.max(lraw, axis=1, keepdims=True)
    ex = jnp.exp(lraw - mx)
    sm = ex / jnp.sum(ex, axis=1, keepdims=True)
    lbv = (sm[:, 0, :] + sm[:, 1, :]) - sm[:, 0, :]

    def gates(i):
        rows = pl.ds(pl.multiple_of(i * tp, tp), tp)
        p = _dot(hn_ref[0, rows, :], w_scr[...])
        q_scr[rows, :] = _silu(p[:, 0:DK])
        v_scr[rows, :] = p[:, 3 * DK:4 * DK].astype(BF16)
        sg_scr[rows, :] = _silu(p[:, 4 * DK:5 * DK])
        for d in range(2):
            lb = lbv[d:d + 1, :]
            f = lb + (1.0 - lb) * jax.nn.sigmoid(p[:, (1 + d) * DK:(2 + d) * DK])
            k_scr[d][rows, :] = 1.0 - f
            g = jnp.log(f)
            g_scr[d][rows, :] = g
            g1 = g.astype(BF16)
            gs_scr[d, :, 0:DK] = g1
            gs_scr[d, :, DK:2 * DK] = (g - g1.astype(F32)).astype(BF16)

    def decay_sums():
        return [_dot(dm_ref[d], gs_scr[d]) for d in range(2)]

    def factors(i, sums):
        rows = pl.ds(pl.multiple_of(i * tp, tp), tp)
        q = q_scr[rows, :]
        unsafe = jnp.int32(0)
        for d in range(2):
            r = sums[d][:, :DK] + sums[d][:, DK:]
            a = r[:tp]
            ext = r[tp:]
            ext_scr[d, i] = ext
            unsafe = unsafe | (jnp.max(jnp.abs(a)) > EXP_SAFE).astype(jnp.int32)
            qd = q * jnp.exp(a)
            kd = k_scr[d][rows, :] * jnp.exp(-a)
            qd_scr[d][rows, :] = qd.astype(BF16)
            kd_scr[d][rows, :] = kd.astype(BF16)
            e_ref = jnp.exp(ext[0:per_tile])
            e_lr = jnp.exp(ext[per_tile:2 * per_tile])
            for c in range(per_tile):
                cr = pl.ds(pl.multiple_of(i * tp + c * CHUNK, CHUNK), CHUNK)
                cs = slice(c * CHUNK, (c + 1) * CHUNK)
                qb_scr[d][cr, :] = (qd[cs] * e_ref[c:c + 1]).astype(BF16)
                kl_scr[d][cr, :] = (kd[cs] * e_lr[c:c + 1]).astype(BF16)
        flag_scr[i] = unsafe

    def init_states():
        for d in range(2):
            if has_s0:
                st_scr[d] = s0_ref[0, 0, d, 0].T
            else:
                st_scr[d] = jnp.zeros((DK, DK), F32)

    init_states()

    ti = lax.broadcasted_iota(jnp.int32, (tp, tp), 0)
    si = lax.broadcasted_iota(jnp.int32, (tp, tp), 1)
    same = (ti // CHUNK) == (si // CHUNK)
    masks = (same & (ti >= si), same & (si >= ti))

    def scan_head(d, tile):
        order = range(per_tile) if d == 0 else range(per_tile - 1, -1, -1)
        r0 = pl.multiple_of(tile * tp, tp)
        vv = v_scr[pl.ds(r0, tp), :]
        for c in range(per_tile):
            cr = pl.ds(pl.multiple_of(r0 + c * CHUNK, CHUNK), CHUNK)
            u_scr[d, c] = _dot_tn(vv[c * CHUNK:(c + 1) * CHUNK, :], kl_scr[d][cr, :])
        ext = ext_scr[d, tile]
        decay = jnp.exp(ext[0:per_tile] + ext[per_tile:2 * per_tile])
        st = st_scr[d]
        for c in order:
            sb_scr[d, c] = st.astype(BF16)
            st = st * decay[c:c + 1] + u_scr[d, c]
        st_scr[d] = st

    def scan_tail(d, tile):
        r0 = pl.multiple_of(tile * tp, tp)
        rows = pl.ds(r0, tp)
        sc = jnp.where(masks[d], _dot_nt(qd_scr[d][rows, :], kd_scr[d][rows, :]), 0.0)
        o_scr[d][rows, :] = _dot(sc.astype(BF16), v_scr[rows, :])
        for c in range(per_tile):
            cr = pl.ds(pl.multiple_of(r0 + c * CHUNK, CHUNK), CHUNK)
            o_scr[d][cr, :] += _dot_nt(qb_scr[d][cr, :], sb_scr[d, c])

    rid = lax.broadcasted_iota(jnp.int32, (BF16_ROWS, DK), 0)

    def scan_exact(d, tile):
        r0 = tile * tp

        def step(j, st):
            t = r0 + ((tp - 1 - j) if d == 1 else j)
            t0 = pl.multiple_of((t // BF16_ROWS) * BF16_ROWS, BF16_ROWS)
            grp = pl.ds(t0, BF16_ROWS)
            sel = rid == (t - t0)
            g = g_scr[d][grp, :]
            f = jnp.exp(jnp.sum(jnp.where(sel, g, 0.0), axis=0, keepdims=True))
            k = jnp.where(sel, 1.0 - jnp.exp(g), 0.0)
            st = st * f + _dot_tn(v_scr[grp, :], k.astype(BF16))
            o = _dot_nt(q_scr[grp, :].astype(BF16), st.astype(BF16))
            o_scr[d][grp, :] = jnp.where(sel, o, o_scr[d][grp, :])
            return st

        st_scr[d] = lax.fori_loop(0, tp, step, st_scr[d])

    def finish(i):
        rows = pl.ds(pl.multiple_of(i * tp, tp), tp)
        o = o_scr[0][rows, :] + o_scr[1][rows, :]
        o = o * lax.rsqrt(jnp.mean(o * o, axis=-1, keepdims=True) + EPS) * gn_ref[...]
        o_ref[0, rows, :] = (o * sg_scr[rows, :]).astype(o_ref.dtype)

    last = n_tiles - 1
    gates(0)
    if n_tiles > 1:
        sums = decay_sums()
        gates(1)
        factors(0, sums)

        def forward(i, carry):
            scan_head(0, i - 1)
            sums = decay_sums()
            gates(i + 1)
            factors(i, sums)
            scan_tail(0, i - 1)
            return carry

        lax.fori_loop(1, last, forward, 0, unroll=2)
        scan_head(0, last - 1)
        factors(last, decay_sums())
        scan_tail(0, last - 1)
    else:
        factors(0, decay_sums())
    scan_head(0, last)
    scan_head(1, last)
    scan_tail(0, last)
    scan_tail(1, last)

    def backward(j, carry):
        t = last - j
        scan_head(1, t)
        finish(t + 1)
        scan_tail(1, t)
        return carry

    lax.fori_loop(1, n_tiles, backward, 0, unroll=3)
    finish(0)

    unsafe = flag_scr[0]
    for t in range(1, n_tiles):
        unsafe = unsafe | flag_scr[t]

    @pl.when(unsafe != 0)
    def _():
        init_states()

        def resweep(d):
            def body(j, carry):
                t = j if d == 0 else last - j

                @pl.when(flag_scr[t] != 0)
                def _():
                    scan_exact(d, t)

                @pl.when(flag_scr[t] == 0)
                def _():
                    scan_head(d, t)
                    scan_tail(d, t)

                if d == 1:
                    finish(t)
                return carry

            lax.fori_loop(0, n_tiles, body, 0)

        resweep(0)
        resweep(1)

    for d in range(2):
        sf_ref[0, 0, d, 0] = st_scr[d].T


def _hgrn_mixer(hn, w_in, lb_raw, g_norm, s0):
    b, seq, _ = hn.shape
    has_s0 = s0 is not None
    st_spec = pl.BlockSpec((1, 1, 2, 1, DK, DK), lambda h, i: (i, 0, 0, h, 0, 0))
    in_specs = [pl.BlockSpec((1, seq, D), lambda h, i: (i, 0, 0))]
    in_specs += [pl.BlockSpec((D, DK), functools.partial(lambda j, h, i: (0, j * HEADS + h), j))
                 for j in range(PROJ)]
    in_specs += [pl.BlockSpec((2, 2, DK), lambda h, i: (0, 0, h)),
                 pl.BlockSpec((1, DK), lambda h, i: (0, 0)),
                 pl.BlockSpec((2, SCAN_TILE + BF16_ROWS, SCAN_TILE), lambda h, i: (0, 0, 0))]
    args = [hn] + [w_in] * PROJ + [lb_raw, g_norm, _decay_matrices()]
    if has_s0:
        in_specs.append(st_spec)
        args.append(s0)
    n_tiles = seq // SCAN_TILE
    bf16_rows = pltpu.VMEM((seq, DK), BF16)
    f32_rows = pltpu.VMEM((seq, DK), F32)
    return pl.pallas_call(
        functools.partial(_hgrn_kernel, has_s0=has_s0),
        grid=(HEADS, b),
        in_specs=in_specs,
        out_specs=[pl.BlockSpec((1, seq, DK), lambda h, i: (i, 0, h)), st_spec],
        out_shape=[jax.ShapeDtypeStruct((b, seq, D), BF16),
                   jax.ShapeDtypeStruct((b, 1, 2, HEADS, DK, DK), F32)],
        scratch_shapes=[bf16_rows] * 8 + [f32_rows] * 6 + [
                        pltpu.VMEM((2, n_tiles, BF16_ROWS, DK), F32),
                        f32_rows, bf16_rows, f32_rows,
                        pltpu.VMEM((2, DK, DK), F32),
                        pltpu.VMEM((2, PER_TILE, DK, DK), F32),
                        pltpu.VMEM((2, PER_TILE, DK, DK), BF16),
                        pltpu.VMEM((2, SCAN_TILE, 2 * DK), BF16),
                        pltpu.VMEM((D, PROJ * DK), BF16),
                        pltpu.SMEM((n_tiles,), jnp.int32)],
        compiler_params=_params("parallel", "arbitrary"),
        name="hgrn_scan",
    )(*args)


def kernel(x_prompt, x_sample, state_hgrn, c, c_ctx, ada_w, ada_b, norm_mix, norm_mlp, fnet_wo,
           hgrn_w_in, hgrn_lb, hgrn_norm, hgrn_wo, mlp_w1, mlp_w2, norm_final):
    bp, lp, _ = x_prompt.shape
    bs, ls, _ = x_sample.shape
    assert ada_w.shape[0] == 2 and ls == SEQ_PTS * SEQ_SLABS and lp == GD

    cs_ch, t_ctx, g_seq = _dft_tables()
    pos = _grid_pos_embed(ls, D)

    cond8 = jnp.zeros((8, D), F32).at[0].set(c_ctx).at[1:1 + bs].set(c)
    mod = _modulation(cond8, ada_w, ada_b)

    wo0 = fnet_wo.astype(BF16)
    wo1 = hgrn_wo.astype(BF16)
    w1 = mlp_w1.astype(BF16)
    w2 = mlp_w2.astype(BF16)
    w_heads = hgrn_w_in[0]
    gm = norm_mix.reshape(-1, 1, D)
    gf = norm_mlp.reshape(-1, 1, D)
    g_fin = norm_final.reshape(1, D)
    g_hn = hgrn_norm[0].reshape(1, DK)

    ctx_row = lambda r: 0
    smp_row = lambda r: 1 + r // ls

    mixed = _fnet_ctx(x_prompt, mod, 0, gm[0], cs_ch, t_ctx)
    x2, hn = _post_mixer(x_prompt.reshape(bp * lp, D), None, mixed.reshape(bp * lp, D), mod, 0,
                         ctx_row, wo0, w1, w2, gf[0], gm[1], final=False)
    o, new_state = _hgrn_mixer(hn.reshape(bp, lp, D), w_heads, hgrn_lb, g_hn, None)
    y_prompt = _post_mixer(x2, None, o.reshape(bp * lp, D), mod, 1, ctx_row, wo1, w1, w2,
                           gf[1], g_fin, final=True).reshape(bp, lp, D)

    mixed = _fnet_sample(x_sample, pos, mod, 0, gm[0], cs_ch, g_seq)
    x2, hn = _post_mixer(x_sample.reshape(bs * ls, D), pos, mixed.reshape(bs * ls, D), mod, 0,
                         smp_row, wo0, w1, w2, gf[0], gm[1], final=False)
    o, _ = _hgrn_mixer(hn.reshape(bs, ls, D), w_heads, hgrn_lb, g_hn, state_hgrn)
    y_sample = _post_mixer(x2, None, o.reshape(bs * ls, D), mod, 1, smp_row, wo1, w1, w2,
                           gf[1], g_fin, final=True).reshape(bs, ls, D)
    return (y_prompt, y_sample, new_state)
```

```python
import functools
import math

import numpy as np
import jax
import jax.numpy as jnp
from jax import lax
from jax.experimental import pallas as pl
from jax.experimental.pallas import tpu as pltpu

F32 = jnp.float32
BF16 = jnp.bfloat16

D = 1024
N_MOD = 6
D_FF = 4 * D
EPS = 1e-6
GROUPS = 4
GD = D // GROUPS
HEADS = 8
DK = 128
GRID_W = 64
POS_BASE = 10000.0
PROJ = 5

SEQ_SLABS = 16
SEQ_PTS = 256
LANES = 128
BF16_ROWS = 16
CHUNK = 64
SCAN_TILE = 256
EXP_SAFE = 85.0
POST_TM = 512
POST_SUB = 256
POST_TF = 1024

VMEM_LIMIT = 56 * 1024 * 1024


def _dft_tables():
    n = np.arange(GD)
    ang = 2.0 * np.pi * ((n[:, None] * n[None, :]) % GD) / GD
    cs_ch = np.concatenate([np.cos(ang), np.sin(ang)], axis=1) / math.sqrt(GD)
    t_ctx = np.concatenate([np.cos(ang), -np.sin(ang)], axis=1) / math.sqrt(GD)
    length = SEQ_PTS * SEQ_SLABS
    k1 = np.arange(SEQ_PTS)[:, None]
    n1 = np.arange(SEQ_PTS)[None, :]
    blocks = []
    for n2 in range(SEQ_SLABS):
        a = 2.0 * np.pi * ((k1 * (SEQ_SLABS * n1 + n2)) % length) / length
        gc, gs = np.cos(a) / math.sqrt(length), np.sin(a) / math.sqrt(length)
        blocks.append(np.block([[gc, -gs], [gs, gc]]))
    g_seq = np.stack(blocks)
    return tuple(jnp.asarray(t, F32).astype(BF16) for t in (cs_ch, t_ctx, g_seq))


def _grid_pos_embed(n_tok, d):
    rows = n_tok // GRID_W
    quarter = d // 4
    omega = 1.0 / (POS_BASE ** (jnp.arange(quarter, dtype=F32) / quarter))
    r = jnp.arange(rows, dtype=F32)[:, None] * omega[None, :]
    cl = jnp.arange(GRID_W, dtype=F32)[:, None] * omega[None, :]
    er = jnp.concatenate([jnp.sin(r), jnp.cos(r)], axis=-1)
    ec = jnp.concatenate([jnp.sin(cl), jnp.cos(cl)], axis=-1)
    emb = jnp.concatenate([jnp.broadcast_to(er[:, None, :], (rows, GRID_W, d // 2)),
                           jnp.broadcast_to(ec[None, :, :], (rows, GRID_W, d // 2))], axis=-1)
    return emb.reshape(n_tok, d)


def _silu(x):
    return x * jax.nn.sigmoid(x)


def _rms(x, g):
    return x * lax.rsqrt(jnp.mean(x * x, axis=-1, keepdims=True) + EPS) * g


def _mod_part(m, idx):
    return m[:, idx * D:(idx + 1) * D]


def _dot(a, b):
    return jnp.dot(a, b, preferred_element_type=F32)


def _dot_nt(a, b):
    return lax.dot_general(a, b, (((1,), (1,)), ((), ())), preferred_element_type=F32)


def _dot_tn(a, b):
    return lax.dot_general(a, b, (((0,), (0,)), ((), ())), preferred_element_type=F32)


def _params(*sem, **kw):
    return pltpu.CompilerParams(dimension_semantics=sem, vmem_limit_bytes=VMEM_LIMIT, **kw)


def _mod_kernel(c_ref, w_ref, b_ref, o_ref):
    s = _silu(c_ref[...])
    o_ref[0] = _dot(s.astype(BF16), w_ref[0].astype(BF16)) + b_ref[0]


def _modulation(cond8, ada_w, ada_b):
    depth = ada_w.shape[0]
    tn = 1536
    out = pl.pallas_call(
        _mod_kernel,
        grid=(depth, N_MOD * D // tn),
        in_specs=[pl.BlockSpec((8, D), lambda l, j: (0, 0)),
                  pl.BlockSpec((1, D, tn), lambda l, j: (l, 0, j)),
                  pl.BlockSpec((1, 1, tn), lambda l, j: (l, 0, j))],
        out_specs=pl.BlockSpec((1, 8, tn), lambda l, j: (l, 0, j)),
        out_shape=jax.ShapeDtypeStruct((depth, 8, N_MOD * D), F32),
        compiler_params=_params("parallel", "parallel"),
        name="adaln_mod",
    )(cond8, ada_w, ada_b.reshape(depth, 1, N_MOD * D))
    return out.reshape(depth, 8, 1, N_MOD * D)


def _channel_dft(hb, cs_ref, put):
    for g in range(GROUPS):
        y = _dot(hb[:, g * GD:(g + 1) * GD], cs_ref[...])
        put(0, g, y[:, :GD].astype(BF16))
        put(1, g, y[:, GD:].astype(BF16))


def _fnet_ctx_kernel(x_ref, mod_ref, g_ref, cs_ref, t_ref, o_ref, y_scr):
    m = mod_ref[0, 0]
    h = _rms(x_ref[0], g_ref[...]) * (1.0 + _mod_part(m, 1)) + _mod_part(m, 0)
    seq = x_ref.shape[1]

    def put(kind, g, val):
        y_scr[kind * seq:(kind + 1) * seq, g * GD:(g + 1) * GD] = val

    _channel_dft(h.astype(BF16), cs_ref, put)
    o_ref[0] = _dot(t_ref[...], y_scr[...]).astype(o_ref.dtype)


def _fnet_ctx(x, mod, layer, gain, cs_ch, t_ctx):
    b, seq, _ = x.shape
    return pl.pallas_call(
        _fnet_ctx_kernel,
        grid=(b,),
        in_specs=[pl.BlockSpec((1, seq, D), lambda i: (i, 0, 0)),
                  pl.BlockSpec((1, 1, 1, N_MOD * D), lambda i: (layer, 0, 0, 0)),
                  pl.BlockSpec((1, D), lambda i: (0, 0)),
                  pl.BlockSpec((GD, 2 * GD), lambda i: (0, 0)),
                  pl.BlockSpec((seq, 2 * seq), lambda i: (0, 0))],
        out_specs=pl.BlockSpec((1, seq, D), lambda i: (i, 0, 0)),
        out_shape=jax.ShapeDtypeStruct((b, seq, D), BF16),
        scratch_shapes=[pltpu.VMEM((2 * seq, D), BF16)],
        compiler_params=_params("parallel"),
        name="fnet_ctx",
    )(x, mod, gain, cs_ch, t_ctx)


def _fnet_slab_kernel(x_ref, pos_ref, mod_ref, g_ref, cs_ref, y_ref, col_scr, xp_scr):
    tm = x_ref.shape[1]
    rows = tm // SEQ_SLABS
    n_col = D // LANES
    for j in range(n_col):
        cols = slice(j * LANES, (j + 1) * LANES)
        col_scr[j] = x_ref[0, :, cols] + pos_ref[:, cols]
    for n2 in range(SEQ_SLABS):
        for j in range(n_col):
            xp_scr[n2 * rows:(n2 + 1) * rows, j * LANES:(j + 1) * LANES] = (
                col_scr[j, pl.ds(n2, rows, stride=SEQ_SLABS), :])
    m = mod_ref[0, 0]
    h = _rms(xp_scr[...], g_ref[...]) * (1.0 + _mod_part(m, 1)) + _mod_part(m, 0)

    def put(kind, g, val):
        for n2 in range(SEQ_SLABS):
            y_ref[0, n2, kind, :, g * GD:(g + 1) * GD] = val[n2 * rows:(n2 + 1) * rows, :]

    _channel_dft(h.astype(BF16), cs_ref, put)


def _fft_real_part(xs):
    n = len(xs)
    if n == 1:
        return xs
    ev = _fft_real_part(xs[0::2])
    od = _fft_real_part(xs[1::2])
    out = [None] * n
    for k in range(n // 2):
        c = math.cos(2.0 * math.pi * k / n)
        s = -math.sin(2.0 * math.pi * k / n)
        orr, oi = od[k]
        if k == 0:
            tr, ti = orr, oi
        elif 4 * k == n:
            tr, ti = oi, -orr
        else:
            tr, ti = orr * c - oi * s, orr * s + oi * c
        er, ei = ev[k]
        out[k] = (er + tr, ei + ti)
        out[k + n // 2] = (er - tr, ei - ti)
    return out


def _fnet_seq_kernel(y_ref, g_ref, o_ref, z_scr):
    tc = o_ref.shape[2]
    for n2 in range(SEQ_SLABS):
        yb = y_ref[0, n2].reshape(2 * SEQ_PTS, tc)
        z_scr[n2] = _dot(g_ref[n2], yb)

    rb = 8

    def body(r, carry):
        r0 = pl.multiple_of(r * rb, rb)
        for c in range(tc // LANES):
            cols = slice(c * LANES, (c + 1) * LANES)
            xs = []
            for n2 in range(SEQ_SLABS):
                re = z_scr[n2, pl.ds(r0, rb), cols]
                nim = z_scr[n2, pl.ds(SEQ_PTS + r0, rb), cols]
                xs.append((re, -nim))
            out = _fft_real_part(xs)
            for k2 in range(SEQ_SLABS):
                o_ref[0, pl.ds(k2 * SEQ_PTS + r0, rb), cols] = out[k2][0]
        return carry

    lax.fori_loop(0, SEQ_PTS // rb, body, 0)


def _fnet_sample(x, pos, mod, layer, gain, cs_ch, g_seq):
    b, seq, _ = x.shape
    tm = 512
    tc = 256
    y = pl.pallas_call(
        _fnet_slab_kernel,
        grid=(seq // tm, b),
        in_specs=[pl.BlockSpec((1, tm, D), lambda j, i: (i, j, 0)),
                  pl.BlockSpec((tm, D), lambda j, i: (j, 0)),
                  pl.BlockSpec((1, 1, 1, N_MOD * D), lambda j, i: (layer, 1 + i, 0, 0)),
                  pl.BlockSpec((1, D), lambda j, i: (0, 0)),
                  pl.BlockSpec((GD, 2 * GD), lambda j, i: (0, 0))],
        out_specs=pl.BlockSpec((1, SEQ_SLABS, 2, tm // SEQ_SLABS, D), lambda j, i: (i, 0, 0, j, 0)),
        out_shape=jax.ShapeDtypeStruct((b, SEQ_SLABS, 2, SEQ_PTS, D), BF16),
        scratch_shapes=[pltpu.VMEM((D // LANES, tm, LANES), F32), pltpu.VMEM((tm, D), F32)],
        compiler_params=_params("parallel", "parallel"),
        name="fnet_slab",
    )(x, pos, mod, gain, cs_ch)
    return pl.pallas_call(
        _fnet_seq_kernel,
        grid=(b, D // tc),
        in_specs=[pl.BlockSpec((1, SEQ_SLABS, 2, SEQ_PTS, tc), lambda i, j: (i, 0, 0, 0, j)),
                  pl.BlockSpec((SEQ_SLABS, 2 * SEQ_PTS, 2 * SEQ_PTS), lambda i, j: (0, 0, 0))],
        out_specs=pl.BlockSpec((1, seq, tc), lambda i, j: (i, 0, j)),
        out_shape=jax.ShapeDtypeStruct((b, seq, D), F32),
        scratch_shapes=[pltpu.VMEM((SEQ_SLABS, 2 * SEQ_PTS, tc), F32)],
        compiler_params=_params("parallel", "parallel"),
        name="fnet_seq",
    )(y, g_seq)


def _post_kernel(*refs, has_pos, final):
    it = iter(refs)
    x_ref = next(it)
    pos_ref = next(it) if has_pos else None
    a_ref, mod_ref, wo_ref, w1_ref, w2_ref, gmlp_ref, gnext_ref = (next(it) for _ in range(7))
    modn_ref = None if final else next(it)
    o_ref = next(it)
    hn_ref = None if final else next(it)
    x1_scr, h_scr, u_scr = (next(it) for _ in range(3))

    m = mod_ref[0, 0]
    subs = [slice(s * POST_SUB, (s + 1) * POST_SUB) for s in range(x_ref.shape[0] // POST_SUB)]

    for r in subs:
        x = x_ref[r, :]
        if has_pos:
            x = x + pos_ref[r, :]
        x1 = x + _mod_part(m, 2) * _dot(a_ref[r, :].astype(BF16), wo_ref[0])
        x1_scr[r, :] = x1
        h = _rms(x1, gmlp_ref[...]) * (1.0 + _mod_part(m, 4)) + _mod_part(m, 3)
        h_scr[r, :] = h.astype(BF16)

    for s, r in enumerate(subs):
        for j in range(D_FF // POST_TF):
            cols = slice(j * POST_TF, (j + 1) * POST_TF)
            u = jnp.maximum(_dot(h_scr[r, :], w1_ref[0, :, cols]), 0.0)
            u_scr[s, :, cols] = (u * u).astype(BF16)
        x2 = x1_scr[r, :] + _mod_part(m, 5) * _dot(u_scr[s], w2_ref[0])
        if final:
            o_ref[r, :] = _rms(x2, gnext_ref[...])
        else:
            o_ref[r, :] = x2
            mn = modn_ref[0, 0]
            hn = _rms(x2, gnext_ref[...]) * (1.0 + _mod_part(mn, 1)) + _mod_part(mn, 0)
            hn_ref[r, :] = hn.astype(BF16)


def _post_mixer(x, pos, a, mod, layer, mod_row, wo, w1, w2, g_mlp, g_next, final):
    rows = x.shape[0]
    tm = POST_TM
    has_pos = pos is not None
    n_pos = pos.shape[0] // tm if has_pos else 1
    row_map = lambda i: (i, 0)
    const = lambda i: (0, 0)
    resident = lambda shape: pl.BlockSpec(shape, lambda i: (layer, 0, 0), pipeline_mode=pl.Buffered(1))
    in_specs = [pl.BlockSpec((tm, D), row_map)]
    args = [x]
    if has_pos:
        in_specs.append(pl.BlockSpec((tm, D), lambda i: (i % n_pos, 0)))
        args.append(pos)
    in_specs += [pl.BlockSpec((tm, D), row_map),
                 pl.BlockSpec((1, 1, 1, N_MOD * D), lambda i: (layer, mod_row(i * tm), 0, 0)),
                 pl.BlockSpec((1, D, D), lambda i: (0, 0, 0), pipeline_mode=pl.Buffered(1)),
                 resident((1, D, D_FF)),
                 resident((1, D_FF, D)),
                 pl.BlockSpec((1, D), const),
                 pl.BlockSpec((1, D), const)]
    args += [a, mod, wo, w1, w2, g_mlp, g_next]
    out_specs = [pl.BlockSpec((tm, D), row_map)]
    out_shape = [jax.ShapeDtypeStruct((rows, D), F32)]
    if not final:
        in_specs.append(pl.BlockSpec((1, 1, 1, N_MOD * D),
                                     lambda i: (layer + 1, mod_row(i * tm), 0, 0)))
        args.append(mod)
        out_specs.append(pl.BlockSpec((tm, D), row_map))
        out_shape.append(jax.ShapeDtypeStruct((rows, D), BF16))
    res = pl.pallas_call(
        functools.partial(_post_kernel, has_pos=has_pos, final=final),
        grid=(rows // tm,),
        in_specs=in_specs,
        out_specs=out_specs,
        out_shape=out_shape,
        scratch_shapes=[pltpu.VMEM((tm, D), F32), pltpu.VMEM((tm, D), BF16),
                        pltpu.VMEM((tm // POST_SUB, POST_SUB, D_FF), BF16)],
        compiler_params=_params("parallel"),
        name="post_mlp_final" if final else "post_mlp",
    )(*args)
    return res[0] if final else (res[0], res[1])


PER_TILE = SCAN_TILE // CHUNK


def _decay_matrices():
    t = np.arange(SCAN_TILE)
    start = (t // CHUNK) * CHUNK
    first = np.arange(PER_TILE) * CHUNK
    same = (t[:, None] // CHUNK) == (t[None, :] // CHUNK)
    out = []
    for incl, mid, last in ((t[None, :] <= t[:, None], CHUNK // 2 - 1, CHUNK - 1),
                            (t[None, :] >= t[:, None], CHUNK // 2, 0)):
        bd = (same & incl).astype(np.float32)
        ref = bd[first + mid]
        pad = np.zeros((BF16_ROWS - 2 * PER_TILE, SCAN_TILE), np.float32)
        out.append(np.concatenate([bd - bd[start + mid], ref, bd[first + last] - ref, pad], axis=0))
    return jnp.asarray(np.stack(out), BF16)


def _hgrn_kernel(*refs, has_s0):
    it = iter(refs)
    hn_ref = next(it)
    wp_refs = [next(it) for _ in range(PROJ)]
    lb_ref, gn_ref, dm_ref = (next(it) for _ in range(3))
    s0_ref = next(it) if has_s0 else None
    o_ref, sf_ref = next(it), next(it)
    qd_scr, kd_scr, qb_scr, kl_scr, g_scr, o_scr, k_scr = ((next(it), next(it)) for _ in range(7))
    ext_scr, q_scr, v_scr, sg_scr, st_scr, u_scr, sb_scr, gs_scr, w_scr, flag_scr = (
        next(it) for _ in range(10))

    @pl.when(pl.program_id(1) == 0)
    def _():
        for j, wp_ref in enumerate(wp_refs):
            w_scr[:, j * DK:(j + 1) * DK] = wp_ref[...].astype(BF16)

    seq = hn_ref.shape[1]
    tp = SCAN_TILE
    n_tiles = seq // tp
    per_tile = PER_TILE

    lraw = lb_ref[...]
    mx = jnp.max(lraw, axis=1, keepdims=True)
    ex = jnp.exp(lraw - mx)
    sm = ex / jnp.sum(ex, axis=1, keepdims=True)
    lbv = (sm[:, 0, :] + sm[:, 1, :]) - sm[:, 0, :]

    def gates(i):
        rows = pl.ds(pl.multiple_of(i * tp, tp), tp)
        p = _dot(hn_ref[0, rows, :], w_scr[...])
        q_scr[rows, :] = _silu(p[:, 0:DK])
        v_scr[rows, :] = p[:, 3 * DK:4 * DK].astype(BF16)
        sg_scr[rows, :] = _silu(p[:, 4 * DK:5 * DK])
        for d in range(2):
            lb = lbv[d:d + 1, :]
            f = lb + (1.0 - lb) * jax.nn.sigmoid(p[:, (1 + d) * DK:(2 + d) * DK])
            k_scr[d][rows, :] = 1.0 - f
            g = jnp.log(f)
            g_scr[d][rows, :] = g
            g1 = g.astype(BF16)
            gs_scr[d, :, 0:DK] = g1
            gs_scr[d, :, DK:2 * DK] = (g - g1.astype(F32)).astype(BF16)

    def decay_sums():
        return [_dot(dm_ref[d], gs_scr[d]) for d in range(2)]

    def factors(i, sums):
        rows = pl.ds(pl.multiple_of(i * tp, tp), tp)
        q = q_scr[rows, :]
        unsafe = jnp.int32(0)
        for d in range(2):
            r = sums[d][:, :DK] + sums[d][:, DK:]
            a = r[:tp]
            ext = r[tp:]
            ext_scr[d, i] = ext
            unsafe = unsafe | (jnp.max(jnp.abs(a)) > EXP_SAFE).astype(jnp.int32)
            qd = q * jnp.exp(a)
            kd = k_scr[d][rows, :] * jnp.exp(-a)
            qd_scr[d][rows, :] = qd.astype(BF16)
            kd_scr[d][rows, :] = kd.astype(BF16)
            e_ref = jnp.exp(ext[0:per_tile])
            e_lr = jnp.exp(ext[per_tile:2 * per_tile])
            for c in range(per_tile):
                cr = pl.ds(pl.multiple_of(i * tp + c * CHUNK, CHUNK), CHUNK)
                cs = slice(c * CHUNK, (c + 1) * CHUNK)
                qb_scr[d][cr, :] = (qd[cs] * e_ref[c:c + 1]).astype(BF16)
                kl_scr[d][cr, :] = (kd[cs] * e_lr[c:c + 1]).astype(BF16)
        flag_scr[i] = unsafe

    def init_states():
        for d in range(2):
            if has_s0:
                st_scr[d] = s0_ref[0, 0, d, 0].T
            else:
                st_scr[d] = jnp.zeros((DK, DK), F32)

    init_states()

    ti = lax.broadcasted_iota(jnp.int32, (tp, tp), 0)
    si = lax.broadcasted_iota(jnp.int32, (tp, tp), 1)
    same = (ti // CHUNK) == (si // CHUNK)
    masks = (same & (ti >= si), same & (si >= ti))

    def scan_head(d, tile):
        order = range(per_tile) if d == 0 else range(per_tile - 1, -1, -1)
        r0 = pl.multiple_of(tile * tp, tp)
        vv = v_scr[pl.ds(r0, tp), :]
        for c in range(per_tile):
            cr = pl.ds(pl.multiple_of(r0 + c * CHUNK, CHUNK), CHUNK)
            u_scr[d, c] = _dot_tn(vv[c * CHUNK:(c + 1) * CHUNK, :], kl_scr[d][cr, :])
        ext = ext_scr[d, tile]
        decay = jnp.exp(ext[0:per_tile] + ext[per_tile:2 * per_tile])
        st = st_scr[d]
        for c in order:
            sb_scr[d, c] = st.astype(BF16)
            st = st * decay[c:c + 1] + u_scr[d, c]
        st_scr[d] = st

    def scan_tail(d, tile):
        r0 = pl.multiple_of(tile * tp, tp)
        rows = pl.ds(r0, tp)
        sc = jnp.where(masks[d], _dot_nt(qd_scr[d][rows, :], kd_scr[d][rows, :]), 0.0)
        o_scr[d][rows, :] = _dot(sc.astype(BF16), v_scr[rows, :])
        for c in range(per_tile):
            cr = pl.ds(pl.multiple_of(r0 + c * CHUNK, CHUNK), CHUNK)
            o_scr[d][cr, :] += _dot_nt(qb_scr[d][cr, :], sb_scr[d, c])

    rid = lax.broadcasted_iota(jnp.int32, (BF16_ROWS, DK), 0)

    def scan_exact(d, tile):
        r0 = tile * tp

        def step(j, st):
            t = r0 + ((tp - 1 - j) if d == 1 else j)
            t0 = pl.multiple_of((t // BF16_ROWS) * BF16_ROWS, BF16_ROWS)
            grp = pl.ds(t0, BF16_ROWS)
            sel = rid == (t - t0)
            g = g_scr[d][grp, :]
            f = jnp.exp(jnp.sum(jnp.where(sel, g, 0.0), axis=0, keepdims=True))
            k = jnp.where(sel, 1.0 - jnp.exp(g), 0.0)
            st = st * f + _dot_tn(v_scr[grp, :], k.astype(BF16))
            o = _dot_nt(q_scr[grp, :].astype(BF16), st.astype(BF16))
            o_scr[d][grp, :] = jnp.where(sel, o, o_scr[d][grp, :])
            return st

        st_scr[d] = lax.fori_loop(0, tp, step, st_scr[d])

    def finish(i):
        rows = pl.ds(pl.multiple_of(i * tp, tp), tp)
        o = o_scr[0][rows, :] + o_scr[1][rows, :]
        o = o * lax.rsqrt(jnp.mean(o * o, axis=-1, keepdims=True) + EPS) * gn_ref[...]
        o_ref[0, rows, :] = (o * sg_scr[rows, :]).astype(o_ref.dtype)

    last = n_tiles - 1
    gates(0)
    if n_tiles > 1:
        sums = decay_sums()
        gates(1)
        factors(0, sums)

        def forward(i, carry):
            scan_head(0, i - 1)
            sums = decay_sums()
            gates(i + 1)
            factors(i, sums)
            scan_tail(0, i - 1)
            return carry

        lax.fori_loop(1, last, forward, 0, unroll=7)
        scan_head(0, last - 1)
        factors(last, decay_sums())
        scan_tail(0, last - 1)
    else:
        factors(0, decay_sums())
    scan_head(0, last)
    scan_head(1, last)
    scan_tail(0, last)
    scan_tail(1, last)

    def backward(j, carry):
        t = last - j
        scan_head(1, t)
        finish(t + 1)
        scan_tail(1, t)
        return carry

    lax.fori_loop(1, n_tiles, backward, 0, unroll=5)
    finish(0)

    unsafe = flag_scr[0]
    for t in range(1, n_tiles):
        unsafe = unsafe | flag_scr[t]

    @pl.when(unsafe != 0)
    def _():
        init_states()

        def resweep(d):
            def body(j, carry):
                t = j if d == 0 else last - j

                @pl.when(flag_scr[t] != 0)
                def _():
                    scan_exact(d, t)

                @pl.when(flag_scr[t] == 0)
                def _():
                    scan_head(d, t)
                    scan_tail(d, t)

                if d == 1:
                    finish(t)
                return carry

            lax.fori_loop(0, n_tiles, body, 0)

        resweep(0)
        resweep(1)

    for d in range(2):
        sf_ref[0, 0, d, 0] = st_scr[d].T


def _hgrn_mixer(hn, w_in, lb_raw, g_norm, s0):
    b, seq, _ = hn.shape
    has_s0 = s0 is not None
    st_spec = pl.BlockSpec((1, 1, 2, 1, DK, DK), lambda h, i: (i, 0, 0, h, 0, 0))
    in_specs = [pl.BlockSpec((1, seq, D), lambda h, i: (i, 0, 0))]
    in_specs += [pl.BlockSpec((D, DK), functools.partial(lambda j, h, i: (0, j * HEADS + h), j))
                 for j in range(PROJ)]
    in_specs += [pl.BlockSpec((2, 2, DK), lambda h, i: (0, 0, h)),
                 pl.BlockSpec((1, DK), lambda h, i: (0, 0)),
                 pl.BlockSpec((2, SCAN_TILE + BF16_ROWS, SCAN_TILE), lambda h, i: (0, 0, 0))]
    args = [hn] + [w_in] * PROJ + [lb_raw, g_norm, _decay_matrices()]
    if has_s0:
        in_specs.append(st_spec)
        args.append(s0)
    n_tiles = seq // SCAN_TILE
    bf16_rows = pltpu.VMEM((seq, DK), BF16)
    f32_rows = pltpu.VMEM((seq, DK), F32)
    return pl.pallas_call(
        functools.partial(_hgrn_kernel, has_s0=has_s0),
        grid=(HEADS, b),
        in_specs=in_specs,
        out_specs=[pl.BlockSpec((1, seq, DK), lambda h, i: (i, 0, h)), st_spec],
        out_shape=[jax.ShapeDtypeStruct((b, seq, D), BF16),
                   jax.ShapeDtypeStruct((b, 1, 2, HEADS, DK, DK), F32)],
        scratch_shapes=[bf16_rows] * 8 + [f32_rows] * 6 + [
                        pltpu.VMEM((2, n_tiles, BF16_ROWS, DK), F32),
                        f32_rows, bf16_rows, f32_rows,
                        pltpu.VMEM((2, DK, DK), F32),
                        pltpu.VMEM((2, PER_TILE, DK, DK), F32),
                        pltpu.VMEM((2, PER_TILE, DK, DK), BF16),
                        pltpu.VMEM((2, SCAN_TILE, 2 * DK), BF16),
                        pltpu.VMEM((D, PROJ * DK), BF16),
                        pltpu.SMEM((n_tiles,), jnp.int32)],
        compiler_params=_params("parallel", "arbitrary"),
        name="hgrn_scan",
    )(*args)


def kernel(x_prompt, x_sample, state_hgrn, c, c_ctx, ada_w, ada_b, norm_mix, norm_mlp, fnet_wo,
           hgrn_w_in, hgrn_lb, hgrn_norm, hgrn_wo, mlp_w1, mlp_w2, norm_final):
    bp, lp, _ = x_prompt.shape
    bs, ls, _ = x_sample.shape
    assert ada_w.shape[0] == 2 and ls == SEQ_PTS * SEQ_SLABS and lp == GD

    cs_ch, t_ctx, g_seq = _dft_tables()
    pos = _grid_pos_embed(ls, D)

    cond8 = jnp.zeros((8, D), F32).at[0].set(c_ctx).at[1:1 + bs].set(c)
    mod = _modulation(cond8, ada_w, ada_b)

    wo0 = fnet_wo.astype(BF16)
    wo1 = hgrn_wo.astype(BF16)
    w1 = mlp_w1.astype(BF16)
    w2 = mlp_w2.astype(BF16)
    w_heads = hgrn_w_in[0]
    gm = norm_mix.reshape(-1, 1, D)
    gf = norm_mlp.reshape(-1, 1, D)
    g_fin = norm_final.reshape(1, D)
    g_hn = hgrn_norm[0].reshape(1, DK)

    ctx_row = lambda r: 0
    smp_row = lambda r: 1 + r // ls

    mixed = _fnet_ctx(x_prompt, mod, 0, gm[0], cs_ch, t_ctx)
    x2, hn = _post_mixer(x_prompt.reshape(bp * lp, D), None, mixed.reshape(bp * lp, D), mod, 0,
                         ctx_row, wo0, w1, w2, gf[0], gm[1], final=False)
    o, new_state = _hgrn_mixer(hn.reshape(bp, lp, D), w_heads, hgrn_lb, g_hn, None)
    y_prompt = _post_mixer(x2, None, o.reshape(bp * lp, D), mod, 1, ctx_row, wo1, w1, w2,
                           gf[1], g_fin, final=True).reshape(bp, lp, D)

    mixed = _fnet_sample(x_sample, pos, mod, 0, gm[0], cs_ch, g_seq)
    x2, hn = _post_mixer(x_sample.reshape(bs * ls, D), pos, mixed.reshape(bs * ls, D), mod, 0,
                         smp_row, wo0, w1, w2, gf[0], gm[1], final=False)
    o, _ = _hgrn_mixer(hn.reshape(bs, ls, D), w_heads, hgrn_lb, g_hn, state_hgrn)
    y_sample = _post_mixer(x2, None, o.reshape(bs * ls, D), mod, 1, smp_row, wo1, w1, w2,
                           gf[1], g_fin, final=True).reshape(bs, ls, D)
    return (y_prompt, y_sample, new_state)
```

```python
import functools
import math

import numpy as np
import jax
import jax.numpy as jnp
from jax import lax
from jax.experimental import pallas as pl
from jax.experimental.pallas import tpu as pltpu

F32 = jnp.float32
BF16 = jnp.bfloat16

D = 1024
N_MOD = 6
D_FF = 4 * D
EPS = 1e-6
GROUPS = 4
GD = D // GROUPS
HEADS = 8
DK = 128
GRID_W = 64
POS_BASE = 10000.0
PROJ = 5

SEQ_SLABS = 16
SEQ_PTS = 256
LANES = 128
BF16_ROWS = 16
CHUNK = 64
SCAN_TILE = 256
EXP_SAFE = 85.0
SWEEP_UNROLL = 7
BACK_UNROLL = 5
POST_TM = 512
POST_SUB = 256
POST_TF = 1024

VMEM_LIMIT = 56 * 1024 * 1024


def _dft_tables():
    n = np.arange(GD)
    ang = 2.0 * np.pi * ((n[:, None] * n[None, :]) % GD) / GD
    cs_ch = np.concatenate([np.cos(ang), np.sin(ang)], axis=1) / math.sqrt(GD)
    t_ctx = np.concatenate([np.cos(ang), -np.sin(ang)], axis=1) / math.sqrt(GD)
    length = SEQ_PTS * SEQ_SLABS
    k1 = np.arange(SEQ_PTS)[:, None]
    n1 = np.arange(SEQ_PTS)[None, :]
    blocks = []
    for n2 in range(SEQ_SLABS):
        a = 2.0 * np.pi * ((k1 * (SEQ_SLABS * n1 + n2)) % length) / length
        gc, gs = np.cos(a) / math.sqrt(length), np.sin(a) / math.sqrt(length)
        blocks.append(np.block([[gc, -gs], [gs, gc]]))
    g_seq = np.stack(blocks)
    return tuple(jnp.asarray(t, F32).astype(BF16) for t in (cs_ch, t_ctx, g_seq))


def _grid_pos_embed(n_tok, d):
    rows = n_tok // GRID_W
    quarter = d // 4
    omega = 1.0 / (POS_BASE ** (jnp.arange(quarter, dtype=F32) / quarter))
    r = jnp.arange(rows, dtype=F32)[:, None] * omega[None, :]
    cl = jnp.arange(GRID_W, dtype=F32)[:, None] * omega[None, :]
    er = jnp.concatenate([jnp.sin(r), jnp.cos(r)], axis=-1)
    ec = jnp.concatenate([jnp.sin(cl), jnp.cos(cl)], axis=-1)
    emb = jnp.concatenate([jnp.broadcast_to(er[:, None, :], (rows, GRID_W, d // 2)),
                           jnp.broadcast_to(ec[None, :, :], (rows, GRID_W, d // 2))], axis=-1)
    return emb.reshape(n_tok, d)


def _silu(x):
    return x * jax.nn.sigmoid(x)


def _rms(x, g):
    return x * lax.rsqrt(jnp.mean(x * x, axis=-1, keepdims=True) + EPS) * g


def _mod_part(m, idx):
    return m[:, idx * D:(idx + 1) * D]


def _dot(a, b):
    return jnp.dot(a, b, preferred_element_type=F32)


def _dot_nt(a, b):
    return lax.dot_general(a, b, (((1,), (1,)), ((), ())), preferred_element_type=F32)


def _dot_tn(a, b):
    return lax.dot_general(a, b, (((0,), (0,)), ((), ())), preferred_element_type=F32)


def _params(*sem, **kw):
    return pltpu.CompilerParams(dimension_semantics=sem, vmem_limit_bytes=VMEM_LIMIT, **kw)


def _mod_kernel(c_ref, w_ref, b_ref, o_ref):
    s = _silu(c_ref[...])
    o_ref[0] = _dot(s.astype(BF16), w_ref[0].astype(BF16)) + b_ref[0]


def _modulation(cond8, ada_w, ada_b):
    depth = ada_w.shape[0]
    tn = 1536
    out = pl.pallas_call(
        _mod_kernel,
        grid=(depth, N_MOD * D // tn),
        in_specs=[pl.BlockSpec((8, D), lambda l, j: (0, 0)),
                  pl.BlockSpec((1, D, tn), lambda l, j: (l, 0, j)),
                  pl.BlockSpec((1, 1, tn), lambda l, j: (l, 0, j))],
        out_specs=pl.BlockSpec((1, 8, tn), lambda l, j: (l, 0, j)),
        out_shape=jax.ShapeDtypeStruct((depth, 8, N_MOD * D), F32),
        compiler_params=_params("parallel", "parallel"),
        name="adaln_mod",
    )(cond8, ada_w, ada_b.reshape(depth, 1, N_MOD * D))
    return out.reshape(depth, 8, 1, N_MOD * D)


def _channel_dft(hb, cs_ref, put):
    for g in range(GROUPS):
        y = _dot(hb[:, g * GD:(g + 1) * GD], cs_ref[...])
        put(0, g, y[:, :GD].astype(BF16))
        put(1, g, y[:, GD:].astype(BF16))


def _fnet_ctx_kernel(x_ref, mod_ref, g_ref, cs_ref, t_ref, o_ref, y_scr):
    m = mod_ref[0, 0]
    h = _rms(x_ref[0], g_ref[...]) * (1.0 + _mod_part(m, 1)) + _mod_part(m, 0)
    seq = x_ref.shape[1]

    def put(kind, g, val):
        y_scr[kind * seq:(kind + 1) * seq, g * GD:(g + 1) * GD] = val

    _channel_dft(h.astype(BF16), cs_ref, put)
    o_ref[0] = _dot(t_ref[...], y_scr[...]).astype(o_ref.dtype)


def _fnet_ctx(x, mod, layer, gain, cs_ch, t_ctx):
    b, seq, _ = x.shape
    return pl.pallas_call(
        _fnet_ctx_kernel,
        grid=(b,),
        in_specs=[pl.BlockSpec((1, seq, D), lambda i: (i, 0, 0)),
                  pl.BlockSpec((1, 1, 1, N_MOD * D), lambda i: (layer, 0, 0, 0)),
                  pl.BlockSpec((1, D), lambda i: (0, 0)),
                  pl.BlockSpec((GD, 2 * GD), lambda i: (0, 0)),
                  pl.BlockSpec((seq, 2 * seq), lambda i: (0, 0))],
        out_specs=pl.BlockSpec((1, seq, D), lambda i: (i, 0, 0)),
        out_shape=jax.ShapeDtypeStruct((b, seq, D), BF16),
        scratch_shapes=[pltpu.VMEM((2 * seq, D), BF16)],
        compiler_params=_params("parallel"),
        name="fnet_ctx",
    )(x, mod, gain, cs_ch, t_ctx)


def _fnet_slab_kernel(x_ref, pos_ref, mod_ref, g_ref, cs_ref, y_ref, col_scr, xp_scr):
    tm = x_ref.shape[1]
    rows = tm // SEQ_SLABS
    n_col = D // LANES
    for j in range(n_col):
        cols = slice(j * LANES, (j + 1) * LANES)
        col_scr[j] = x_ref[0, :, cols] + pos_ref[:, cols]
    for n2 in range(SEQ_SLABS):
        for j in range(n_col):
            xp_scr[n2 * rows:(n2 + 1) * rows, j * LANES:(j + 1) * LANES] = (
                col_scr[j, pl.ds(n2, rows, stride=SEQ_SLABS), :])
    m = mod_ref[0, 0]
    h = _rms(xp_scr[...], g_ref[...]) * (1.0 + _mod_part(m, 1)) + _mod_part(m, 0)

    def put(kind, g, val):
        for n2 in range(SEQ_SLABS):
            y_ref[0, n2, kind, :, g * GD:(g + 1) * GD] = val[n2 * rows:(n2 + 1) * rows, :]

    _channel_dft(h.astype(BF16), cs_ref, put)


def _fft_real_part(xs):
    n = len(xs)
    if n == 1:
        return xs
    ev = _fft_real_part(xs[0::2])
    od = _fft_real_part(xs[1::2])
    out = [None] * n
    for k in range(n // 2):
        c = math.cos(2.0 * math.pi * k / n)
        s = -math.sin(2.0 * math.pi * k / n)
        orr, oi = od[k]
        if k == 0:
            tr, ti = orr, oi
        elif 4 * k == n:
            tr, ti = oi, -orr
        else:
            tr, ti = orr * c - oi * s, orr * s + oi * c
        er, ei = ev[k]
        out[k] = (er + tr, ei + ti)
        out[k + n // 2] = (er - tr, ei - ti)
    return out


def _fnet_seq_kernel(y_ref, g_ref, o_ref, z_scr):
    tc = o_ref.shape[2]
    for n2 in range(SEQ_SLABS):
        yb = y_ref[0, n2].reshape(2 * SEQ_PTS, tc)
        z_scr[n2] = _dot(g_ref[n2], yb)

    rb = 8

    def body(r, carry):
        r0 = pl.multiple_of(r * rb, rb)
        for c in range(tc // LANES):
            cols = slice(c * LANES, (c + 1) * LANES)
            xs = []
            for n2 in range(SEQ_SLABS):
                re = z_scr[n2, pl.ds(r0, rb), cols]
                nim = z_scr[n2, pl.ds(SEQ_PTS + r0, rb), cols]
                xs.append((re, -nim))
            out = _fft_real_part(xs)
            for k2 in range(SEQ_SLABS):
                o_ref[0, pl.ds(k2 * SEQ_PTS + r0, rb), cols] = out[k2][0]
        return carry

    lax.fori_loop(0, SEQ_PTS // rb, body, 0)


def _fnet_sample(x, pos, mod, layer, gain, cs_ch, g_seq):
    b, seq, _ = x.shape
    tm = 512
    tc = 256
    y = pl.pallas_call(
        _fnet_slab_kernel,
        grid=(seq // tm, b),
        in_specs=[pl.BlockSpec((1, tm, D), lambda j, i: (i, j, 0)),
                  pl.BlockSpec((tm, D), lambda j, i: (j, 0)),
                  pl.BlockSpec((1, 1, 1, N_MOD * D), lambda j, i: (layer, 1 + i, 0, 0)),
                  pl.BlockSpec((1, D), lambda j, i: (0, 0)),
                  pl.BlockSpec((GD, 2 * GD), lambda j, i: (0, 0))],
        out_specs=pl.BlockSpec((1, SEQ_SLABS, 2, tm // SEQ_SLABS, D), lambda j, i: (i, 0, 0, j, 0)),
        out_shape=jax.ShapeDtypeStruct((b, SEQ_SLABS, 2, SEQ_PTS, D), BF16),
        scratch_shapes=[pltpu.VMEM((D // LANES, tm, LANES), F32), pltpu.VMEM((tm, D), F32)],
        compiler_params=_params("parallel", "parallel"),
        name="fnet_slab",
    )(x, pos, mod, gain, cs_ch)
    return pl.pallas_call(
        _fnet_seq_kernel,
        grid=(b, D // tc),
        in_specs=[pl.BlockSpec((1, SEQ_SLABS, 2, SEQ_PTS, tc), lambda i, j: (i, 0, 0, 0, j)),
                  pl.BlockSpec((SEQ_SLABS, 2 * SEQ_PTS, 2 * SEQ_PTS), lambda i, j: (0, 0, 0))],
        out_specs=pl.BlockSpec((1, seq, tc), lambda i, j: (i, 0, j)),
        out_shape=jax.ShapeDtypeStruct((b, seq, D), F32),
        scratch_shapes=[pltpu.VMEM((SEQ_SLABS, 2 * SEQ_PTS, tc), F32)],
        compiler_params=_params("parallel", "parallel"),
        name="fnet_seq",
    )(y, g_seq)


def _post_kernel(*refs, has_pos, final):
    it = iter(refs)
    x_ref = next(it)
    pos_ref = next(it) if has_pos else None
    a_ref, mod_ref, wo_ref, w1_ref, w2_ref, gmlp_ref, gnext_ref = (next(it) for _ in range(7))
    modn_ref = None if final else next(it)
    o_ref = next(it)
    hn_ref = None if final else next(it)
    x1_scr, h_scr, u_scr = (next(it) for _ in range(3))

    m = mod_ref[0, 0]
    subs = [slice(s * POST_SUB, (s + 1) * POST_SUB) for s in range(x_ref.shape[0] // POST_SUB)]

    for r in subs:
        x = x_ref[r, :]
        if has_pos:
            x = x + pos_ref[r, :]
        x1 = x + _mod_part(m, 2) * _dot(a_ref[r, :].astype(BF16), wo_ref[0])
        x1_scr[r, :] = x1
        h = _rms(x1, gmlp_ref[...]) * (1.0 + _mod_part(m, 4)) + _mod_part(m, 3)
        h_scr[r, :] = h.astype(BF16)

    for s, r in enumerate(subs):
        for j in range(D_FF // POST_TF):
            cols = slice(j * POST_TF, (j + 1) * POST_TF)
            u = jnp.maximum(_dot(h_scr[r, :], w1_ref[0, :, cols]), 0.0)
            u_scr[s, :, cols] = (u * u).astype(BF16)
        x2 = x1_scr[r, :] + _mod_part(m, 5) * _dot(u_scr[s], w2_ref[0])
        if final:
            o_ref[r, :] = _rms(x2, gnext_ref[...])
        else:
            o_ref[r, :] = x2
            mn = modn_ref[0, 0]
            hn = _rms(x2, gnext_ref[...]) * (1.0 + _mod_part(mn, 1)) + _mod_part(mn, 0)
            hn_ref[r, :] = hn.astype(BF16)


def _post_mixer(x, pos, a, mod, layer, mod_row, wo, w1, w2, g_mlp, g_next, final):
    rows = x.shape[0]
    tm = POST_TM
    has_pos = pos is not None
    n_pos = pos.shape[0] // tm if has_pos else 1
    row_map = lambda i: (i, 0)
    const = lambda i: (0, 0)
    resident = lambda shape: pl.BlockSpec(shape, lambda i: (layer, 0, 0), pipeline_mode=pl.Buffered(1))
    in_specs = [pl.BlockSpec((tm, D), row_map)]
    args = [x]
    if has_pos:
        in_specs.append(pl.BlockSpec((tm, D), lambda i: (i % n_pos, 0)))
        args.append(pos)
    in_specs += [pl.BlockSpec((tm, D), row_map),
                 pl.BlockSpec((1, 1, 1, N_MOD * D), lambda i: (layer, mod_row(i * tm), 0, 0)),
                 pl.BlockSpec((1, D, D), lambda i: (0, 0, 0), pipeline_mode=pl.Buffered(1)),
                 resident((1, D, D_FF)),
                 resident((1, D_FF, D)),
                 pl.BlockSpec((1, D), const),
                 pl.BlockSpec((1, D), const)]
    args += [a, mod, wo, w1, w2, g_mlp, g_next]
    out_specs = [pl.BlockSpec((tm, D), row_map)]
    out_shape = [jax.ShapeDtypeStruct((rows, D), F32)]
    if not final:
        in_specs.append(pl.BlockSpec((1, 1, 1, N_MOD * D),
                                     lambda i: (layer + 1, mod_row(i * tm), 0, 0)))
        args.append(mod)
        out_specs.append(pl.BlockSpec((tm, D), row_map))
        out_shape.append(jax.ShapeDtypeStruct((rows, D), BF16))
    res = pl.pallas_call(
        functools.partial(_post_kernel, has_pos=has_pos, final=final),
        grid=(rows // tm,),
        in_specs=in_specs,
        out_specs=out_specs,
        out_shape=out_shape,
        scratch_shapes=[pltpu.VMEM((tm, D), F32), pltpu.VMEM((tm, D), BF16),
                        pltpu.VMEM((tm // POST_SUB, POST_SUB, D_FF), BF16)],
        compiler_params=_params("parallel"),
        name="post_mlp_final" if final else "post_mlp",
    )(*args)
    return res[0] if final else (res[0], res[1])


PER_TILE = SCAN_TILE // CHUNK


def _decay_matrices():
    t = np.arange(SCAN_TILE)
    start = (t // CHUNK) * CHUNK
    first = np.arange(PER_TILE) * CHUNK
    same = (t[:, None] // CHUNK) == (t[None, :] // CHUNK)
    out = []
    for incl, mid, last in ((t[None, :] <= t[:, None], CHUNK // 2 - 1, CHUNK - 1),
                            (t[None, :] >= t[:, None], CHUNK // 2, 0)):
        bd = (same & incl).astype(np.float32)
        ref = bd[first + mid]
        pad = np.zeros((BF16_ROWS - 2 * PER_TILE, SCAN_TILE), np.float32)
        out.append(np.concatenate([bd - bd[start + mid], ref, bd[first + last] - ref, pad], axis=0))
    return jnp.asarray(np.stack(out), BF16)


def _hgrn_kernel(*refs, has_s0, independent):
    it = iter(refs)
    hn_ref = next(it)
    wp_refs = [next(it) for _ in range(PROJ)]
    lb_ref, gn_ref, dm_ref = (next(it) for _ in range(3))
    s0_ref = next(it) if has_s0 else None
    o_ref, sf_ref = next(it), next(it)
    qd_scr, kd_scr, qb_scr, kl_scr, g_scr, o_scr, k_scr = ((next(it), next(it)) for _ in range(7))
    ext_scr, q_scr, v_scr, sg_scr, st_scr, u_scr, sb_scr, gs_scr, w_scr, flag_scr = (
        next(it) for _ in range(10))

    @pl.when(pl.program_id(1) == 0)
    def _():
        for j, wp_ref in enumerate(wp_refs):
            w_scr[:, j * DK:(j + 1) * DK] = wp_ref[...].astype(BF16)

    seq = hn_ref.shape[1]
    tp = SCAN_TILE
    n_tiles = seq // tp
    per_tile = PER_TILE

    lraw = lb_ref[...]
    mx = jnp.max(lraw, axis=1, keepdims=True)
    ex = jnp.exp(lraw - mx)
    sm = ex / jnp.sum(ex, axis=1, keepdims=True)
    lbv = (sm[:, 0, :] + sm[:, 1, :]) - sm[:, 0, :]

    def gates(i):
        rows = pl.ds(pl.multiple_of(i * tp, tp), tp)
        p = _dot(hn_ref[0, rows, :], w_scr[...])
        q_scr[rows, :] = _silu(p[:, 0:DK])
        v_scr[rows, :] = p[:, 3 * DK:4 * DK].astype(BF16)
        sg_scr[rows, :] = _silu(p[:, 4 * DK:5 * DK])
        for d in range(2):
            lb = lbv[d:d + 1, :]
            f = lb + (1.0 - lb) * jax.nn.sigmoid(p[:, (1 + d) * DK:(2 + d) * DK])
            k_scr[d][rows, :] = 1.0 - f
            g = jnp.log(f)
            g_scr[d][rows, :] = g
            g1 = g.astype(BF16)
            gs_scr[d, :, 0:DK] = g1
            gs_scr[d, :, DK:2 * DK] = (g - g1.astype(F32)).astype(BF16)

    def decay_sums():
        return [_dot(dm_ref[d], gs_scr[d]) for d in range(2)]

    def factors(i, sums):
        rows = pl.ds(pl.multiple_of(i * tp, tp), tp)
        q = q_scr[rows, :]
        unsafe = jnp.int32(0)
        for d in range(2):
            r = sums[d][:, :DK] + sums[d][:, DK:]
            a = r[:tp]
            ext = r[tp:]
            ext_scr[d, i] = ext
            unsafe = unsafe | (jnp.max(jnp.abs(a)) > EXP_SAFE).astype(jnp.int32)
            qd = q * jnp.exp(a)
            kd = k_scr[d][rows, :] * jnp.exp(-a)
            qd_scr[d][rows, :] = qd.astype(BF16)
            kd_scr[d][rows, :] = kd.astype(BF16)
            e_ref = jnp.exp(ext[0:per_tile])
            e_lr = jnp.exp(ext[per_tile:2 * per_tile])
            for c in range(per_tile):
                cr = pl.ds(pl.multiple_of(i * tp + c * CHUNK, CHUNK), CHUNK)
                cs = slice(c * CHUNK, (c + 1) * CHUNK)
                qb_scr[d][cr, :] = (qd[cs] * e_ref[c:c + 1]).astype(BF16)
                kl_scr[d][cr, :] = (kd[cs] * e_lr[c:c + 1]).astype(BF16)
        flag_scr[i] = unsafe

    def init_states():
        for d in range(2):
            if has_s0:
                st_scr[d] = s0_ref[0, 0, d, 0].T
            else:
                st_scr[d] = jnp.zeros((DK, DK), F32)

    def state_in(d):
        return jnp.zeros((DK, DK), F32) if independent else st_scr[d]

    def state_out(d, tile, st):
        if independent:
            sf_ref[tile, 0, d, 0] = st.T
        else:
            st_scr[d] = st

    if not independent:
        init_states()

    ti = lax.broadcasted_iota(jnp.int32, (tp, tp), 0)
    si = lax.broadcasted_iota(jnp.int32, (tp, tp), 1)
    same = (ti // CHUNK) == (si // CHUNK)
    masks = (same & (ti >= si), same & (si >= ti))

    def scan_head(d, tile):
        order = range(per_tile) if d == 0 else range(per_tile - 1, -1, -1)
        r0 = pl.multiple_of(tile * tp, tp)
        vv = v_scr[pl.ds(r0, tp), :]
        for c in range(per_tile):
            cr = pl.ds(pl.multiple_of(r0 + c * CHUNK, CHUNK), CHUNK)
            u_scr[d, c] = _dot_tn(vv[c * CHUNK:(c + 1) * CHUNK, :], kl_scr[d][cr, :])
        ext = ext_scr[d, tile]
        decay = jnp.exp(ext[0:per_tile] + ext[per_tile:2 * per_tile])
        st = state_in(d)
        for c in order:
            sb_scr[d, c] = st.astype(BF16)
            st = st * decay[c:c + 1] + u_scr[d, c]
        state_out(d, tile, st)

    def scan_tail(d, tile):
        r0 = pl.multiple_of(tile * tp, tp)
        rows = pl.ds(r0, tp)
        sc = jnp.where(masks[d], _dot_nt(qd_scr[d][rows, :], kd_scr[d][rows, :]), 0.0)
        o_scr[d][rows, :] = _dot(sc.astype(BF16), v_scr[rows, :])
        for c in range(per_tile):
            cr = pl.ds(pl.multiple_of(r0 + c * CHUNK, CHUNK), CHUNK)
            o_scr[d][cr, :] += _dot_nt(qb_scr[d][cr, :], sb_scr[d, c])

    rid = lax.broadcasted_iota(jnp.int32, (BF16_ROWS, DK), 0)

    def scan_exact(d, tile):
        r0 = tile * tp

        def step(j, st):
            t = r0 + ((tp - 1 - j) if d == 1 else j)
            t0 = pl.multiple_of((t // BF16_ROWS) * BF16_ROWS, BF16_ROWS)
            grp = pl.ds(t0, BF16_ROWS)
            sel = rid == (t - t0)
            g = g_scr[d][grp, :]
            f = jnp.exp(jnp.sum(jnp.where(sel, g, 0.0), axis=0, keepdims=True))
            k = jnp.where(sel, 1.0 - jnp.exp(g), 0.0)
            st = st * f + _dot_tn(v_scr[grp, :], k.astype(BF16))
            o = _dot_nt(q_scr[grp, :].astype(BF16), st.astype(BF16))
            o_scr[d][grp, :] = jnp.where(sel, o, o_scr[d][grp, :])
            return st

        state_out(d, tile, lax.fori_loop(0, tp, step, state_in(d)))

    def finish(i):
        rows = pl.ds(pl.multiple_of(i * tp, tp), tp)
        o = o_scr[0][rows, :] + o_scr[1][rows, :]
        o = o * lax.rsqrt(jnp.mean(o * o, axis=-1, keepdims=True) + EPS) * gn_ref[...]
        o_ref[0, rows, :] = (o * sg_scr[rows, :]).astype(o_ref.dtype)

    dirs = (0, 1) if independent else (0,)

    def scan_heads(t):
        for d in dirs:
            scan_head(d, t)

    def scan_tails(t):
        for d in dirs:
            scan_tail(d, t)
        if independent:
            finish(t)

    last = n_tiles - 1
    gates(0)
    if n_tiles > 1:
        sums = decay_sums()
        gates(1)
        factors(0, sums)

        def forward(i, carry):
            scan_heads(i - 1)
            sums = decay_sums()
            gates(i + 1)
            factors(i, sums)
            scan_tails(i - 1)
            return carry

        lax.fori_loop(1, last, forward, 0, unroll=SWEEP_UNROLL)
        scan_heads(last - 1)
        factors(last, decay_sums())
        scan_tails(last - 1)
    else:
        factors(0, decay_sums())

    if independent:
        scan_heads(last)
        scan_tails(last)
    else:
        scan_head(0, last)
        scan_head(1, last)
        scan_tail(0, last)
        scan_tail(1, last)

        def backward(j, carry):
            t = last - j
            scan_head(1, t)
            finish(t + 1)
            scan_tail(1, t)
            return carry

        lax.fori_loop(1, n_tiles, backward, 0, unroll=BACK_UNROLL)
        finish(0)

    unsafe = flag_scr[0]
    for t in range(1, n_tiles):
        unsafe = unsafe | flag_scr[t]

    @pl.when(unsafe != 0)
    def _():
        if not independent:
            init_states()

        def resweep(d):
            def body(j, carry):
                t = j if d == 0 else last - j

                @pl.when(flag_scr[t] != 0)
                def _():
                    scan_exact(d, t)

                @pl.when(flag_scr[t] == 0)
                def _():
                    scan_head(d, t)
                    scan_tail(d, t)

                if d == 1:
                    finish(t)
                return carry

            lax.fori_loop(0, n_tiles, body, 0)

        resweep(0)
        resweep(1)

    if not independent:
        for d in range(2):
            sf_ref[0, 0, d, 0] = st_scr[d].T


def _hgrn_mixer(hn, w_in, lb_raw, g_norm, s0):
    n_seq, seq_len, _ = hn.shape
    has_s0 = s0 is not None
    independent = (not has_s0) and seq_len == SCAN_TILE
    if independent:
        hn = hn.reshape(1, n_seq * seq_len, D)
    b, seq, _ = hn.shape
    st_rows = n_seq if independent else 1
    st_spec = pl.BlockSpec((st_rows, 1, 2, 1, DK, DK), lambda h, i: (i, 0, 0, h, 0, 0))
    in_specs = [pl.BlockSpec((1, seq, D), lambda h, i: (i, 0, 0))]
    in_specs += [pl.BlockSpec((D, DK), functools.partial(lambda j, h, i: (0, j * HEADS + h), j))
                 for j in range(PROJ)]
    in_specs += [pl.BlockSpec((2, 2, DK), lambda h, i: (0, 0, h)),
                 pl.BlockSpec((1, DK), lambda h, i: (0, 0)),
                 pl.BlockSpec((2, SCAN_TILE + BF16_ROWS, SCAN_TILE), lambda h, i: (0, 0, 0))]
    args = [hn] + [w_in] * PROJ + [lb_raw, g_norm, _decay_matrices()]
    if has_s0:
        in_specs.append(st_spec)
        args.append(s0)
    n_tiles = seq // SCAN_TILE
    bf16_rows = pltpu.VMEM((seq, DK), BF16)
    f32_rows = pltpu.VMEM((seq, DK), F32)
    o, s_fin = pl.pallas_call(
        functools.partial(_hgrn_kernel, has_s0=has_s0, independent=independent),
        grid=(HEADS, b),
        in_specs=in_specs,
        out_specs=[pl.BlockSpec((1, seq, DK), lambda h, i: (i, 0, h)), st_spec],
        out_shape=[jax.ShapeDtypeStruct((b, seq, D), BF16),
                   jax.ShapeDtypeStruct((n_seq, 1, 2, HEADS, DK, DK), F32)],
        scratch_shapes=[bf16_rows] * 8 + [f32_rows] * 6 + [
                        pltpu.VMEM((2, n_tiles, BF16_ROWS, DK), F32),
                        f32_rows, bf16_rows, f32_rows,
                        pltpu.VMEM((2, DK, DK), F32),
                        pltpu.VMEM((2, PER_TILE, DK, DK), F32),
                        pltpu.VMEM((2, PER_TILE, DK, DK), BF16),
                        pltpu.VMEM((2, SCAN_TILE, 2 * DK), BF16),
                        pltpu.VMEM((D, PROJ * DK), BF16),
                        pltpu.SMEM((n_tiles,), jnp.int32)],
        compiler_params=_params("parallel", "arbitrary"),
        name="hgrn_scan",
    )(*args)
    return o.reshape(n_seq, seq_len, D), s_fin


def kernel(x_prompt, x_sample, state_hgrn, c, c_ctx, ada_w, ada_b, norm_mix, norm_mlp, fnet_wo,
           hgrn_w_in, hgrn_lb, hgrn_norm, hgrn_wo, mlp_w1, mlp_w2, norm_final):
    bp, lp, _ = x_prompt.shape
    bs, ls, _ = x_sample.shape
    assert ada_w.shape[0] == 2 and ls == SEQ_PTS * SEQ_SLABS and lp == GD

    cs_ch, t_ctx, g_seq = _dft_tables()
    pos = _grid_pos_embed(ls, D)

    cond8 = jnp.zeros((8, D), F32).at[0].set(c_ctx).at[1:1 + bs].set(c)
    mod = _modulation(cond8, ada_w, ada_b)

    wo0 = fnet_wo.astype(BF16)
    wo1 = hgrn_wo.astype(BF16)
    w1 = mlp_w1.astype(BF16)
    w2 = mlp_w2.astype(BF16)
    w_heads = hgrn_w_in[0]
    gm = norm_mix.reshape(-1, 1, D)
    gf = norm_mlp.reshape(-1, 1, D)
    g_fin = norm_final.reshape(1, D)
    g_hn = hgrn_norm[0].reshape(1, DK)

    ctx_row = lambda r: 0
    smp_row = lambda r: 1 + r // ls

    mixed = _fnet_ctx(x_prompt, mod, 0, gm[0], cs_ch, t_ctx)
    x2, hn = _post_mixer(x_prompt.reshape(bp * lp, D), None, mixed.reshape(bp * lp, D), mod, 0,
                         ctx_row, wo0, w1, w2, gf[0], gm[1], final=False)
    o, new_state = _hgrn_mixer(hn.reshape(bp, lp, D), w_heads, hgrn_lb, g_hn, None)
    y_prompt = _post_mixer(x2, None, o.reshape(bp * lp, D), mod, 1, ctx_row, wo1, w1, w2,
                           gf[1], g_fin, final=True).reshape(bp, lp, D)

    mixed = _fnet_sample(x_sample, pos, mod, 0, gm[0], cs_ch, g_seq)
    x2, hn = _post_mixer(x_sample.reshape(bs * ls, D), pos, mixed.reshape(bs * ls, D), mod, 0,
                         smp_row, wo0, w1, w2, gf[0], gm[1], final=False)
    o, _ = _hgrn_mixer(hn.reshape(bs, ls, D), w_heads, hgrn_lb, g_hn, state_hgrn)
    y_sample = _post_mixer(x2, None, o.reshape(bs * ls, D), mod, 1, smp_row, wo1, w1, w2,
                           gf[1], g_fin, final=True).reshape(bs, ls, D)
    return (y_prompt, y_sample, new_state)
```

```python
import functools
import math

import numpy as np
import jax
import jax.numpy as jnp
from jax import lax
from jax.experimental import pallas as pl
from jax.experimental.pallas import tpu as pltpu

F32 = jnp.float32
BF16 = jnp.bfloat16

D = 1024
N_MOD = 6
D_FF = 4 * D
EPS = 1e-6
GROUPS = 4
GD = D // GROUPS
HEADS = 8
DK = 128
GRID_W = 64
POS_BASE = 10000.0
PROJ = 5

SEQ_SLABS = 16
SEQ_PTS = 256
LANES = 128
BF16_ROWS = 16
CHUNK = 64
SCAN_TILE = 256
EXP_SAFE = 85.0
SWEEP_UNROLL = 7
BACK_UNROLL = 7
POST_TM = 512
POST_SUB = 256
POST_TF = 1024

VMEM_LIMIT = 56 * 1024 * 1024


def _dft_tables():
    n = np.arange(GD)
    ang = 2.0 * np.pi * ((n[:, None] * n[None, :]) % GD) / GD
    cs_ch = np.concatenate([np.cos(ang), np.sin(ang)], axis=1) / math.sqrt(GD)
    t_ctx = np.concatenate([np.cos(ang), -np.sin(ang)], axis=1) / math.sqrt(GD)
    length = SEQ_PTS * SEQ_SLABS
    k1 = np.arange(SEQ_PTS)[:, None]
    n1 = np.arange(SEQ_PTS)[None, :]
    blocks = []
    for n2 in range(SEQ_SLABS):
        a = 2.0 * np.pi * ((k1 * (SEQ_SLABS * n1 + n2)) % length) / length
        gc, gs = np.cos(a) / math.sqrt(length), np.sin(a) / math.sqrt(length)
        blocks.append(np.block([[gc, -gs], [gs, gc]]))
    g_seq = np.stack(blocks)
    return tuple(jnp.asarray(t, F32).astype(BF16) for t in (cs_ch, t_ctx, g_seq))


def _grid_pos_embed(n_tok, d):
    rows = n_tok // GRID_W
    quarter = d // 4
    omega = 1.0 / (POS_BASE ** (jnp.arange(quarter, dtype=F32) / quarter))
    r = jnp.arange(rows, dtype=F32)[:, None] * omega[None, :]
    cl = jnp.arange(GRID_W, dtype=F32)[:, None] * omega[None, :]
    er = jnp.concatenate([jnp.sin(r), jnp.cos(r)], axis=-1)
    ec = jnp.concatenate([jnp.sin(cl), jnp.cos(cl)], axis=-1)
    emb = jnp.concatenate([jnp.broadcast_to(er[:, None, :], (rows, GRID_W, d // 2)),
                           jnp.broadcast_to(ec[None, :, :], (rows, GRID_W, d // 2))], axis=-1)
    return emb.reshape(n_tok, d)


def _silu(x):
    return x * jax.nn.sigmoid(x)


def _rms(x, g):
    return x * lax.rsqrt(jnp.mean(x * x, axis=-1, keepdims=True) + EPS) * g


def _mod_part(m, idx):
    return m[:, idx * D:(idx + 1) * D]


def _dot(a, b):
    return jnp.dot(a, b, preferred_element_type=F32)


def _dot_nt(a, b):
    return lax.dot_general(a, b, (((1,), (1,)), ((), ())), preferred_element_type=F32)


def _dot_tn(a, b):
    return lax.dot_general(a, b, (((0,), (0,)), ((), ())), preferred_element_type=F32)


def _params(*sem, **kw):
    return pltpu.CompilerParams(dimension_semantics=sem, vmem_limit_bytes=VMEM_LIMIT, **kw)


def _mod_kernel(c_ref, w_ref, b_ref, o_ref):
    s = _silu(c_ref[...])
    o_ref[0] = _dot(s.astype(BF16), w_ref[0].astype(BF16)) + b_ref[0]


def _modulation(cond8, ada_w, ada_b):
    depth = ada_w.shape[0]
    tn = 1536
    out = pl.pallas_call(
        _mod_kernel,
        grid=(depth, N_MOD * D // tn),
        in_specs=[pl.BlockSpec((8, D), lambda l, j: (0, 0)),
                  pl.BlockSpec((1, D, tn), lambda l, j: (l, 0, j)),
                  pl.BlockSpec((1, 1, tn), lambda l, j: (l, 0, j))],
        out_specs=pl.BlockSpec((1, 8, tn), lambda l, j: (l, 0, j)),
        out_shape=jax.ShapeDtypeStruct((depth, 8, N_MOD * D), F32),
        compiler_params=_params("parallel", "parallel"),
        name="adaln_mod",
    )(cond8, ada_w, ada_b.reshape(depth, 1, N_MOD * D))
    return out.reshape(depth, 8, 1, N_MOD * D)


def _channel_dft(hb, cs_ref, put):
    for g in range(GROUPS):
        y = _dot(hb[:, g * GD:(g + 1) * GD], cs_ref[...])
        put(0, g, y[:, :GD].astype(BF16))
        put(1, g, y[:, GD:].astype(BF16))


def _fnet_ctx_kernel(x_ref, mod_ref, g_ref, cs_ref, t_ref, o_ref, y_scr):
    m = mod_ref[0, 0]
    h = _rms(x_ref[0], g_ref[...]) * (1.0 + _mod_part(m, 1)) + _mod_part(m, 0)
    seq = x_ref.shape[1]

    def put(kind, g, val):
        y_scr[kind * seq:(kind + 1) * seq, g * GD:(g + 1) * GD] = val

    _channel_dft(h.astype(BF16), cs_ref, put)
    o_ref[0] = _dot(t_ref[...], y_scr[...]).astype(o_ref.dtype)


def _fnet_ctx(x, mod, layer, gain, cs_ch, t_ctx):
    b, seq, _ = x.shape
    return pl.pallas_call(
        _fnet_ctx_kernel,
        grid=(b,),
        in_specs=[pl.BlockSpec((1, seq, D), lambda i: (i, 0, 0)),
                  pl.BlockSpec((1, 1, 1, N_MOD * D), lambda i: (layer, 0, 0, 0)),
                  pl.BlockSpec((1, D), lambda i: (0, 0)),
                  pl.BlockSpec((GD, 2 * GD), lambda i: (0, 0)),
                  pl.BlockSpec((seq, 2 * seq), lambda i: (0, 0))],
        out_specs=pl.BlockSpec((1, seq, D), lambda i: (i, 0, 0)),
        out_shape=jax.ShapeDtypeStruct((b, seq, D), BF16),
        scratch_shapes=[pltpu.VMEM((2 * seq, D), BF16)],
        compiler_params=_params("parallel"),
        name="fnet_ctx",
    )(x, mod, gain, cs_ch, t_ctx)


def _fnet_slab_kernel(x_ref, pos_ref, mod_ref, g_ref, cs_ref, y_ref, col_scr, xp_scr):
    tm = x_ref.shape[1]
    rows = tm // SEQ_SLABS
    n_col = D // LANES
    for j in range(n_col):
        cols = slice(j * LANES, (j + 1) * LANES)
        col_scr[j] = x_ref[0, :, cols] + pos_ref[:, cols]
    for n2 in range(SEQ_SLABS):
        for j in range(n_col):
            xp_scr[n2 * rows:(n2 + 1) * rows, j * LANES:(j + 1) * LANES] = (
                col_scr[j, pl.ds(n2, rows, stride=SEQ_SLABS), :])
    m = mod_ref[0, 0]
    h = _rms(xp_scr[...], g_ref[...]) * (1.0 + _mod_part(m, 1)) + _mod_part(m, 0)

    def put(kind, g, val):
        for n2 in range(SEQ_SLABS):
            y_ref[0, n2, kind, :, g * GD:(g + 1) * GD] = val[n2 * rows:(n2 + 1) * rows, :]

    _channel_dft(h.astype(BF16), cs_ref, put)


def _fft_real_part(xs):
    n = len(xs)
    if n == 1:
        return xs
    ev = _fft_real_part(xs[0::2])
    od = _fft_real_part(xs[1::2])
    out = [None] * n
    for k in range(n // 2):
        c = math.cos(2.0 * math.pi * k / n)
        s = -math.sin(2.0 * math.pi * k / n)
        orr, oi = od[k]
        if k == 0:
            tr, ti = orr, oi
        elif 4 * k == n:
            tr, ti = oi, -orr
        else:
            tr, ti = orr * c - oi * s, orr * s + oi * c
        er, ei = ev[k]
        out[k] = (er + tr, ei + ti)
        out[k + n // 2] = (er - tr, ei - ti)
    return out


def _fnet_seq_kernel(y_ref, g_ref, o_ref, z_scr):
    tc = o_ref.shape[2]
    for n2 in range(SEQ_SLABS):
        yb = y_ref[0, n2].reshape(2 * SEQ_PTS, tc)
        z_scr[n2] = _dot(g_ref[n2], yb)

    rb = 8

    def body(r, carry):
        r0 = pl.multiple_of(r * rb, rb)
        for c in range(tc // LANES):
            cols = slice(c * LANES, (c + 1) * LANES)
            xs = []
            for n2 in range(SEQ_SLABS):
                re = z_scr[n2, pl.ds(r0, rb), cols]
                nim = z_scr[n2, pl.ds(SEQ_PTS + r0, rb), cols]
                xs.append((re, -nim))
            out = _fft_real_part(xs)
            for k2 in range(SEQ_SLABS):
                o_ref[0, pl.ds(k2 * SEQ_PTS + r0, rb), cols] = out[k2][0]
        return carry

    lax.fori_loop(0, SEQ_PTS // rb, body, 0)


def _fnet_sample(x, pos, mod, layer, gain, cs_ch, g_seq):
    b, seq, _ = x.shape
    tm = 512
    tc = 256
    y = pl.pallas_call(
        _fnet_slab_kernel,
        grid=(seq // tm, b),
        in_specs=[pl.BlockSpec((1, tm, D), lambda j, i: (i, j, 0)),
                  pl.BlockSpec((tm, D), lambda j, i: (j, 0)),
                  pl.BlockSpec((1, 1, 1, N_MOD * D), lambda j, i: (layer, 1 + i, 0, 0)),
                  pl.BlockSpec((1, D), lambda j, i: (0, 0)),
                  pl.BlockSpec((GD, 2 * GD), lambda j, i: (0, 0))],
        out_specs=pl.BlockSpec((1, SEQ_SLABS, 2, tm // SEQ_SLABS, D), lambda j, i: (i, 0, 0, j, 0)),
        out_shape=jax.ShapeDtypeStruct((b, SEQ_SLABS, 2, SEQ_PTS, D), BF16),
        scratch_shapes=[pltpu.VMEM((D // LANES, tm, LANES), F32), pltpu.VMEM((tm, D), F32)],
        compiler_params=_params("parallel", "parallel"),
        name="fnet_slab",
    )(x, pos, mod, gain, cs_ch)
    return pl.pallas_call(
        _fnet_seq_kernel,
        grid=(b, D // tc),
        in_specs=[pl.BlockSpec((1, SEQ_SLABS, 2, SEQ_PTS, tc), lambda i, j: (i, 0, 0, 0, j)),
                  pl.BlockSpec((SEQ_SLABS, 2 * SEQ_PTS, 2 * SEQ_PTS), lambda i, j: (0, 0, 0))],
        out_specs=pl.BlockSpec((1, seq, tc), lambda i, j: (i, 0, j)),
        out_shape=jax.ShapeDtypeStruct((b, seq, D), F32),
        scratch_shapes=[pltpu.VMEM((SEQ_SLABS, 2 * SEQ_PTS, tc), F32)],
        compiler_params=_params("parallel", "parallel"),
        name="fnet_seq",
    )(y, g_seq)


def _post_kernel(*refs, has_pos, final):
    it = iter(refs)
    x_ref = next(it)
    pos_ref = next(it) if has_pos else None
    a_ref, mod_ref, wo_ref, w1_ref, w2_ref, gmlp_ref, gnext_ref = (next(it) for _ in range(7))
    modn_ref = None if final else next(it)
    o_ref = next(it)
    hn_ref = None if final else next(it)
    x1_scr, h_scr, u_scr = (next(it) for _ in range(3))

    m = mod_ref[0, 0]
    subs = [slice(s * POST_SUB, (s + 1) * POST_SUB) for s in range(x_ref.shape[0] // POST_SUB)]

    for r in subs:
        x = x_ref[r, :]
        if has_pos:
            x = x + pos_ref[r, :]
        x1 = x + _mod_part(m, 2) * _dot(a_ref[r, :].astype(BF16), wo_ref[0])
        x1_scr[r, :] = x1
        h = _rms(x1, gmlp_ref[...]) * (1.0 + _mod_part(m, 4)) + _mod_part(m, 3)
        h_scr[r, :] = h.astype(BF16)

    for s, r in enumerate(subs):
        for j in range(D_FF // POST_TF):
            cols = slice(j * POST_TF, (j + 1) * POST_TF)
            u = jnp.maximum(_dot(h_scr[r, :], w1_ref[0, :, cols]), 0.0)
            u_scr[s, :, cols] = (u * u).astype(BF16)
        x2 = x1_scr[r, :] + _mod_part(m, 5) * _dot(u_scr[s], w2_ref[0])
        if final:
            o_ref[r, :] = _rms(x2, gnext_ref[...])
        else:
            o_ref[r, :] = x2
            mn = modn_ref[0, 0]
            hn = _rms(x2, gnext_ref[...]) * (1.0 + _mod_part(mn, 1)) + _mod_part(mn, 0)
            hn_ref[r, :] = hn.astype(BF16)


def _post_mixer(x, pos, a, mod, layer, mod_row, wo, w1, w2, g_mlp, g_next, final):
    rows = x.shape[0]
    tm = POST_TM
    has_pos = pos is not None
    n_pos = pos.shape[0] // tm if has_pos else 1
    row_map = lambda i: (i, 0)
    const = lambda i: (0, 0)
    resident = lambda shape: pl.BlockSpec(shape, lambda i: (layer, 0, 0), pipeline_mode=pl.Buffered(1))
    in_specs = [pl.BlockSpec((tm, D), row_map)]
    args = [x]
    if has_pos:
        in_specs.append(pl.BlockSpec((tm, D), lambda i: (i % n_pos, 0)))
        args.append(pos)
    in_specs += [pl.BlockSpec((tm, D), row_map),
                 pl.BlockSpec((1, 1, 1, N_MOD * D), lambda i: (layer, mod_row(i * tm), 0, 0)),
                 pl.BlockSpec((1, D, D), lambda i: (0, 0, 0), pipeline_mode=pl.Buffered(1)),
                 resident((1, D, D_FF)),
                 resident((1, D_FF, D)),
                 pl.BlockSpec((1, D), const),
                 pl.BlockSpec((1, D), const)]
    args += [a, mod, wo, w1, w2, g_mlp, g_next]
    out_specs = [pl.BlockSpec((tm, D), row_map)]
    out_shape = [jax.ShapeDtypeStruct((rows, D), F32)]
    if not final:
        in_specs.append(pl.BlockSpec((1, 1, 1, N_MOD * D),
                                     lambda i: (layer + 1, mod_row(i * tm), 0, 0)))
        args.append(mod)
        out_specs.append(pl.BlockSpec((tm, D), row_map))
        out_shape.append(jax.ShapeDtypeStruct((rows, D), BF16))
    res = pl.pallas_call(
        functools.partial(_post_kernel, has_pos=has_pos, final=final),
        grid=(rows // tm,),
        in_specs=in_specs,
        out_specs=out_specs,
        out_shape=out_shape,
        scratch_shapes=[pltpu.VMEM((tm, D), F32), pltpu.VMEM((tm, D), BF16),
                        pltpu.VMEM((tm // POST_SUB, POST_SUB, D_FF), BF16)],
        compiler_params=_params("parallel"),
        name="post_mlp_final" if final else "post_mlp",
    )(*args)
    return res[0] if final else (res[0], res[1])


PER_TILE = SCAN_TILE // CHUNK


def _decay_matrices():
    t = np.arange(SCAN_TILE)
    start = (t // CHUNK) * CHUNK
    first = np.arange(PER_TILE) * CHUNK
    same = (t[:, None] // CHUNK) == (t[None, :] // CHUNK)
    out = []
    for incl, mid, last in ((t[None, :] <= t[:, None], CHUNK // 2 - 1, CHUNK - 1),
                            (t[None, :] >= t[:, None], CHUNK // 2, 0)):
        bd = (same & incl).astype(np.float32)
        ref = bd[first + mid]
        pad = np.zeros((BF16_ROWS - 2 * PER_TILE, SCAN_TILE), np.float32)
        out.append(np.concatenate([bd - bd[start + mid], ref, bd[first + last] - ref, pad], axis=0))
    return jnp.asarray(np.stack(out), BF16)


def _hgrn_kernel(*refs, has_s0, independent):
    it = iter(refs)
    hn_ref = next(it)
    wp_refs = [next(it) for _ in range(PROJ)]
    lb_ref, gn_ref, dm_ref = (next(it) for _ in range(3))
    s0_ref = next(it) if has_s0 else None
    o_ref, sf_ref = next(it), next(it)
    qd_scr, kd_scr, qb_scr, kl_scr, g_scr, o_scr, k_scr = ((next(it), next(it)) for _ in range(7))
    ext_scr, q_scr, v_scr, sg_scr, st_scr, u_scr, sb_scr, gs_scr, w_scr, flag_scr = (
        next(it) for _ in range(10))

    @pl.when(pl.program_id(1) == 0)
    def _():
        for j, wp_ref in enumerate(wp_refs):
            w_scr[:, j * DK:(j + 1) * DK] = wp_ref[...].astype(BF16)

    seq = hn_ref.shape[1]
    tp = SCAN_TILE
    n_tiles = seq // tp
    per_tile = PER_TILE

    lraw = lb_ref[...]
    mx = jnp.max(lraw, axis=1, keepdims=True)
    ex = jnp.exp(lraw - mx)
    sm = ex / jnp.sum(ex, axis=1, keepdims=True)
    lbv = (sm[:, 0, :] + sm[:, 1, :]) - sm[:, 0, :]

    def gates(i):
        rows = pl.ds(pl.multiple_of(i * tp, tp), tp)
        p = _dot(hn_ref[0, rows, :], w_scr[...])
        q_scr[rows, :] = _silu(p[:, 0:DK])
        v_scr[rows, :] = p[:, 3 * DK:4 * DK].astype(BF16)
        sg_scr[rows, :] = _silu(p[:, 4 * DK:5 * DK])
        for d in range(2):
            lb = lbv[d:d + 1, :]
            f = lb + (1.0 - lb) * jax.nn.sigmoid(p[:, (1 + d) * DK:(2 + d) * DK])
            k_scr[d][rows, :] = 1.0 - f
            g = jnp.log(f)
            g_scr[d][rows, :] = g
            g1 = g.astype(BF16)
            gs_scr[d, :, 0:DK] = g1
            gs_scr[d, :, DK:2 * DK] = (g - g1.astype(F32)).astype(BF16)

    def decay_sums():
        return [_dot(dm_ref[d], gs_scr[d]) for d in range(2)]

    def factors(i, sums):
        rows = pl.ds(pl.multiple_of(i * tp, tp), tp)
        q = q_scr[rows, :]
        unsafe = jnp.int32(0)
        for d in range(2):
            r = sums[d][:, :DK] + sums[d][:, DK:]
            a = r[:tp]
            ext = r[tp:]
            ext_scr[d, i] = ext
            unsafe = unsafe | (jnp.max(jnp.abs(a)) > EXP_SAFE).astype(jnp.int32)
            qd = q * jnp.exp(a)
            kd = k_scr[d][rows, :] * jnp.exp(-a)
            qd_scr[d][rows, :] = qd.astype(BF16)
            kd_scr[d][rows, :] = kd.astype(BF16)
            e_ref = jnp.exp(ext[0:per_tile])
            e_lr = jnp.exp(ext[per_tile:2 * per_tile])
            for c in range(per_tile):
                cr = pl.ds(pl.multiple_of(i * tp + c * CHUNK, CHUNK), CHUNK)
                cs = slice(c * CHUNK, (c + 1) * CHUNK)
                qb_scr[d][cr, :] = (qd[cs] * e_ref[c:c + 1]).astype(BF16)
                kl_scr[d][cr, :] = (kd[cs] * e_lr[c:c + 1]).astype(BF16)
        flag_scr[i] = unsafe

    def init_states():
        for d in range(2):
            if has_s0:
                st_scr[d] = s0_ref[0, 0, d, 0].T
            else:
                st_scr[d] = jnp.zeros((DK, DK), F32)

    def state_in(d):
        return jnp.zeros((DK, DK), F32) if independent else st_scr[d]

    def state_out(d, tile, st):
        if independent:
            sf_ref[tile, 0, d, 0] = st.T
        else:
            st_scr[d] = st

    if not independent:
        init_states()

    ti = lax.broadcasted_iota(jnp.int32, (tp, tp), 0)
    si = lax.broadcasted_iota(jnp.int32, (tp, tp), 1)
    same = (ti // CHUNK) == (si // CHUNK)
    masks = (same & (ti >= si), same & (si >= ti))

    def scan_head(d, tile, slot=None):
        slot = d if slot is None else slot
        order = range(per_tile) if d == 0 else range(per_tile - 1, -1, -1)
        r0 = pl.multiple_of(tile * tp, tp)
        vv = v_scr[pl.ds(r0, tp), :]
        for c in range(per_tile):
            cr = pl.ds(pl.multiple_of(r0 + c * CHUNK, CHUNK), CHUNK)
            u_scr[slot, c] = _dot_tn(vv[c * CHUNK:(c + 1) * CHUNK, :], kl_scr[d][cr, :])
        ext = ext_scr[d, tile]
        decay = jnp.exp(ext[0:per_tile] + ext[per_tile:2 * per_tile])
        st = state_in(d)
        for c in order:
            sb_scr[slot, c] = st.astype(BF16)
            st = st * decay[c:c + 1] + u_scr[slot, c]
        state_out(d, tile, st)

    def scan_scores(d, tile):
        rows = pl.ds(pl.multiple_of(tile * tp, tp), tp)
        return jnp.where(masks[d], _dot_nt(qd_scr[d][rows, :], kd_scr[d][rows, :]), 0.0)

    def scan_out(d, tile, sc, slot=None):
        slot = d if slot is None else slot
        r0 = pl.multiple_of(tile * tp, tp)
        rows = pl.ds(r0, tp)
        o_scr[d][rows, :] = _dot(sc.astype(BF16), v_scr[rows, :])
        for c in range(per_tile):
            cr = pl.ds(pl.multiple_of(r0 + c * CHUNK, CHUNK), CHUNK)
            o_scr[d][cr, :] += _dot_nt(qb_scr[d][cr, :], sb_scr[slot, c])

    def scan_tail(d, tile, slot=None):
        scan_out(d, tile, scan_scores(d, tile), slot)

    def scan_tail_pair(a, b):
        sc = [scan_scores(d, tile) for d, tile, _ in (a, b)]
        for (d, tile, slot), s in zip((a, b), sc):
            scan_out(d, tile, s, slot)

    rid = lax.broadcasted_iota(jnp.int32, (BF16_ROWS, DK), 0)

    def scan_exact(d, tile):
        r0 = tile * tp

        def step(j, st):
            t = r0 + ((tp - 1 - j) if d == 1 else j)
            t0 = pl.multiple_of((t // BF16_ROWS) * BF16_ROWS, BF16_ROWS)
            grp = pl.ds(t0, BF16_ROWS)
            sel = rid == (t - t0)
            g = g_scr[d][grp, :]
            f = jnp.exp(jnp.sum(jnp.where(sel, g, 0.0), axis=0, keepdims=True))
            k = jnp.where(sel, 1.0 - jnp.exp(g), 0.0)
            st = st * f + _dot_tn(v_scr[grp, :], k.astype(BF16))
            o = _dot_nt(q_scr[grp, :].astype(BF16), st.astype(BF16))
            o_scr[d][grp, :] = jnp.where(sel, o, o_scr[d][grp, :])
            return st

        state_out(d, tile, lax.fori_loop(0, tp, step, state_in(d)))

    def finish(i):
        rows = pl.ds(pl.multiple_of(i * tp, tp), tp)
        o = o_scr[0][rows, :] + o_scr[1][rows, :]
        o = o * lax.rsqrt(jnp.mean(o * o, axis=-1, keepdims=True) + EPS) * gn_ref[...]
        o_ref[0, rows, :] = (o * sg_scr[rows, :]).astype(o_ref.dtype)

    dirs = (0, 1) if independent else (0,)

    def scan_heads(t):
        for d in dirs:
            scan_head(d, t)

    def scan_tails(t):
        if independent:
            scan_tail_pair((0, t, 0), (1, t, 1))
            finish(t)
        else:
            scan_tail(0, t)

    last = n_tiles - 1
    gates(0)
    if n_tiles > 1:
        sums = decay_sums()
        gates(1)
        factors(0, sums)

        def forward(i, carry):
            scan_heads(i - 1)
            sums = decay_sums()
            gates(i + 1)
            factors(i, sums)
            scan_tails(i - 1)
            return carry

        lax.fori_loop(1, last, forward, 0, unroll=SWEEP_UNROLL)
        scan_heads(last - 1)
        factors(last, decay_sums())
        scan_tails(last - 1)
    else:
        factors(0, decay_sums())

    if independent:
        scan_heads(last)
        scan_tails(last)
    else:
        scan_head(0, last)
        scan_head(1, last)
        scan_tail_pair((0, last, 0), (1, last, 1))
        finish(last)
        t0 = last - 1
        if t0 % 2 == 0:
            scan_head(1, t0, 0)
            scan_tail(1, t0, 0)
            finish(t0)
            t0 -= 1

        def backward(j, carry):
            t = t0 - 2 * j
            scan_head(1, t, 0)
            scan_head(1, t - 1, 1)
            scan_tail_pair((1, t, 0), (1, t - 1, 1))
            finish(t)
            finish(t - 1)
            return carry

        lax.fori_loop(0, (t0 + 1) // 2, backward, 0, unroll=BACK_UNROLL)

    unsafe = flag_scr[0]
    for t in range(1, n_tiles):
        unsafe = unsafe | flag_scr[t]

    @pl.when(unsafe != 0)
    def _():
        if not independent:
            init_states()

        def resweep(d):
            def body(j, carry):
                t = j if d == 0 else last - j

                @pl.when(flag_scr[t] != 0)
                def _():
                    scan_exact(d, t)

                @pl.when(flag_scr[t] == 0)
                def _():
                    scan_head(d, t)
                    scan_tail(d, t)

                if d == 1:
                    finish(t)
                return carry

            lax.fori_loop(0, n_tiles, body, 0)

        resweep(0)
        resweep(1)

    if not independent:
        for d in range(2):
            sf_ref[0, 0, d, 0] = st_scr[d].T


def _hgrn_mixer(hn, w_in, lb_raw, g_norm, s0):
    n_seq, seq_len, _ = hn.shape
    has_s0 = s0 is not None
    independent = (not has_s0) and seq_len == SCAN_TILE
    if independent:
        hn = hn.reshape(1, n_seq * seq_len, D)
    b, seq, _ = hn.shape
    st_rows = n_seq if independent else 1
    st_spec = pl.BlockSpec((st_rows, 1, 2, 1, DK, DK), lambda h, i: (i, 0, 0, h, 0, 0))
    in_specs = [pl.BlockSpec((1, seq, D), lambda h, i: (i, 0, 0))]
    in_specs += [pl.BlockSpec((D, DK), functools.partial(lambda j, h, i: (0, j * HEADS + h), j))
                 for j in range(PROJ)]
    in_specs += [pl.BlockSpec((2, 2, DK), lambda h, i: (0, 0, h)),
                 pl.BlockSpec((1, DK), lambda h, i: (0, 0)),
                 pl.BlockSpec((2, SCAN_TILE + BF16_ROWS, SCAN_TILE), lambda h, i: (0, 0, 0))]
    args = [hn] + [w_in] * PROJ + [lb_raw, g_norm, _decay_matrices()]
    if has_s0:
        in_specs.append(st_spec)
        args.append(s0)
    n_tiles = seq // SCAN_TILE
    bf16_rows = pltpu.VMEM((seq, DK), BF16)
    f32_rows = pltpu.VMEM((seq, DK), F32)
    o, s_fin = pl.pallas_call(
        functools.partial(_hgrn_kernel, has_s0=has_s0, independent=independent),
        grid=(HEADS, b),
        in_specs=in_specs,
        out_specs=[pl.BlockSpec((1, seq, DK), lambda h, i: (i, 0, h)), st_spec],
        out_shape=[jax.ShapeDtypeStruct((b, seq, D), BF16),
                   jax.ShapeDtypeStruct((n_seq, 1, 2, HEADS, DK, DK), F32)],
        scratch_shapes=[bf16_rows] * 8 + [f32_rows] * 6 + [
                        pltpu.VMEM((2, n_tiles, BF16_ROWS, DK), F32),
                        f32_rows, bf16_rows, f32_rows,
                        pltpu.VMEM((2, DK, DK), F32),
                        pltpu.VMEM((2, PER_TILE, DK, DK), F32),
                        pltpu.VMEM((2, PER_TILE, DK, DK), BF16),
                        pltpu.VMEM((2, SCAN_TILE, 2 * DK), BF16),
                        pltpu.VMEM((D, PROJ * DK), BF16),
                        pltpu.SMEM((n_tiles,), jnp.int32)],
        compiler_params=_params("parallel", "arbitrary"),
        name="hgrn_scan",
    )(*args)
    return o.reshape(n_seq, seq_len, D), s_fin


def kernel(x_prompt, x_sample, state_hgrn, c, c_ctx, ada_w, ada_b, norm_mix, norm_mlp, fnet_wo,
           hgrn_w_in, hgrn_lb, hgrn_norm, hgrn_wo, mlp_w1, mlp_w2, norm_final):
    bp, lp, _ = x_prompt.shape
    bs, ls, _ = x_sample.shape
    assert ada_w.shape[0] == 2 and ls == SEQ_PTS * SEQ_SLABS and lp == GD

    cs_ch, t_ctx, g_seq = _dft_tables()
    pos = _grid_pos_embed(ls, D)

    cond8 = jnp.zeros((8, D), F32).at[0].set(c_ctx).at[1:1 + bs].set(c)
    mod = _modulation(cond8, ada_w, ada_b)

    wo0 = fnet_wo.astype(BF16)
    wo1 = hgrn_wo.astype(BF16)
    w1 = mlp_w1.astype(BF16)
    w2 = mlp_w2.astype(BF16)
    w_heads = hgrn_w_in[0]
    gm = norm_mix.reshape(-1, 1, D)
    gf = norm_mlp.reshape(-1, 1, D)
    g_fin = norm_final.reshape(1, D)
    g_hn = hgrn_norm[0].reshape(1, DK)

    ctx_row = lambda r: 0
    smp_row = lambda r: 1 + r // ls

    mixed = _fnet_ctx(x_prompt, mod, 0, gm[0], cs_ch, t_ctx)
    x2, hn = _post_mixer(x_prompt.reshape(bp * lp, D), None, mixed.reshape(bp * lp, D), mod, 0,
                         ctx_row, wo0, w1, w2, gf[0], gm[1], final=False)
    o, new_state = _hgrn_mixer(hn.reshape(bp, lp, D), w_heads, hgrn_lb, g_hn, None)
    y_prompt = _post_mixer(x2, None, o.reshape(bp * lp, D), mod, 1, ctx_row, wo1, w1, w2,
                           gf[1], g_fin, final=True).reshape(bp, lp, D)

    mixed = _fnet_sample(x_sample, pos, mod, 0, gm[0], cs_ch, g_seq)
    x2, hn = _post_mixer(x_sample.reshape(bs * ls, D), pos, mixed.reshape(bs * ls, D), mod, 0,
                         smp_row, wo0, w1, w2, gf[0], gm[1], final=False)
    o, _ = _hgrn_mixer(hn.reshape(bs, ls, D), w_heads, hgrn_lb, g_hn, state_hgrn)
    y_sample = _post_mixer(x2, None, o.reshape(bs * ls, D), mod, 1, smp_row, wo1, w1, w2,
                           gf[1], g_fin, final=True).reshape(bs, ls, D)
    return (y_prompt, y_sample, new_state)
```

```python
import functools
import math

import numpy as np
import jax
import jax.numpy as jnp
from jax import lax
from jax.experimental import pallas as pl
from jax.experimental.pallas import tpu as pltpu

F32 = jnp.float32
BF16 = jnp.bfloat16

D = 1024
N_MOD = 6
D_FF = 4 * D
EPS = 1e-6
GROUPS = 4
GD = D // GROUPS
HEADS = 8
DK = 128
GRID_W = 64
POS_BASE = 10000.0
PROJ = 5

SEQ_SLABS = 16
SEQ_PTS = 256
LANES = 128
BF16_ROWS = 16
CHUNK = 64
SCAN_TILE = 256
EXP_SAFE = 85.0
SWEEP_UNROLL = 14
BACK_UNROLL = 7
POST_TM = 512
POST_SUB = 256
POST_TF = 1024

VMEM_LIMIT = 56 * 1024 * 1024


def _dft_tables():
    n = np.arange(GD)
    ang = 2.0 * np.pi * ((n[:, None] * n[None, :]) % GD) / GD
    cs_ch = np.concatenate([np.cos(ang), np.sin(ang)], axis=1) / math.sqrt(GD)
    t_ctx = np.concatenate([np.cos(ang), -np.sin(ang)], axis=1) / math.sqrt(GD)
    length = SEQ_PTS * SEQ_SLABS
    k1 = np.arange(SEQ_PTS)[:, None]
    n1 = np.arange(SEQ_PTS)[None, :]
    blocks = []
    for n2 in range(SEQ_SLABS):
        a = 2.0 * np.pi * ((k1 * (SEQ_SLABS * n1 + n2)) % length) / length
        gc, gs = np.cos(a) / math.sqrt(length), np.sin(a) / math.sqrt(length)
        blocks.append(np.block([[gc, -gs], [gs, gc]]))
    g_seq = np.stack(blocks)
    return tuple(jnp.asarray(t, F32).astype(BF16) for t in (cs_ch, t_ctx, g_seq))


def _grid_pos_embed(n_tok, d):
    rows = n_tok // GRID_W
    quarter = d // 4
    omega = 1.0 / (POS_BASE ** (jnp.arange(quarter, dtype=F32) / quarter))
    r = jnp.arange(rows, dtype=F32)[:, None] * omega[None, :]
    cl = jnp.arange(GRID_W, dtype=F32)[:, None] * omega[None, :]
    er = jnp.concatenate([jnp.sin(r), jnp.cos(r)], axis=-1)
    ec = jnp.concatenate([jnp.sin(cl), jnp.cos(cl)], axis=-1)
    emb = jnp.concatenate([jnp.broadcast_to(er[:, None, :], (rows, GRID_W, d // 2)),
                           jnp.broadcast_to(ec[None, :, :], (rows, GRID_W, d // 2))], axis=-1)
    return emb.reshape(n_tok, d)


def _silu(x):
    return x * jax.nn.sigmoid(x)


def _rms(x, g):
    return x * lax.rsqrt(jnp.mean(x * x, axis=-1, keepdims=True) + EPS) * g


def _mod_part(m, idx):
    return m[:, idx * D:(idx + 1) * D]


def _dot(a, b):
    return jnp.dot(a, b, preferred_element_type=F32)


def _dot_nt(a, b):
    return lax.dot_general(a, b, (((1,), (1,)), ((), ())), preferred_element_type=F32)


def _dot_tn(a, b):
    return lax.dot_general(a, b, (((0,), (0,)), ((), ())), preferred_element_type=F32)


def _params(*sem, **kw):
    return pltpu.CompilerParams(dimension_semantics=sem, vmem_limit_bytes=VMEM_LIMIT, **kw)


def _mod_kernel(c_ref, w_ref, b_ref, o_ref):
    s = _silu(c_ref[...])
    o_ref[0] = _dot(s.astype(BF16), w_ref[0].astype(BF16)) + b_ref[0]


def _modulation(cond8, ada_w, ada_b):
    depth = ada_w.shape[0]
    tn = 1536
    out = pl.pallas_call(
        _mod_kernel,
        grid=(depth, N_MOD * D // tn),
        in_specs=[pl.BlockSpec((8, D), lambda l, j: (0, 0)),
                  pl.BlockSpec((1, D, tn), lambda l, j: (l, 0, j)),
                  pl.BlockSpec((1, 1, tn), lambda l, j: (l, 0, j))],
        out_specs=pl.BlockSpec((1, 8, tn), lambda l, j: (l, 0, j)),
        out_shape=jax.ShapeDtypeStruct((depth, 8, N_MOD * D), F32),
        compiler_params=_params("parallel", "parallel"),
        name="adaln_mod",
    )(cond8, ada_w, ada_b.reshape(depth, 1, N_MOD * D))
    return out.reshape(depth, 8, 1, N_MOD * D)


def _channel_dft(hb, cs_ref, put):
    for g in range(GROUPS):
        y = _dot(hb[:, g * GD:(g + 1) * GD], cs_ref[...])
        put(0, g, y[:, :GD].astype(BF16))
        put(1, g, y[:, GD:].astype(BF16))


def _fnet_ctx_kernel(x_ref, mod_ref, g_ref, cs_ref, t_ref, o_ref, y_scr):
    m = mod_ref[0, 0]
    h = _rms(x_ref[0], g_ref[...]) * (1.0 + _mod_part(m, 1)) + _mod_part(m, 0)
    seq = x_ref.shape[1]

    def put(kind, g, val):
        y_scr[kind * seq:(kind + 1) * seq, g * GD:(g + 1) * GD] = val

    _channel_dft(h.astype(BF16), cs_ref, put)
    o_ref[0] = _dot(t_ref[...], y_scr[...]).astype(o_ref.dtype)


def _fnet_ctx(x, mod, layer, gain, cs_ch, t_ctx):
    b, seq, _ = x.shape
    return pl.pallas_call(
        _fnet_ctx_kernel,
        grid=(b,),
        in_specs=[pl.BlockSpec((1, seq, D), lambda i: (i, 0, 0)),
                  pl.BlockSpec((1, 1, 1, N_MOD * D), lambda i: (layer, 0, 0, 0)),
                  pl.BlockSpec((1, D), lambda i: (0, 0)),
                  pl.BlockSpec((GD, 2 * GD), lambda i: (0, 0)),
                  pl.BlockSpec((seq, 2 * seq), lambda i: (0, 0))],
        out_specs=pl.BlockSpec((1, seq, D), lambda i: (i, 0, 0)),
        out_shape=jax.ShapeDtypeStruct((b, seq, D), BF16),
        scratch_shapes=[pltpu.VMEM((2 * seq, D), BF16)],
        compiler_params=_params("parallel"),
        name="fnet_ctx",
    )(x, mod, gain, cs_ch, t_ctx)


def _fnet_slab_kernel(x_ref, pos_ref, mod_ref, g_ref, cs_ref, y_ref, col_scr, xp_scr):
    tm = x_ref.shape[1]
    rows = tm // SEQ_SLABS
    n_col = D // LANES
    for j in range(n_col):
        cols = slice(j * LANES, (j + 1) * LANES)
        col_scr[j] = x_ref[0, :, cols] + pos_ref[:, cols]
    for n2 in range(SEQ_SLABS):
        for j in range(n_col):
            xp_scr[n2 * rows:(n2 + 1) * rows, j * LANES:(j + 1) * LANES] = (
                col_scr[j, pl.ds(n2, rows, stride=SEQ_SLABS), :])
    m = mod_ref[0, 0]
    h = _rms(xp_scr[...], g_ref[...]) * (1.0 + _mod_part(m, 1)) + _mod_part(m, 0)

    def put(kind, g, val):
        for n2 in range(SEQ_SLABS):
            y_ref[0, n2, kind, :, g * GD:(g + 1) * GD] = val[n2 * rows:(n2 + 1) * rows, :]

    _channel_dft(h.astype(BF16), cs_ref, put)


def _fft_real_part(xs):
    n = len(xs)
    if n == 1:
        return xs
    ev = _fft_real_part(xs[0::2])
    od = _fft_real_part(xs[1::2])
    out = [None] * n
    for k in range(n // 2):
        c = math.cos(2.0 * math.pi * k / n)
        s = -math.sin(2.0 * math.pi * k / n)
        orr, oi = od[k]
        if k == 0:
            tr, ti = orr, oi
        elif 4 * k == n:
            tr, ti = oi, -orr
        else:
            tr, ti = orr * c - oi * s, orr * s + oi * c
        er, ei = ev[k]
        out[k] = (er + tr, ei + ti)
        out[k + n // 2] = (er - tr, ei - ti)
    return out


def _fnet_seq_kernel(y_ref, g_ref, o_ref, z_scr):
    tc = o_ref.shape[2]
    for n2 in range(SEQ_SLABS):
        yb = y_ref[0, n2].reshape(2 * SEQ_PTS, tc)
        z_scr[n2] = _dot(g_ref[n2], yb)

    rb = 8

    def body(r, carry):
        r0 = pl.multiple_of(r * rb, rb)
        for c in range(tc // LANES):
            cols = slice(c * LANES, (c + 1) * LANES)
            xs = []
            for n2 in range(SEQ_SLABS):
                re = z_scr[n2, pl.ds(r0, rb), cols]
                nim = z_scr[n2, pl.ds(SEQ_PTS + r0, rb), cols]
                xs.append((re, -nim))
            out = _fft_real_part(xs)
            for k2 in range(SEQ_SLABS):
                o_ref[0, pl.ds(k2 * SEQ_PTS + r0, rb), cols] = out[k2][0]
        return carry

    lax.fori_loop(0, SEQ_PTS // rb, body, 0)


def _fnet_sample(x, pos, mod, layer, gain, cs_ch, g_seq):
    b, seq, _ = x.shape
    tm = 512
    tc = 256
    y = pl.pallas_call(
        _fnet_slab_kernel,
        grid=(seq // tm, b),
        in_specs=[pl.BlockSpec((1, tm, D), lambda j, i: (i, j, 0)),
                  pl.BlockSpec((tm, D), lambda j, i: (j, 0)),
                  pl.BlockSpec((1, 1, 1, N_MOD * D), lambda j, i: (layer, 1 + i, 0, 0)),
                  pl.BlockSpec((1, D), lambda j, i: (0, 0)),
                  pl.BlockSpec((GD, 2 * GD), lambda j, i: (0, 0))],
        out_specs=pl.BlockSpec((1, SEQ_SLABS, 2, tm // SEQ_SLABS, D), lambda j, i: (i, 0, 0, j, 0)),
        out_shape=jax.ShapeDtypeStruct((b, SEQ_SLABS, 2, SEQ_PTS, D), BF16),
        scratch_shapes=[pltpu.VMEM((D // LANES, tm, LANES), F32), pltpu.VMEM((tm, D), F32)],
        compiler_params=_params("parallel", "parallel"),
        name="fnet_slab",
    )(x, pos, mod, gain, cs_ch)
    return pl.pallas_call(
        _fnet_seq_kernel,
        grid=(b, D // tc),
        in_specs=[pl.BlockSpec((1, SEQ_SLABS, 2, SEQ_PTS, tc), lambda i, j: (i, 0, 0, 0, j)),
                  pl.BlockSpec((SEQ_SLABS, 2 * SEQ_PTS, 2 * SEQ_PTS), lambda i, j: (0, 0, 0))],
        out_specs=pl.BlockSpec((1, seq, tc), lambda i, j: (i, 0, j)),
        out_shape=jax.ShapeDtypeStruct((b, seq, D), F32),
        scratch_shapes=[pltpu.VMEM((SEQ_SLABS, 2 * SEQ_PTS, tc), F32)],
        compiler_params=_params("parallel", "parallel"),
        name="fnet_seq",
    )(y, g_seq)


def _post_kernel(*refs, has_pos, final):
    it = iter(refs)
    x_ref = next(it)
    pos_ref = next(it) if has_pos else None
    a_ref, mod_ref, wo_ref, w1_ref, w2_ref, gmlp_ref, gnext_ref = (next(it) for _ in range(7))
    modn_ref = None if final else next(it)
    o_ref = next(it)
    hn_ref = None if final else next(it)
    x1_scr, h_scr, u_scr = (next(it) for _ in range(3))

    m = mod_ref[0, 0]
    subs = [slice(s * POST_SUB, (s + 1) * POST_SUB) for s in range(x_ref.shape[0] // POST_SUB)]

    for r in subs:
        x = x_ref[r, :]
        if has_pos:
            x = x + pos_ref[r, :]
        x1 = x + _mod_part(m, 2) * _dot(a_ref[r, :].astype(BF16), wo_ref[0])
        x1_scr[r, :] = x1
        h = _rms(x1, gmlp_ref[...]) * (1.0 + _mod_part(m, 4)) + _mod_part(m, 3)
        h_scr[r, :] = h.astype(BF16)

    for s, r in enumerate(subs):
        for j in range(D_FF // POST_TF):
            cols = slice(j * POST_TF, (j + 1) * POST_TF)
            u = jnp.maximum(_dot(h_scr[r, :], w1_ref[0, :, cols]), 0.0)
            u_scr[s, :, cols] = (u * u).astype(BF16)
        x2 = x1_scr[r, :] + _mod_part(m, 5) * _dot(u_scr[s], w2_ref[0])
        if final:
            o_ref[r, :] = _rms(x2, gnext_ref[...])
        else:
            o_ref[r, :] = x2
            mn = modn_ref[0, 0]
            hn = _rms(x2, gnext_ref[...]) * (1.0 + _mod_part(mn, 1)) + _mod_part(mn, 0)
            hn_ref[r, :] = hn.astype(BF16)


def _post_mixer(x, pos, a, mod, layer, mod_row, wo, w1, w2, g_mlp, g_next, final):
    rows = x.shape[0]
    tm = POST_TM
    has_pos = pos is not None
    n_pos = pos.shape[0] // tm if has_pos else 1
    row_map = lambda i: (i, 0)
    const = lambda i: (0, 0)
    resident = lambda shape: pl.BlockSpec(shape, lambda i: (layer, 0, 0), pipeline_mode=pl.Buffered(1))
    in_specs = [pl.BlockSpec((tm, D), row_map)]
    args = [x]
    if has_pos:
        in_specs.append(pl.BlockSpec((tm, D), lambda i: (i % n_pos, 0)))
        args.append(pos)
    in_specs += [pl.BlockSpec((tm, D), row_map),
                 pl.BlockSpec((1, 1, 1, N_MOD * D), lambda i: (layer, mod_row(i * tm), 0, 0)),
                 pl.BlockSpec((1, D, D), lambda i: (0, 0, 0), pipeline_mode=pl.Buffered(1)),
                 resident((1, D, D_FF)),
                 resident((1, D_FF, D)),
                 pl.BlockSpec((1, D), const),
                 pl.BlockSpec((1, D), const)]
    args += [a, mod, wo, w1, w2, g_mlp, g_next]
    out_specs = [pl.BlockSpec((tm, D), row_map)]
    out_shape = [jax.ShapeDtypeStruct((rows, D), F32)]
    if not final:
        in_specs.append(pl.BlockSpec((1, 1, 1, N_MOD * D),
                                     lambda i: (layer + 1, mod_row(i * tm), 0, 0)))
        args.append(mod)
        out_specs.append(pl.BlockSpec((tm, D), row_map))
        out_shape.append(jax.ShapeDtypeStruct((rows, D), BF16))
    res = pl.pallas_call(
        functools.partial(_post_kernel, has_pos=has_pos, final=final),
        grid=(rows // tm,),
        in_specs=in_specs,
        out_specs=out_specs,
        out_shape=out_shape,
        scratch_shapes=[pltpu.VMEM((tm, D), F32), pltpu.VMEM((tm, D), BF16),
                        pltpu.VMEM((tm // POST_SUB, POST_SUB, D_FF), BF16)],
        compiler_params=_params("parallel"),
        name="post_mlp_final" if final else "post_mlp",
    )(*args)
    return res[0] if final else (res[0], res[1])


PER_TILE = SCAN_TILE // CHUNK


def _decay_matrices():
    t = np.arange(SCAN_TILE)
    start = (t // CHUNK) * CHUNK
    first = np.arange(PER_TILE) * CHUNK
    same = (t[:, None] // CHUNK) == (t[None, :] // CHUNK)
    out = []
    for incl, mid, last in ((t[None, :] <= t[:, None], CHUNK // 2 - 1, CHUNK - 1),
                            (t[None, :] >= t[:, None], CHUNK // 2, 0)):
        bd = (same & incl).astype(np.float32)
        ref = bd[first + mid]
        pad = np.zeros((BF16_ROWS - 2 * PER_TILE, SCAN_TILE), np.float32)
        out.append(np.concatenate([bd - bd[start + mid], ref, bd[first + last] - ref, pad], axis=0))
    return jnp.asarray(np.stack(out), BF16)


def _hgrn_kernel(*refs, has_s0, independent):
    it = iter(refs)
    hn_ref = next(it)
    wp_refs = [next(it) for _ in range(PROJ)]
    lb_ref, gn_ref, dm_ref = (next(it) for _ in range(3))
    s0_ref = next(it) if has_s0 else None
    o_ref, sf_ref = next(it), next(it)
    qd_scr, kd_scr, qb_scr, kl_scr, g_scr, o_scr, k_scr = ((next(it), next(it)) for _ in range(7))
    ext_scr, q_scr, v_scr, sg_scr, st_scr, u_scr, sb_scr, gs_scr, w_scr, flag_scr = (
        next(it) for _ in range(10))

    @pl.when(pl.program_id(1) == 0)
    def _():
        for j, wp_ref in enumerate(wp_refs):
            w_scr[:, j * DK:(j + 1) * DK] = wp_ref[...].astype(BF16)

    seq = hn_ref.shape[1]
    tp = SCAN_TILE
    n_tiles = seq // tp
    per_tile = PER_TILE

    lraw = lb_ref[...]
    mx = jnp.max(lraw, axis=1, keepdims=True)
    ex = jnp.exp(lraw - mx)
    sm = ex / jnp.sum(ex, axis=1, keepdims=True)
    lbv = (sm[:, 0, :] + sm[:, 1, :]) - sm[:, 0, :]

    def gates(i):
        rows = pl.ds(pl.multiple_of(i * tp, tp), tp)
        p = _dot(hn_ref[0, rows, :], w_scr[...])
        q_scr[rows, :] = _silu(p[:, 0:DK])
        v_scr[rows, :] = p[:, 3 * DK:4 * DK].astype(BF16)
        sg_scr[rows, :] = _silu(p[:, 4 * DK:5 * DK])
        for d in range(2):
            lb = lbv[d:d + 1, :]
            f = lb + (1.0 - lb) * jax.nn.sigmoid(p[:, (1 + d) * DK:(2 + d) * DK])
            k_scr[d][rows, :] = 1.0 - f
            g = jnp.log(f)
            g_scr[d][rows, :] = g
            g1 = g.astype(BF16)
            gs_scr[d, :, 0:DK] = g1
            gs_scr[d, :, DK:2 * DK] = (g - g1.astype(F32)).astype(BF16)

    def decay_sums():
        return [_dot(dm_ref[d], gs_scr[d]) for d in range(2)]

    def factors(i, sums):
        rows = pl.ds(pl.multiple_of(i * tp, tp), tp)
        q = q_scr[rows, :]
        unsafe = jnp.int32(0)
        for d in range(2):
            r = sums[d][:, :DK] + sums[d][:, DK:]
            a = r[:tp]
            ext = r[tp:]
            ext_scr[d, i] = ext
            unsafe = unsafe | (jnp.max(jnp.abs(a)) > EXP_SAFE).astype(jnp.int32)
            qd = q * jnp.exp(a)
            kd = k_scr[d][rows, :] * jnp.exp(-a)
            qd_scr[d][rows, :] = qd.astype(BF16)
            kd_scr[d][rows, :] = kd.astype(BF16)
            e_ref = jnp.exp(ext[0:per_tile])
            e_lr = jnp.exp(ext[per_tile:2 * per_tile])
            for c in range(per_tile):
                cr = pl.ds(pl.multiple_of(i * tp + c * CHUNK, CHUNK), CHUNK)
                cs = slice(c * CHUNK, (c + 1) * CHUNK)
                qb_scr[d][cr, :] = (qd[cs] * e_ref[c:c + 1]).astype(BF16)
                kl_scr[d][cr, :] = (kd[cs] * e_lr[c:c + 1]).astype(BF16)
        flag_scr[i] = unsafe

    def init_states():
        for d in range(2):
            if has_s0:
                st_scr[d] = s0_ref[0, 0, d, 0].T
            else:
                st_scr[d] = jnp.zeros((DK, DK), F32)

    def state_in(d):
        return jnp.zeros((DK, DK), F32) if independent else st_scr[d]

    def state_out(d, tile, st):
        if independent:
            sf_ref[tile, 0, d, 0] = st.T
        else:
            st_scr[d] = st

    if not independent:
        init_states()

    ti = lax.broadcasted_iota(jnp.int32, (tp, tp), 0)
    si = lax.broadcasted_iota(jnp.int32, (tp, tp), 1)
    same = (ti // CHUNK) == (si // CHUNK)
    masks = (same & (ti >= si), same & (si >= ti))

    def scan_head(d, tile, slot=None):
        slot = d if slot is None else slot
        order = range(per_tile) if d == 0 else range(per_tile - 1, -1, -1)
        r0 = pl.multiple_of(tile * tp, tp)
        vv = v_scr[pl.ds(r0, tp), :]
        for c in range(per_tile):
            cr = pl.ds(pl.multiple_of(r0 + c * CHUNK, CHUNK), CHUNK)
            u_scr[slot, c] = _dot_tn(vv[c * CHUNK:(c + 1) * CHUNK, :], kl_scr[d][cr, :])
        ext = ext_scr[d, tile]
        decay = jnp.exp(ext[0:per_tile] + ext[per_tile:2 * per_tile])
        st = state_in(d)
        for c in order:
            sb_scr[slot, c] = st.astype(BF16)
            st = st * decay[c:c + 1] + u_scr[slot, c]
        state_out(d, tile, st)

    def scan_scores(d, tile):
        rows = pl.ds(pl.multiple_of(tile * tp, tp), tp)
        return jnp.where(masks[d], _dot_nt(qd_scr[d][rows, :], kd_scr[d][rows, :]), 0.0)

    def scan_out(d, tile, sc, slot=None):
        slot = d if slot is None else slot
        r0 = pl.multiple_of(tile * tp, tp)
        rows = pl.ds(r0, tp)
        o_scr[d][rows, :] = _dot(sc.astype(BF16), v_scr[rows, :])
        for c in range(per_tile):
            cr = pl.ds(pl.multiple_of(r0 + c * CHUNK, CHUNK), CHUNK)
            o_scr[d][cr, :] += _dot_nt(qb_scr[d][cr, :], sb_scr[slot, c])

    def scan_tail(d, tile, slot=None):
        scan_out(d, tile, scan_scores(d, tile), slot)

    def scan_tail_pair(a, b):
        sc = [scan_scores(d, tile) for d, tile, _ in (a, b)]
        for (d, tile, slot), s in zip((a, b), sc):
            scan_out(d, tile, s, slot)

    rid = lax.broadcasted_iota(jnp.int32, (BF16_ROWS, DK), 0)

    def scan_exact(d, tile):
        r0 = tile * tp

        def step(j, st):
            t = r0 + ((tp - 1 - j) if d == 1 else j)
            t0 = pl.multiple_of((t // BF16_ROWS) * BF16_ROWS, BF16_ROWS)
            grp = pl.ds(t0, BF16_ROWS)
            sel = rid == (t - t0)
            g = g_scr[d][grp, :]
            f = jnp.exp(jnp.sum(jnp.where(sel, g, 0.0), axis=0, keepdims=True))
            k = jnp.where(sel, 1.0 - jnp.exp(g), 0.0)
            st = st * f + _dot_tn(v_scr[grp, :], k.astype(BF16))
            o = _dot_nt(q_scr[grp, :].astype(BF16), st.astype(BF16))
            o_scr[d][grp, :] = jnp.where(sel, o, o_scr[d][grp, :])
            return st

        state_out(d, tile, lax.fori_loop(0, tp, step, state_in(d)))

    def finish(i):
        rows = pl.ds(pl.multiple_of(i * tp, tp), tp)
        o = o_scr[0][rows, :] + o_scr[1][rows, :]
        o = o * lax.rsqrt(jnp.mean(o * o, axis=-1, keepdims=True) + EPS) * gn_ref[...]
        o_ref[0, rows, :] = (o * sg_scr[rows, :]).astype(o_ref.dtype)

    dirs = (0, 1) if independent else (0,)

    def scan_heads(t):
        for d in dirs:
            scan_head(d, t)

    def scan_tails(t):
        if independent:
            scan_tail_pair((0, t, 0), (1, t, 1))
            finish(t)
        else:
            scan_tail(0, t)

    last = n_tiles - 1
    gates(0)
    if n_tiles > 1:
        sums = decay_sums()
        gates(1)
        factors(0, sums)

        def forward(i, carry):
            scan_heads(i - 1)
            sums = decay_sums()
            gates(i + 1)
            factors(i, sums)
            scan_tails(i - 1)
            return carry

        lax.fori_loop(1, last, forward, 0, unroll=SWEEP_UNROLL)
        scan_heads(last - 1)
        factors(last, decay_sums())
        scan_tails(last - 1)
    else:
        factors(0, decay_sums())

    if independent:
        scan_heads(last)
        scan_tails(last)
    else:
        scan_head(0, last)
        scan_head(1, last)
        scan_tail_pair((0, last, 0), (1, last, 1))
        finish(last)
        t0 = last - 1
        if t0 % 2 == 0:
            scan_head(1, t0, 0)
            scan_tail(1, t0, 0)
            finish(t0)
            t0 -= 1

        def backward(j, carry):
            t = t0 - 2 * j
            scan_head(1, t, 0)
            scan_head(1, t - 1, 1)
            scan_tail_pair((1, t, 0), (1, t - 1, 1))
            finish(t)
            finish(t - 1)
            return carry

        lax.fori_loop(0, (t0 + 1) // 2, backward, 0, unroll=BACK_UNROLL)

    unsafe = flag_scr[0]
    for t in range(1, n_tiles):
        unsafe = unsafe | flag_scr[t]

    @pl.when(unsafe != 0)
    def _():
        if not independent:
            init_states()

        def resweep(d):
            def body(j, carry):
                t = j if d == 0 else last - j

                @pl.when(flag_scr[t] != 0)
                def _():
                    scan_exact(d, t)

                @pl.when(flag_scr[t] == 0)
                def _():
                    scan_head(d, t)
                    scan_tail(d, t)

                if d == 1:
                    finish(t)
                return carry

            lax.fori_loop(0, n_tiles, body, 0)

        resweep(0)
        resweep(1)

    if not independent:
        for d in range(2):
            sf_ref[0, 0, d, 0] = st_scr[d].T


def _hgrn_mixer(hn, w_in, lb_raw, g_norm, s0):
    n_seq, seq_len, _ = hn.shape
    has_s0 = s0 is not None
    independent = (not has_s0) and seq_len == SCAN_TILE
    if independent:
        hn = hn.reshape(1, n_seq * seq_len, D)
    b, seq, _ = hn.shape
    st_rows = n_seq if independent else 1
    st_spec = pl.BlockSpec((st_rows, 1, 2, 1, DK, DK), lambda h, i: (i, 0, 0, h, 0, 0))
    in_specs = [pl.BlockSpec((1, seq, D), lambda h, i: (i, 0, 0))]
    in_specs += [pl.BlockSpec((D, DK), functools.partial(lambda j, h, i: (0, j * HEADS + h), j))
                 for j in range(PROJ)]
    in_specs += [pl.BlockSpec((2, 2, DK), lambda h, i: (0, 0, h)),
                 pl.BlockSpec((1, DK), lambda h, i: (0, 0)),
                 pl.BlockSpec((2, SCAN_TILE + BF16_ROWS, SCAN_TILE), lambda h, i: (0, 0, 0))]
    args = [hn] + [w_in] * PROJ + [lb_raw, g_norm, _decay_matrices()]
    if has_s0:
        in_specs.append(st_spec)
        args.append(s0)
    n_tiles = seq // SCAN_TILE
    bf16_rows = pltpu.VMEM((seq, DK), BF16)
    f32_rows = pltpu.VMEM((seq, DK), F32)
    o, s_fin = pl.pallas_call(
        functools.partial(_hgrn_kernel, has_s0=has_s0, independent=independent),
        grid=(HEADS, b),
        in_specs=in_specs,
        out_specs=[pl.BlockSpec((1, seq, DK), lambda h, i: (i, 0, h)), st_spec],
        out_shape=[jax.ShapeDtypeStruct((b, seq, D), BF16),
                   jax.ShapeDtypeStruct((n_seq, 1, 2, HEADS, DK, DK), F32)],
        scratch_shapes=[bf16_rows] * 8 + [f32_rows] * 6 + [
                        pltpu.VMEM((2, n_tiles, BF16_ROWS, DK), F32),
                        f32_rows, bf16_rows, f32_rows,
                        pltpu.VMEM((2, DK, DK), F32),
                        pltpu.VMEM((2, PER_TILE, DK, DK), F32),
                        pltpu.VMEM((2, PER_TILE, DK, DK), BF16),
                        pltpu.VMEM((2, SCAN_TILE, 2 * DK), BF16),
                        pltpu.VMEM((D, PROJ * DK), BF16),
                        pltpu.SMEM((n_tiles,), jnp.int32)],
        compiler_params=_params("parallel", "arbitrary"),
        name="hgrn_scan",
    )(*args)
    return o.reshape(n_seq, seq_len, D), s_fin


def kernel(x_prompt, x_sample, state_hgrn, c, c_ctx, ada_w, ada_b, norm_mix, norm_mlp, fnet_wo,
           hgrn_w_in, hgrn_lb, hgrn_norm, hgrn_wo, mlp_w1, mlp_w2, norm_final):
    bp, lp, _ = x_prompt.shape
    bs, ls, _ = x_sample.shape
    assert ada_w.shape[0] == 2 and ls == SEQ_PTS * SEQ_SLABS and lp == GD

    cs_ch, t_ctx, g_seq = _dft_tables()
    pos = _grid_pos_embed(ls, D)

    cond8 = jnp.zeros((8, D), F32).at[0].set(c_ctx).at[1:1 + bs].set(c)
    mod = _modulation(cond8, ada_w, ada_b)

    wo0 = fnet_wo.astype(BF16)
    wo1 = hgrn_wo.astype(BF16)
    w1 = mlp_w1.astype(BF16)
    w2 = mlp_w2.astype(BF16)
    w_heads = hgrn_w_in[0]
    gm = norm_mix.reshape(-1, 1, D)
    gf = norm_mlp.reshape(-1, 1, D)
    g_fin = norm_final.reshape(1, D)
    g_hn = hgrn_norm[0].reshape(1, DK)

    ctx_row = lambda r: 0
    smp_row = lambda r: 1 + r // ls

    mixed = _fnet_ctx(x_prompt, mod, 0, gm[0], cs_ch, t_ctx)
    x2, hn = _post_mixer(x_prompt.reshape(bp * lp, D), None, mixed.reshape(bp * lp, D), mod, 0,
                         ctx_row, wo0, w1, w2, gf[0], gm[1], final=False)
    o, new_state = _hgrn_mixer(hn.reshape(bp, lp, D), w_heads, hgrn_lb, g_hn, None)
    y_prompt = _post_mixer(x2, None, o.reshape(bp * lp, D), mod, 1, ctx_row, wo1, w1, w2,
                           gf[1], g_fin, final=True).reshape(bp, lp, D)

    mixed = _fnet_sample(x_sample, pos, mod, 0, gm[0], cs_ch, g_seq)
    x2, hn = _post_mixer(x_sample.reshape(bs * ls, D), pos, mixed.reshape(bs * ls, D), mod, 0,
                         smp_row, wo0, w1, w2, gf[0], gm[1], final=False)
    o, _ = _hgrn_mixer(hn.reshape(bs, ls, D), w_heads, hgrn_lb, g_hn, state_hgrn)
    y_sample = _post_mixer(x2, None, o.reshape(bs * ls, D), mod, 1, smp_row, wo1, w1, w2,
                           gf[1], g_fin, final=True).reshape(bs, ls, D)
    return (y_prompt, y_sample, new_state)
```

```python
import functools
import math

import numpy as np
import jax
import jax.numpy as jnp
from jax import lax
from jax.experimental import pallas as pl
from jax.experimental.pallas import tpu as pltpu

F32 = jnp.float32
BF16 = jnp.bfloat16

D = 1024
N_MOD = 6
D_FF = 4 * D
EPS = 1e-6
GROUPS = 4
GD = D // GROUPS
HEADS = 8
DK = 128
GRID_W = 64
POS_BASE = 10000.0
PROJ = 5

SEQ_SLABS = 16
SEQ_PTS = 256
LANES = 128
BF16_ROWS = 16
CHUNK = 64
SCAN_TILE = 256
EXP_SAFE = 85.0
SWEEP_UNROLL = 14
BACK_UNROLL = 7
SLAB_SUB = 512
POST_TM = 512
POST_SUB = 256
POST_TF = 1024

VMEM_LIMIT = 56 * 1024 * 1024


def _dft_tables():
    n = np.arange(GD)
    ang = 2.0 * np.pi * ((n[:, None] * n[None, :]) % GD) / GD
    cs_ch = np.concatenate([np.cos(ang), np.sin(ang)], axis=1) / math.sqrt(GD)
    t_ctx = np.concatenate([np.cos(ang), -np.sin(ang)], axis=1) / math.sqrt(GD)
    length = SEQ_PTS * SEQ_SLABS
    k1 = np.arange(SEQ_PTS)[:, None]
    n1 = np.arange(SEQ_PTS)[None, :]
    blocks = []
    for n2 in range(SEQ_SLABS):
        a = 2.0 * np.pi * ((k1 * (SEQ_SLABS * n1 + n2)) % length) / length
        gc, gs = np.cos(a) / math.sqrt(length), np.sin(a) / math.sqrt(length)
        blocks.append(np.block([[gc, -gs], [gs, gc]]))
    g_seq = np.stack(blocks)
    return tuple(jnp.asarray(t, F32).astype(BF16) for t in (cs_ch, t_ctx, g_seq))


def _grid_pos_embed(n_tok, d):
    rows = n_tok // GRID_W
    quarter = d // 4
    omega = 1.0 / (POS_BASE ** (jnp.arange(quarter, dtype=F32) / quarter))
    r = jnp.arange(rows, dtype=F32)[:, None] * omega[None, :]
    cl = jnp.arange(GRID_W, dtype=F32)[:, None] * omega[None, :]
    er = jnp.concatenate([jnp.sin(r), jnp.cos(r)], axis=-1)
    ec = jnp.concatenate([jnp.sin(cl), jnp.cos(cl)], axis=-1)
    emb = jnp.concatenate([jnp.broadcast_to(er[:, None, :], (rows, GRID_W, d // 2)),
                           jnp.broadcast_to(ec[None, :, :], (rows, GRID_W, d // 2))], axis=-1)
    return emb.reshape(n_tok, d)


def _silu(x):
    return x * jax.nn.sigmoid(x)


def _rms(x, g):
    return x * lax.rsqrt(jnp.mean(x * x, axis=-1, keepdims=True) + EPS) * g


def _mod_part(m, idx):
    return m[:, idx * D:(idx + 1) * D]


def _dot(a, b):
    return jnp.dot(a, b, preferred_element_type=F32)


def _dot_nt(a, b):
    return lax.dot_general(a, b, (((1,), (1,)), ((), ())), preferred_element_type=F32)


def _dot_tn(a, b):
    return lax.dot_general(a, b, (((0,), (0,)), ((), ())), preferred_element_type=F32)


def _params(*sem, **kw):
    return pltpu.CompilerParams(dimension_semantics=sem, vmem_limit_bytes=VMEM_LIMIT, **kw)


def _mod_kernel(c_ref, w_ref, b_ref, o_ref):
    s = _silu(c_ref[...])
    o_ref[0] = _dot(s.astype(BF16), w_ref[0].astype(BF16)) + b_ref[0]


def _modulation(cond8, ada_w, ada_b):
    depth = ada_w.shape[0]
    tn = 1536
    out = pl.pallas_call(
        _mod_kernel,
        grid=(depth, N_MOD * D // tn),
        in_specs=[pl.BlockSpec((8, D), lambda l, j: (0, 0)),
                  pl.BlockSpec((1, D, tn), lambda l, j: (l, 0, j)),
                  pl.BlockSpec((1, 1, tn), lambda l, j: (l, 0, j))],
        out_specs=pl.BlockSpec((1, 8, tn), lambda l, j: (l, 0, j)),
        out_shape=jax.ShapeDtypeStruct((depth, 8, N_MOD * D), F32),
        compiler_params=_params("parallel", "parallel"),
        name="adaln_mod",
    )(cond8, ada_w, ada_b.reshape(depth, 1, N_MOD * D))
    return out.reshape(depth, 8, 1, N_MOD * D)


def _channel_dft(hb, cs_ref, put):
    for g in range(GROUPS):
        y = _dot(hb[:, g * GD:(g + 1) * GD], cs_ref[...])
        put(0, g, y[:, :GD].astype(BF16))
        put(1, g, y[:, GD:].astype(BF16))


def _fnet_ctx_kernel(x_ref, mod_ref, g_ref, cs_ref, t_ref, o_ref, y_scr):
    m = mod_ref[0, 0]
    h = _rms(x_ref[0], g_ref[...]) * (1.0 + _mod_part(m, 1)) + _mod_part(m, 0)
    seq = x_ref.shape[1]

    def put(kind, g, val):
        y_scr[kind * seq:(kind + 1) * seq, g * GD:(g + 1) * GD] = val

    _channel_dft(h.astype(BF16), cs_ref, put)
    o_ref[0] = _dot(t_ref[...], y_scr[...]).astype(o_ref.dtype)


def _fnet_ctx(x, mod, layer, gain, cs_ch, t_ctx):
    b, seq, _ = x.shape
    return pl.pallas_call(
        _fnet_ctx_kernel,
        grid=(b,),
        in_specs=[pl.BlockSpec((1, seq, D), lambda i: (i, 0, 0)),
                  pl.BlockSpec((1, 1, 1, N_MOD * D), lambda i: (layer, 0, 0, 0)),
                  pl.BlockSpec((1, D), lambda i: (0, 0)),
                  pl.BlockSpec((GD, 2 * GD), lambda i: (0, 0)),
                  pl.BlockSpec((seq, 2 * seq), lambda i: (0, 0))],
        out_specs=pl.BlockSpec((1, seq, D), lambda i: (i, 0, 0)),
        out_shape=jax.ShapeDtypeStruct((b, seq, D), BF16),
        scratch_shapes=[pltpu.VMEM((2 * seq, D), BF16)],
        compiler_params=_params("parallel"),
        name="fnet_ctx",
    )(x, mod, gain, cs_ch, t_ctx)


def _fnet_slab_kernel(x_ref, pos_ref, mod_ref, g_ref, cs_ref, y_ref, col_scr, hb_scr):
    rows = SLAB_SUB // SEQ_SLABS
    n_col = D // LANES
    m = mod_ref[0, 0]
    for s in range(x_ref.shape[1] // SLAB_SUB):
        sub = slice(s * SLAB_SUB, (s + 1) * SLAB_SUB)
        for j in range(n_col):
            cols = slice(j * LANES, (j + 1) * LANES)
            col_scr[s, j] = x_ref[0, sub, cols] + pos_ref[sub, cols]
        for n2 in range(SEQ_SLABS):
            xs = jnp.concatenate([col_scr[s, j, pl.ds(n2, rows, stride=SEQ_SLABS), :]
                                  for j in range(n_col)], axis=1)
            h = _rms(xs, g_ref[...]) * (1.0 + _mod_part(m, 1)) + _mod_part(m, 0)
            hb_scr[s, n2 * rows:(n2 + 1) * rows, :] = h.astype(BF16)
        dst = slice(s * rows, (s + 1) * rows)
        for g in range(GROUPS):
            y = _dot(hb_scr[s, :, g * GD:(g + 1) * GD], cs_ref[...])
            for n2 in range(SEQ_SLABS):
                src = slice(n2 * rows, (n2 + 1) * rows)
                y_ref[0, n2, 0, dst, g * GD:(g + 1) * GD] = y[src, :GD].astype(BF16)
                y_ref[0, n2, 1, dst, g * GD:(g + 1) * GD] = y[src, GD:].astype(BF16)


def _fft_real_part(xs):
    n = len(xs)
    if n == 1:
        return xs
    ev = _fft_real_part(xs[0::2])
    od = _fft_real_part(xs[1::2])
    out = [None] * n
    for k in range(n // 2):
        c = math.cos(2.0 * math.pi * k / n)
        s = -math.sin(2.0 * math.pi * k / n)
        orr, oi = od[k]
        if k == 0:
            tr, ti = orr, oi
        elif 4 * k == n:
            tr, ti = oi, -orr
        else:
            tr, ti = orr * c - oi * s, orr * s + oi * c
        er, ei = ev[k]
        out[k] = (er + tr, ei + ti)
        out[k + n // 2] = (er - tr, ei - ti)
    return out


def _fnet_seq_kernel(y_ref, g_ref, o_ref, z_scr):
    tc = o_ref.shape[2]
    for n2 in range(SEQ_SLABS):
        yb = y_ref[0, n2].reshape(2 * SEQ_PTS, tc)
        z_scr[n2] = _dot(g_ref[n2], yb)

    rb = 8

    def body(r, carry):
        r0 = pl.multiple_of(r * rb, rb)
        for c in range(tc // LANES):
            cols = slice(c * LANES, (c + 1) * LANES)
            xs = []
            for n2 in range(SEQ_SLABS):
                re = z_scr[n2, pl.ds(r0, rb), cols]
                nim = z_scr[n2, pl.ds(SEQ_PTS + r0, rb), cols]
                xs.append((re, -nim))
            out = _fft_real_part(xs)
            for k2 in range(SEQ_SLABS):
                o_ref[0, pl.ds(k2 * SEQ_PTS + r0, rb), cols] = out[k2][0]
        return carry

    lax.fori_loop(0, SEQ_PTS // rb, body, 0)


def _fnet_sample(x, pos, mod, layer, gain, cs_ch, g_seq):
    b, seq, _ = x.shape
    tm = 2 * SLAB_SUB
    tc = 256
    y = pl.pallas_call(
        _fnet_slab_kernel,
        grid=(seq // tm, b),
        in_specs=[pl.BlockSpec((1, tm, D), lambda j, i: (i, j, 0)),
                  pl.BlockSpec((tm, D), lambda j, i: (j, 0)),
                  pl.BlockSpec((1, 1, 1, N_MOD * D), lambda j, i: (layer, 1 + i, 0, 0)),
                  pl.BlockSpec((1, D), lambda j, i: (0, 0)),
                  pl.BlockSpec((GD, 2 * GD), lambda j, i: (0, 0))],
        out_specs=pl.BlockSpec((1, SEQ_SLABS, 2, tm // SEQ_SLABS, D), lambda j, i: (i, 0, 0, j, 0)),
        out_shape=jax.ShapeDtypeStruct((b, SEQ_SLABS, 2, SEQ_PTS, D), BF16),
        scratch_shapes=[pltpu.VMEM((tm // SLAB_SUB, D // LANES, SLAB_SUB, LANES), F32),
                        pltpu.VMEM((tm // SLAB_SUB, SLAB_SUB, D), BF16)],
        compiler_params=_params("parallel", "parallel"),
        name="fnet_slab",
    )(x, pos, mod, gain, cs_ch)
    return pl.pallas_call(
        _fnet_seq_kernel,
        grid=(b, D // tc),
        in_specs=[pl.BlockSpec((1, SEQ_SLABS, 2, SEQ_PTS, tc), lambda i, j: (i, 0, 0, 0, j)),
                  pl.BlockSpec((SEQ_SLABS, 2 * SEQ_PTS, 2 * SEQ_PTS), lambda i, j: (0, 0, 0))],
        out_specs=pl.BlockSpec((1, seq, tc), lambda i, j: (i, 0, j)),
        out_shape=jax.ShapeDtypeStruct((b, seq, D), F32),
        scratch_shapes=[pltpu.VMEM((SEQ_SLABS, 2 * SEQ_PTS, tc), F32)],
        compiler_params=_params("parallel", "parallel"),
        name="fnet_seq",
    )(y, g_seq)


def _post_kernel(*refs, has_pos, final):
    it = iter(refs)
    x_ref = next(it)
    pos_ref = next(it) if has_pos else None
    a_ref, mod_ref, wo_ref, w1_ref, w2_ref, gmlp_ref, gnext_ref = (next(it) for _ in range(7))
    modn_ref = None if final else next(it)
    o_ref = next(it)
    hn_ref = None if final else next(it)
    x1_scr, h_scr, u_scr = (next(it) for _ in range(3))

    m = mod_ref[0, 0]
    subs = [slice(s * POST_SUB, (s + 1) * POST_SUB) for s in range(x_ref.shape[0] // POST_SUB)]

    for r in subs:
        x = x_ref[r, :]
        if has_pos:
            x = x + pos_ref[r, :]
        x1 = x + _mod_part(m, 2) * _dot(a_ref[r, :].astype(BF16), wo_ref[0])
        x1_scr[r, :] = x1
        h = _rms(x1, gmlp_ref[...]) * (1.0 + _mod_part(m, 4)) + _mod_part(m, 3)
        h_scr[r, :] = h.astype(BF16)

    for s, r in enumerate(subs):
        for j in range(D_FF // POST_TF):
            cols = slice(j * POST_TF, (j + 1) * POST_TF)
            u = jnp.maximum(_dot(h_scr[r, :], w1_ref[0, :, cols]), 0.0)
            u_scr[s, :, cols] = (u * u).astype(BF16)
        x2 = x1_scr[r, :] + _mod_part(m, 5) * _dot(u_scr[s], w2_ref[0])
        if final:
            o_ref[r, :] = _rms(x2, gnext_ref[...])
        else:
            o_ref[r, :] = x2
            mn = modn_ref[0, 0]
            hn = _rms(x2, gnext_ref[...]) * (1.0 + _mod_part(mn, 1)) + _mod_part(mn, 0)
            hn_ref[r, :] = hn.astype(BF16)


def _post_mixer(x, pos, a, mod, layer, mod_row, wo, w1, w2, g_mlp, g_next, final):
    rows = x.shape[0]
    tm = 2 * POST_TM if final else POST_TM
    has_pos = pos is not None
    n_pos = pos.shape[0] // tm if has_pos else 1
    row_map = lambda i: (i, 0)
    const = lambda i: (0, 0)
    resident = lambda shape: pl.BlockSpec(shape, lambda i: (layer, 0, 0), pipeline_mode=pl.Buffered(1))
    in_specs = [pl.BlockSpec((tm, D), row_map)]
    args = [x]
    if has_pos:
        in_specs.append(pl.BlockSpec((tm, D), lambda i: (i % n_pos, 0)))
        args.append(pos)
    in_specs += [pl.BlockSpec((tm, D), row_map),
                 pl.BlockSpec((1, 1, 1, N_MOD * D), lambda i: (layer, mod_row(i * tm), 0, 0)),
                 pl.BlockSpec((1, D, D), lambda i: (0, 0, 0), pipeline_mode=pl.Buffered(1)),
                 resident((1, D, D_FF)),
                 resident((1, D_FF, D)),
                 pl.BlockSpec((1, D), const),
                 pl.BlockSpec((1, D), const)]
    args += [a, mod, wo, w1, w2, g_mlp, g_next]
    out_specs = [pl.BlockSpec((tm, D), row_map)]
    out_shape = [jax.ShapeDtypeStruct((rows, D), F32)]
    if not final:
        in_specs.append(pl.BlockSpec((1, 1, 1, N_MOD * D),
                                     lambda i: (layer + 1, mod_row(i * tm), 0, 0)))
        args.append(mod)
        out_specs.append(pl.BlockSpec((tm, D), row_map))
        out_shape.append(jax.ShapeDtypeStruct((rows, D), BF16))
    res = pl.pallas_call(
        functools.partial(_post_kernel, has_pos=has_pos, final=final),
        grid=(rows // tm,),
        in_specs=in_specs,
        out_specs=out_specs,
        out_shape=out_shape,
        scratch_shapes=[pltpu.VMEM((tm, D), F32), pltpu.VMEM((tm, D), BF16),
                        pltpu.VMEM((tm // POST_SUB, POST_SUB, D_FF), BF16)],
        compiler_params=_params("parallel"),
        name="post_mlp_final" if final else "post_mlp",
    )(*args)
    return res[0] if final else (res[0], res[1])


PER_TILE = SCAN_TILE // CHUNK


def _decay_matrices():
    t = np.arange(SCAN_TILE)
    start = (t // CHUNK) * CHUNK
    first = np.arange(PER_TILE) * CHUNK
    same = (t[:, None] // CHUNK) == (t[None, :] // CHUNK)
    out = []
    for incl, mid, last in ((t[None, :] <= t[:, None], CHUNK // 2 - 1, CHUNK - 1),
                            (t[None, :] >= t[:, None], CHUNK // 2, 0)):
        bd = (same & incl).astype(np.float32)
        ref = bd[first + mid]
        pad = np.zeros((BF16_ROWS - 2 * PER_TILE, SCAN_TILE), np.float32)
        out.append(np.concatenate([bd - bd[start + mid], ref, bd[first + last] - ref, pad], axis=0))
    return jnp.asarray(np.stack(out), BF16)


def _hgrn_kernel(*refs, has_s0, independent):
    it = iter(refs)
    hn_ref = next(it)
    wp_refs = [next(it) for _ in range(PROJ)]
    lb_ref, gn_ref, dm_ref = (next(it) for _ in range(3))
    s0_ref = next(it) if has_s0 else None
    o_ref, sf_ref = next(it), next(it)
    qd_scr, kd_scr, qb_scr, kl_scr, g_scr, o_scr, k_scr = ((next(it), next(it)) for _ in range(7))
    ext_scr, q_scr, v_scr, sg_scr, st_scr, u_scr, sb_scr, gs_scr, w_scr, flag_scr = (
        next(it) for _ in range(10))

    @pl.when(pl.program_id(1) == 0)
    def _():
        for j, wp_ref in enumerate(wp_refs):
            w_scr[:, j * DK:(j + 1) * DK] = wp_ref[...].astype(BF16)

    seq = hn_ref.shape[1]
    tp = SCAN_TILE
    n_tiles = seq // tp
    per_tile = PER_TILE

    lraw = lb_ref[...]
    mx = jnp.max(lraw, axis=1, keepdims=True)
    ex = jnp.exp(lraw - mx)
    sm = ex / jnp.sum(ex, axis=1, keepdims=True)
    lbv = (sm[:, 0, :] + sm[:, 1, :]) - sm[:, 0, :]

    def gates(i):
        rows = pl.ds(pl.multiple_of(i * tp, tp), tp)
        p = _dot(hn_ref[0, rows, :], w_scr[...])
        q_scr[rows, :] = _silu(p[:, 0:DK])
        v_scr[rows, :] = p[:, 3 * DK:4 * DK].astype(BF16)
        sg_scr[rows, :] = _silu(p[:, 4 * DK:5 * DK])
        for d in range(2):
            lb = lbv[d:d + 1, :]
            f = lb + (1.0 - lb) * jax.nn.sigmoid(p[:, (1 + d) * DK:(2 + d) * DK])
            k_scr[d][rows, :] = 1.0 - f
            g = jnp.log(f)
            g_scr[d][rows, :] = g
            g1 = g.astype(BF16)
            gs_scr[d, :, 0:DK] = g1
            gs_scr[d, :, DK:2 * DK] = (g - g1.astype(F32)).astype(BF16)

    def decay_sums():
        return [_dot(dm_ref[d], gs_scr[d]) for d in range(2)]

    def factors(i, sums):
        rows = pl.ds(pl.multiple_of(i * tp, tp), tp)
        q = q_scr[rows, :]
        unsafe = jnp.int32(0)
        for d in range(2):
            r = sums[d][:, :DK] + sums[d][:, DK:]
            a = r[:tp]
            ext = r[tp:]
            ext_scr[d, i] = ext
            unsafe = unsafe | (jnp.max(jnp.abs(a)) > EXP_SAFE).astype(jnp.int32)
            qd = q * jnp.exp(a)
            kd = k_scr[d][rows, :] * jnp.exp(-a)
            qd_scr[d][rows, :] = qd.astype(BF16)
            kd_scr[d][rows, :] = kd.astype(BF16)
            e_ref = jnp.exp(ext[0:per_tile])
            e_lr = jnp.exp(ext[per_tile:2 * per_tile])
            for c in range(per_tile):
                cr = pl.ds(pl.multiple_of(i * tp + c * CHUNK, CHUNK), CHUNK)
                cs = slice(c * CHUNK, (c + 1) * CHUNK)
                qb_scr[d][cr, :] = (qd[cs] * e_ref[c:c + 1]).astype(BF16)
                kl_scr[d][cr, :] = (kd[cs] * e_lr[c:c + 1]).astype(BF16)
        flag_scr[i] = unsafe

    def init_states():
        for d in range(2):
            if has_s0:
                st_scr[d] = s0_ref[0, 0, d, 0].T
            else:
                st_scr[d] = jnp.zeros((DK, DK), F32)

    def state_in(d):
        return jnp.zeros((DK, DK), F32) if independent else st_scr[d]

    def state_out(d, tile, st):
        if independent:
            sf_ref[tile, 0, d, 0] = st.T
        else:
            st_scr[d] = st

    if not independent:
        init_states()

    ti = lax.broadcasted_iota(jnp.int32, (tp, tp), 0)
    si = lax.broadcasted_iota(jnp.int32, (tp, tp), 1)
    same = (ti // CHUNK) == (si // CHUNK)
    masks = (same & (ti >= si), same & (si >= ti))

    def scan_head(d, tile, slot=None):
        slot = d if slot is None else slot
        order = range(per_tile) if d == 0 else range(per_tile - 1, -1, -1)
        r0 = pl.multiple_of(tile * tp, tp)
        vv = v_scr[pl.ds(r0, tp), :]
        for c in range(per_tile):
            cr = pl.ds(pl.multiple_of(r0 + c * CHUNK, CHUNK), CHUNK)
            u_scr[slot, c] = _dot_tn(vv[c * CHUNK:(c + 1) * CHUNK, :], kl_scr[d][cr, :])
        ext = ext_scr[d, tile]
        decay = jnp.exp(ext[0:per_tile] + ext[per_tile:2 * per_tile])
        st = state_in(d)
        for c in order:
            sb_scr[slot, c] = st.astype(BF16)
            st = st * decay[c:c + 1] + u_scr[slot, c]
        state_out(d, tile, st)

    def scan_scores(d, tile):
        rows = pl.ds(pl.multiple_of(tile * tp, tp), tp)
        return jnp.where(masks[d], _dot_nt(qd_scr[d][rows, :], kd_scr[d][rows, :]), 0.0)

    def scan_out(d, tile, sc, slot=None):
        slot = d if slot is None else slot
        r0 = pl.multiple_of(tile * tp, tp)
        rows = pl.ds(r0, tp)
        o_scr[d][rows, :] = _dot(sc.astype(BF16), v_scr[rows, :])
        for c in range(per_tile):
            cr = pl.ds(pl.multiple_of(r0 + c * CHUNK, CHUNK), CHUNK)
            o_scr[d][cr, :] += _dot_nt(qb_scr[d][cr, :], sb_scr[slot, c])

    def scan_tail(d, tile, slot=None):
        scan_out(d, tile, scan_scores(d, tile), slot)

    def scan_tail_pair(a, b):
        sc = [scan_scores(d, tile) for d, tile, _ in (a, b)]
        for (d, tile, slot), s in zip((a, b), sc):
            scan_out(d, tile, s, slot)

    rid = lax.broadcasted_iota(jnp.int32, (BF16_ROWS, DK), 0)

    def scan_exact(d, tile):
        r0 = tile * tp

        def step(j, st):
            t = r0 + ((tp - 1 - j) if d == 1 else j)
            t0 = pl.multiple_of((t // BF16_ROWS) * BF16_ROWS, BF16_ROWS)
            grp = pl.ds(t0, BF16_ROWS)
            sel = rid == (t - t0)
            g = g_scr[d][grp, :]
            f = jnp.exp(jnp.sum(jnp.where(sel, g, 0.0), axis=0, keepdims=True))
            k = jnp.where(sel, 1.0 - jnp.exp(g), 0.0)
            st = st * f + _dot_tn(v_scr[grp, :], k.astype(BF16))
            o = _dot_nt(q_scr[grp, :].astype(BF16), st.astype(BF16))
            o_scr[d][grp, :] = jnp.where(sel, o, o_scr[d][grp, :])
            return st

        state_out(d, tile, lax.fori_loop(0, tp, step, state_in(d)))

    def finish(i):
        rows = pl.ds(pl.multiple_of(i * tp, tp), tp)
        o = o_scr[0][rows, :] + o_scr[1][rows, :]
        o = o * lax.rsqrt(jnp.mean(o * o, axis=-1, keepdims=True) + EPS) * gn_ref[...]
        o_ref[0, rows, :] = (o * sg_scr[rows, :]).astype(o_ref.dtype)

    dirs = (0, 1) if independent else (0,)

    def scan_heads(t):
        for d in dirs:
            scan_head(d, t)

    def scan_tails(t):
        if independent:
            scan_tail_pair((0, t, 0), (1, t, 1))
            finish(t)
        else:
            scan_tail(0, t)

    last = n_tiles - 1
    gates(0)
    if n_tiles > 1:
        sums = decay_sums()
        gates(1)
        factors(0, sums)

        def forward(i, carry):
            scan_heads(i - 1)
            sums = decay_sums()
            gates(i + 1)
            factors(i, sums)
            scan_tails(i - 1)
            return carry

        lax.fori_loop(1, last, forward, 0, unroll=SWEEP_UNROLL)
        scan_heads(last - 1)
        factors(last, decay_sums())
        scan_tails(last - 1)
    else:
        factors(0, decay_sums())

    if independent:
        scan_heads(last)
        scan_tails(last)
    else:
        scan_head(0, last)
        scan_head(1, last)
        scan_tail_pair((0, last, 0), (1, last, 1))
        finish(last)
        t0 = last - 1
        if t0 % 2 == 0:
            scan_head(1, t0, 0)
            scan_tail(1, t0, 0)
            finish(t0)
            t0 -= 1

        def backward(j, carry):
            t = t0 - 2 * j
            scan_head(1, t, 0)
            scan_head(1, t - 1, 1)
            scan_tail_pair((1, t, 0), (1, t - 1, 1))
            finish(t)
            finish(t - 1)
            return carry

        lax.fori_loop(0, (t0 + 1) // 2, backward, 0, unroll=BACK_UNROLL)

    unsafe = flag_scr[0]
    for t in range(1, n_tiles):
        unsafe = unsafe | flag_scr[t]

    @pl.when(unsafe != 0)
    def _():
        if not independent:
            init_states()

        def resweep(d):
            def body(j, carry):
                t = j if d == 0 else last - j

                @pl.when(flag_scr[t] != 0)
                def _():
                    scan_exact(d, t)

                @pl.when(flag_scr[t] == 0)
                def _():
                    scan_head(d, t)
                    scan_tail(d, t)

                if d == 1:
                    finish(t)
                return carry

            lax.fori_loop(0, n_tiles, body, 0)

        resweep(0)
        resweep(1)

    if not independent:
        for d in range(2):
            sf_ref[0, 0, d, 0] = st_scr[d].T


def _hgrn_mixer(hn, w_in, lb_raw, g_norm, s0):
    n_seq, seq_len, _ = hn.shape
    has_s0 = s0 is not None
    independent = (not has_s0) and seq_len == SCAN_TILE
    if independent:
        hn = hn.reshape(1, n_seq * seq_len, D)
    b, seq, _ = hn.shape
    st_rows = n_seq if independent else 1
    st_spec = pl.BlockSpec((st_rows, 1, 2, 1, DK, DK), lambda h, i: (i, 0, 0, h, 0, 0))
    in_specs = [pl.BlockSpec((1, seq, D), lambda h, i: (i, 0, 0))]
    in_specs += [pl.BlockSpec((D, DK), functools.partial(lambda j, h, i: (0, j * HEADS + h), j))
                 for j in range(PROJ)]
    in_specs += [pl.BlockSpec((2, 2, DK), lambda h, i: (0, 0, h)),
                 pl.BlockSpec((1, DK), lambda h, i: (0, 0)),
                 pl.BlockSpec((2, SCAN_TILE + BF16_ROWS, SCAN_TILE), lambda h, i: (0, 0, 0))]
    args = [hn] + [w_in] * PROJ + [lb_raw, g_norm, _decay_matrices()]
    if has_s0:
        in_specs.append(st_spec)
        args.append(s0)
    n_tiles = seq // SCAN_TILE
    bf16_rows = pltpu.VMEM((seq, DK), BF16)
    f32_rows = pltpu.VMEM((seq, DK), F32)
    o, s_fin = pl.pallas_call(
        functools.partial(_hgrn_kernel, has_s0=has_s0, independent=independent),
        grid=(HEADS, b),
        in_specs=in_specs,
        out_specs=[pl.BlockSpec((1, seq, DK), lambda h, i: (i, 0, h)), st_spec],
        out_shape=[jax.ShapeDtypeStruct((b, seq, D), BF16),
                   jax.ShapeDtypeStruct((n_seq, 1, 2, HEADS, DK, DK), F32)],
        scratch_shapes=[bf16_rows] * 8 + [f32_rows] * 6 + [
                        pltpu.VMEM((2, n_tiles, BF16_ROWS, DK), F32),
                        f32_rows, bf16_rows, f32_rows,
                        pltpu.VMEM((2, DK, DK), F32),
                        pltpu.VMEM((2, PER_TILE, DK, DK), F32),
                        pltpu.VMEM((2, PER_TILE, DK, DK), BF16),
                        pltpu.VMEM((2, SCAN_TILE, 2 * DK), BF16),
                        pltpu.VMEM((D, PROJ * DK), BF16),
                        pltpu.SMEM((n_tiles,), jnp.int32)],
        compiler_params=_params("parallel", "arbitrary"),
        name="hgrn_scan",
    )(*args)
    return o.reshape(n_seq, seq_len, D), s_fin


def kernel(x_prompt, x_sample, state_hgrn, c, c_ctx, ada_w, ada_b, norm_mix, norm_mlp, fnet_wo,
           hgrn_w_in, hgrn_lb, hgrn_norm, hgrn_wo, mlp_w1, mlp_w2, norm_final):
    bp, lp, _ = x_prompt.shape
    bs, ls, _ = x_sample.shape
    assert ada_w.shape[0] == 2 and ls == SEQ_PTS * SEQ_SLABS and lp == GD

    cs_ch, t_ctx, g_seq = _dft_tables()
    pos = _grid_pos_embed(ls, D)

    cond8 = jnp.zeros((8, D), F32).at[0].set(c_ctx).at[1:1 + bs].set(c)
    mod = _modulation(cond8, ada_w, ada_b)

    wo0 = fnet_wo.astype(BF16)
    wo1 = hgrn_wo.astype(BF16)
    w1 = mlp_w1.astype(BF16)
    w2 = mlp_w2.astype(BF16)
    w_heads = hgrn_w_in[0]
    gm = norm_mix.reshape(-1, 1, D)
    gf = norm_mlp.reshape(-1, 1, D)
    g_fin = norm_final.reshape(1, D)
    g_hn = hgrn_norm[0].reshape(1, DK)

    ctx_row = lambda r: 0
    smp_row = lambda r: 1 + r // ls

    mixed = _fnet_ctx(x_prompt, mod, 0, gm[0], cs_ch, t_ctx)
    x2, hn = _post_mixer(x_prompt.reshape(bp * lp, D), None, mixed.reshape(bp * lp, D), mod, 0,
                         ctx_row, wo0, w1, w2, gf[0], gm[1], final=False)
    o, new_state = _hgrn_mixer(hn.reshape(bp, lp, D), w_heads, hgrn_lb, g_hn, None)
    y_prompt = _post_mixer(x2, None, o.reshape(bp * lp, D), mod, 1, ctx_row, wo1, w1, w2,
                           gf[1], g_fin, final=True).reshape(bp, lp, D)

    mixed = _fnet_sample(x_sample, pos, mod, 0, gm[0], cs_ch, g_seq)
    x2, hn = _post_mixer(x_sample.reshape(bs * ls, D), pos, mixed.reshape(bs * ls, D), mod, 0,
                         smp_row, wo0, w1, w2, gf[0], gm[1], final=False)
    o, _ = _hgrn_mixer(hn.reshape(bs, ls, D), w_heads, hgrn_lb, g_hn, state_hgrn)
    y_sample = _post_mixer(x2, None, o.reshape(bs * ls, D), mod, 1, smp_row, wo1, w1, w2,
                           gf[1], g_fin, final=True).reshape(bs, ls, D)
    return (y_prompt, y_sample, new_state)
```

```python
import functools
import math

import numpy as np
import jax
import jax.numpy as jnp
from jax import lax
from jax.experimental import pallas as pl
from jax.experimental.pallas import tpu as pltpu

F32 = jnp.float32
BF16 = jnp.bfloat16

D = 1024
N_MOD = 6
D_FF = 4 * D
EPS = 1e-6
GROUPS = 4
GD = D // GROUPS
HEADS = 8
DK = 128
GRID_W = 64
POS_BASE = 10000.0
PROJ = 5

SEQ_SLABS = 16
SEQ_PTS = 256
LANES = 128
BF16_ROWS = 16
CHUNK = 64
SCAN_TILE = 256
EXP_SAFE = 85.0
SWEEP_UNROLL = 14
BACK_UNROLL = 7
SLAB_SUB = 512
POST_TM = 512
POST_SUB = 256
POST_TF = 1024

VMEM_LIMIT = 56 * 1024 * 1024


def _dft_tables():
    n = np.arange(GD)
    ang = 2.0 * np.pi * ((n[:, None] * n[None, :]) % GD) / GD
    cs_ch = np.concatenate([np.cos(ang), np.sin(ang)], axis=1) / math.sqrt(GD)
    t_ctx = np.concatenate([np.cos(ang), -np.sin(ang)], axis=1) / math.sqrt(GD)
    length = SEQ_PTS * SEQ_SLABS
    k1 = np.arange(SEQ_PTS)[:, None]
    n1 = np.arange(SEQ_PTS)[None, :]
    blocks = []
    for n2 in range(SEQ_SLABS):
        a = 2.0 * np.pi * ((k1 * (SEQ_SLABS * n1 + n2)) % length) / length
        gc, gs = np.cos(a) / math.sqrt(length), np.sin(a) / math.sqrt(length)
        blocks.append(np.block([[gc, -gs], [gs, gc]]))
    g_seq = np.stack(blocks)
    return tuple(jnp.asarray(t, F32).astype(BF16) for t in (cs_ch, t_ctx, g_seq))


def _grid_pos_embed(n_tok, d):
    rows = n_tok // GRID_W
    quarter = d // 4
    omega = 1.0 / (POS_BASE ** (jnp.arange(quarter, dtype=F32) / quarter))
    r = jnp.arange(rows, dtype=F32)[:, None] * omega[None, :]
    cl = jnp.arange(GRID_W, dtype=F32)[:, None] * omega[None, :]
    er = jnp.concatenate([jnp.sin(r), jnp.cos(r)], axis=-1)
    ec = jnp.concatenate([jnp.sin(cl), jnp.cos(cl)], axis=-1)
    emb = jnp.concatenate([jnp.broadcast_to(er[:, None, :], (rows, GRID_W, d // 2)),
                           jnp.broadcast_to(ec[None, :, :], (rows, GRID_W, d // 2))], axis=-1)
    return emb.reshape(n_tok, d)


def _silu(x):
    return x * jax.nn.sigmoid(x)


def _rms(x, g):
    return x * lax.rsqrt(jnp.mean(x * x, axis=-1, keepdims=True) + EPS) * g


def _mod_part(m, idx):
    return m[:, idx * D:(idx + 1) * D]


def _dot(a, b):
    return jnp.dot(a, b, preferred_element_type=F32)


def _dot_nt(a, b):
    return lax.dot_general(a, b, (((1,), (1,)), ((), ())), preferred_element_type=F32)


def _dot_tn(a, b):
    return lax.dot_general(a, b, (((0,), (0,)), ((), ())), preferred_element_type=F32)


def _params(*sem, **kw):
    return pltpu.CompilerParams(dimension_semantics=sem, vmem_limit_bytes=VMEM_LIMIT, **kw)


def _mod_kernel(c_ref, w_ref, b_ref, o_ref):
    s = _silu(c_ref[...])
    o_ref[0] = _dot(s.astype(BF16), w_ref[0].astype(BF16)) + b_ref[0]


def _modulation(cond8, ada_w, ada_b):
    depth = ada_w.shape[0]
    tn = 1536
    out = pl.pallas_call(
        _mod_kernel,
        grid=(depth, N_MOD * D // tn),
        in_specs=[pl.BlockSpec((8, D), lambda l, j: (0, 0)),
                  pl.BlockSpec((1, D, tn), lambda l, j: (l, 0, j)),
                  pl.BlockSpec((1, 1, tn), lambda l, j: (l, 0, j))],
        out_specs=pl.BlockSpec((1, 8, tn), lambda l, j: (l, 0, j)),
        out_shape=jax.ShapeDtypeStruct((depth, 8, N_MOD * D), F32),
        compiler_params=_params("parallel", "parallel"),
        name="adaln_mod",
    )(cond8, ada_w, ada_b.reshape(depth, 1, N_MOD * D))
    return out.reshape(depth, 8, 1, N_MOD * D)


def _channel_dft(hb, cs_ref, put):
    for g in range(GROUPS):
        y = _dot(hb[:, g * GD:(g + 1) * GD], cs_ref[...])
        put(0, g, y[:, :GD].astype(BF16))
        put(1, g, y[:, GD:].astype(BF16))


def _fnet_ctx_kernel(x_ref, mod_ref, g_ref, cs_ref, t_ref, o_ref, y_scr):
    m = mod_ref[0, 0]
    h = _rms(x_ref[0], g_ref[...]) * (1.0 + _mod_part(m, 1)) + _mod_part(m, 0)
    seq = x_ref.shape[1]

    def put(kind, g, val):
        y_scr[kind * seq:(kind + 1) * seq, g * GD:(g + 1) * GD] = val

    _channel_dft(h.astype(BF16), cs_ref, put)
    o_ref[0] = _dot(t_ref[...], y_scr[...]).astype(o_ref.dtype)


def _fnet_ctx(x, mod, layer, gain, cs_ch, t_ctx):
    b, seq, _ = x.shape
    return pl.pallas_call(
        _fnet_ctx_kernel,
        grid=(b,),
        in_specs=[pl.BlockSpec((1, seq, D), lambda i: (i, 0, 0)),
                  pl.BlockSpec((1, 1, 1, N_MOD * D), lambda i: (layer, 0, 0, 0)),
                  pl.BlockSpec((1, D), lambda i: (0, 0)),
                  pl.BlockSpec((GD, 2 * GD), lambda i: (0, 0)),
                  pl.BlockSpec((seq, 2 * seq), lambda i: (0, 0))],
        out_specs=pl.BlockSpec((1, seq, D), lambda i: (i, 0, 0)),
        out_shape=jax.ShapeDtypeStruct((b, seq, D), BF16),
        scratch_shapes=[pltpu.VMEM((2 * seq, D), BF16)],
        compiler_params=_params("parallel"),
        name="fnet_ctx",
    )(x, mod, gain, cs_ch, t_ctx)


def _fnet_slab_kernel(x_ref, pos_ref, mod_ref, g_ref, cs_ref, y_ref, col_scr, hb_scr):
    rows = SLAB_SUB // SEQ_SLABS
    n_col = D // LANES
    m = mod_ref[0, 0]
    for s in range(x_ref.shape[1] // SLAB_SUB):
        sub = slice(s * SLAB_SUB, (s + 1) * SLAB_SUB)
        for j in range(n_col):
            cols = slice(j * LANES, (j + 1) * LANES)
            col_scr[s, j] = x_ref[0, sub, cols] + pos_ref[sub, cols]
        for n2 in range(SEQ_SLABS):
            xs = jnp.concatenate([col_scr[s, j, pl.ds(n2, rows, stride=SEQ_SLABS), :]
                                  for j in range(n_col)], axis=1)
            h = _rms(xs, g_ref[...]) * (1.0 + _mod_part(m, 1)) + _mod_part(m, 0)
            hb_scr[s, n2 * rows:(n2 + 1) * rows, :] = h.astype(BF16)
        dst = slice(s * rows, (s + 1) * rows)
        for g in range(GROUPS):
            y = _dot(hb_scr[s, :, g * GD:(g + 1) * GD], cs_ref[...])
            for n2 in range(SEQ_SLABS):
                src = slice(n2 * rows, (n2 + 1) * rows)
                y_ref[0, n2, 0, dst, g * GD:(g + 1) * GD] = y[src, :GD].astype(BF16)
                y_ref[0, n2, 1, dst, g * GD:(g + 1) * GD] = y[src, GD:].astype(BF16)


def _fft_real_part(xs):
    n = len(xs)
    if n == 1:
        return xs
    ev = _fft_real_part(xs[0::2])
    od = _fft_real_part(xs[1::2])
    out = [None] * n
    for k in range(n // 2):
        c = math.cos(2.0 * math.pi * k / n)
        s = -math.sin(2.0 * math.pi * k / n)
        orr, oi = od[k]
        if k == 0:
            tr, ti = orr, oi
        elif 4 * k == n:
            tr, ti = oi, -orr
        else:
            tr, ti = orr * c - oi * s, orr * s + oi * c
        er, ei = ev[k]
        out[k] = (er + tr, ei + ti)
        out[k + n // 2] = (er - tr, ei - ti)
    return out


def _fnet_seq_kernel(y_ref, g_ref, o_ref, *z_scr):
    tc = o_ref.shape[2]
    s = pl.program_id(0)
    last = pl.num_programs(0) - 1
    rb = 8
    chunks = [(r0, c0) for r0 in range(0, SEQ_PTS, rb) for c0 in range(0, tc, LANES)]
    per_dot = len(chunks) // SEQ_SLABS

    def slab_dot(z_cur, n2):
        yb = y_ref[0, n2].reshape(2 * SEQ_PTS, tc)
        z_cur[n2] = _dot(g_ref[n2], yb)

    def fft_chunk(z_prev, r0, c0):
        rows, cols = slice(r0, r0 + rb), slice(c0, c0 + LANES)
        xs = [(z_prev[n2, rows, cols], -z_prev[n2, SEQ_PTS + r0:SEQ_PTS + r0 + rb, cols])
              for n2 in range(SEQ_SLABS)]
        out = _fft_real_part(xs)
        for k2 in range(SEQ_SLABS):
            o_ref[0, k2 * SEQ_PTS + r0:k2 * SEQ_PTS + r0 + rb, cols] = out[k2][0]

    for parity, (z_cur, z_prev) in enumerate(((z_scr[0], z_scr[1]), (z_scr[1], z_scr[0]))):
        mine = (s % 2) == parity

        @pl.when(mine & (s == 0))
        def _():
            for n2 in range(SEQ_SLABS):
                slab_dot(z_cur, n2)

        @pl.when(mine & (s > 0) & (s < last))
        def _():
            for n2 in range(SEQ_SLABS):
                slab_dot(z_cur, n2)
                for r0, c0 in chunks[n2 * per_dot:(n2 + 1) * per_dot]:
                    fft_chunk(z_prev, r0, c0)

        @pl.when(mine & (s == last))
        def _():
            for r0, c0 in chunks:
                fft_chunk(z_prev, r0, c0)


def _fnet_sample(x, pos, mod, layer, gain, cs_ch, g_seq):
    b, seq, _ = x.shape
    tm = 2 * SLAB_SUB
    tc = 256
    y = pl.pallas_call(
        _fnet_slab_kernel,
        grid=(seq // tm, b),
        in_specs=[pl.BlockSpec((1, tm, D), lambda j, i: (i, j, 0)),
                  pl.BlockSpec((tm, D), lambda j, i: (j, 0)),
                  pl.BlockSpec((1, 1, 1, N_MOD * D), lambda j, i: (layer, 1 + i, 0, 0)),
                  pl.BlockSpec((1, D), lambda j, i: (0, 0)),
                  pl.BlockSpec((GD, 2 * GD), lambda j, i: (0, 0))],
        out_specs=pl.BlockSpec((1, SEQ_SLABS, 2, tm // SEQ_SLABS, D), lambda j, i: (i, 0, 0, j, 0)),
        out_shape=jax.ShapeDtypeStruct((b, SEQ_SLABS, 2, SEQ_PTS, D), BF16),
        scratch_shapes=[pltpu.VMEM((tm // SLAB_SUB, D // LANES, SLAB_SUB, LANES), F32),
                        pltpu.VMEM((tm // SLAB_SUB, SLAB_SUB, D), BF16)],
        compiler_params=_params("parallel", "parallel"),
        name="fnet_slab",
    )(x, pos, mod, gain, cs_ch)
    ncb = D // tc
    n_blk = b * ncb
    blk_in = lambda s: jnp.minimum(s, n_blk - 1)
    blk_out = lambda s: jnp.maximum(s - 1, 0)
    return pl.pallas_call(
        _fnet_seq_kernel,
        grid=(n_blk + 1,),
        in_specs=[pl.BlockSpec((1, SEQ_SLABS, 2, SEQ_PTS, tc),
                               lambda s: (blk_in(s) // ncb, 0, 0, 0, blk_in(s) % ncb)),
                  pl.BlockSpec((SEQ_SLABS, 2 * SEQ_PTS, 2 * SEQ_PTS), lambda s: (0, 0, 0),
                               pipeline_mode=pl.Buffered(1))],
        out_specs=pl.BlockSpec((1, seq, tc), lambda s: (blk_out(s) // ncb, 0, blk_out(s) % ncb)),
        out_shape=jax.ShapeDtypeStruct((b, seq, D), F32),
        scratch_shapes=[pltpu.VMEM((SEQ_SLABS, 2 * SEQ_PTS, tc), F32)] * 2,
        compiler_params=_params("arbitrary"),
        name="fnet_seq",
    )(y, g_seq)


def _post_kernel(*refs, has_pos, final):
    it = iter(refs)
    x_ref = next(it)
    pos_ref = next(it) if has_pos else None
    a_ref, mod_ref, wo_ref, w1_ref, w2_ref, gmlp_ref, gnext_ref = (next(it) for _ in range(7))
    modn_ref = None if final else next(it)
    o_ref = next(it)
    hn_ref = None if final else next(it)
    x1_scr, h_scr, u_scr = (next(it) for _ in range(3))

    m = mod_ref[0, 0]
    subs = [slice(s * POST_SUB, (s + 1) * POST_SUB) for s in range(x_ref.shape[0] // POST_SUB)]

    for r in subs:
        x = x_ref[r, :]
        if has_pos:
            x = x + pos_ref[r, :]
        x1 = x + _mod_part(m, 2) * _dot(a_ref[r, :].astype(BF16), wo_ref[0])
        x1_scr[r, :] = x1
        h = _rms(x1, gmlp_ref[...]) * (1.0 + _mod_part(m, 4)) + _mod_part(m, 3)
        h_scr[r, :] = h.astype(BF16)

    for s, r in enumerate(subs):
        for j in range(D_FF // POST_TF):
            cols = slice(j * POST_TF, (j + 1) * POST_TF)
            u = jnp.maximum(_dot(h_scr[r, :], w1_ref[0, :, cols]), 0.0)
            u_scr[s, :, cols] = (u * u).astype(BF16)
        x2 = x1_scr[r, :] + _mod_part(m, 5) * _dot(u_scr[s], w2_ref[0])
        if final:
            o_ref[r, :] = _rms(x2, gnext_ref[...])
        else:
            o_ref[r, :] = x2
            mn = modn_ref[0, 0]
            hn = _rms(x2, gnext_ref[...]) * (1.0 + _mod_part(mn, 1)) + _mod_part(mn, 0)
            hn_ref[r, :] = hn.astype(BF16)


def _post_mixer(x, pos, a, mod, layer, mod_row, wo, w1, w2, g_mlp, g_next, final):
    rows = x.shape[0]
    tm = 2 * POST_TM if final else POST_TM
    has_pos = pos is not None
    n_pos = pos.shape[0] // tm if has_pos else 1
    row_map = lambda i: (i, 0)
    const = lambda i: (0, 0)
    resident = lambda shape: pl.BlockSpec(shape, lambda i: (layer, 0, 0), pipeline_mode=pl.Buffered(1))
    in_specs = [pl.BlockSpec((tm, D), row_map)]
    args = [x]
    if has_pos:
        in_specs.append(pl.BlockSpec((tm, D), lambda i: (i % n_pos, 0)))
        args.append(pos)
    in_specs += [pl.BlockSpec((tm, D), row_map),
                 pl.BlockSpec((1, 1, 1, N_MOD * D), lambda i: (layer, mod_row(i * tm), 0, 0)),
                 pl.BlockSpec((1, D, D), lambda i: (0, 0, 0), pipeline_mode=pl.Buffered(1)),
                 resident((1, D, D_FF)),
                 resident((1, D_FF, D)),
                 pl.BlockSpec((1, D), const),
                 pl.BlockSpec((1, D), const)]
    args += [a, mod, wo, w1, w2, g_mlp, g_next]
    out_specs = [pl.BlockSpec((tm, D), row_map)]
    out_shape = [jax.ShapeDtypeStruct((rows, D), F32)]
    if not final:
        in_specs.append(pl.BlockSpec((1, 1, 1, N_MOD * D),
                                     lambda i: (layer + 1, mod_row(i * tm), 0, 0)))
        args.append(mod)
        out_specs.append(pl.BlockSpec((tm, D), row_map))
        out_shape.append(jax.ShapeDtypeStruct((rows, D), BF16))
    res = pl.pallas_call(
        functools.partial(_post_kernel, has_pos=has_pos, final=final),
        grid=(rows // tm,),
        in_specs=in_specs,
        out_specs=out_specs,
        out_shape=out_shape,
        scratch_shapes=[pltpu.VMEM((tm, D), F32), pltpu.VMEM((tm, D), BF16),
                        pltpu.VMEM((tm // POST_SUB, POST_SUB, D_FF), BF16)],
        compiler_params=_params("parallel"),
        name="post_mlp_final" if final else "post_mlp",
    )(*args)
    return res[0] if final else (res[0], res[1])


PER_TILE = SCAN_TILE // CHUNK


def _decay_matrices():
    t = np.arange(SCAN_TILE)
    start = (t // CHUNK) * CHUNK
    first = np.arange(PER_TILE) * CHUNK
    same = (t[:, None] // CHUNK) == (t[None, :] // CHUNK)
    out = []
    for incl, mid, last in ((t[None, :] <= t[:, None], CHUNK // 2 - 1, CHUNK - 1),
                            (t[None, :] >= t[:, None], CHUNK // 2, 0)):
        bd = (same & incl).astype(np.float32)
        ref = bd[first + mid]
        pad = np.zeros((BF16_ROWS - 2 * PER_TILE, SCAN_TILE), np.float32)
        out.append(np.concatenate([bd - bd[start + mid], ref, bd[first + last] - ref, pad], axis=0))
    return jnp.asarray(np.stack(out), BF16)


def _hgrn_kernel(*refs, has_s0, independent):
    it = iter(refs)
    hn_ref = next(it)
    wp_refs = [next(it) for _ in range(PROJ)]
    lb_ref, gn_ref, dm_ref = (next(it) for _ in range(3))
    s0_ref = next(it) if has_s0 else None
    o_ref, sf_ref = next(it), next(it)
    qd_scr, kd_scr, qb_scr, kl_scr, g_scr, o_scr, k_scr = ((next(it), next(it)) for _ in range(7))
    ext_scr, q_scr, v_scr, sg_scr, st_scr, u_scr, sb_scr, gs_scr, w_scr, flag_scr = (
        next(it) for _ in range(10))

    @pl.when(pl.program_id(1) == 0)
    def _():
        for j, wp_ref in enumerate(wp_refs):
            w_scr[:, j * DK:(j + 1) * DK] = wp_ref[...].astype(BF16)

    seq = hn_ref.shape[1]
    tp = SCAN_TILE
    n_tiles = seq // tp
    per_tile = PER_TILE

    lraw = lb_ref[...]
    mx = jnp.max(lraw, axis=1, keepdims=True)
    ex = jnp.exp(lraw - mx)
    sm = ex / jnp.sum(ex, axis=1, keepdims=True)
    lbv = (sm[:, 0, :] + sm[:, 1, :]) - sm[:, 0, :]

    def gates(i):
        rows = pl.ds(pl.multiple_of(i * tp, tp), tp)
        p = _dot(hn_ref[0, rows, :], w_scr[...])
        q_scr[rows, :] = _silu(p[:, 0:DK])
        v_scr[rows, :] = p[:, 3 * DK:4 * DK].astype(BF16)
        sg_scr[rows, :] = _silu(p[:, 4 * DK:5 * DK])
        for d in range(2):
            lb = lbv[d:d + 1, :]
            f = lb + (1.0 - lb) * jax.nn.sigmoid(p[:, (1 + d) * DK:(2 + d) * DK])
            k_scr[d][rows, :] = 1.0 - f
            g = jnp.log(f)
            g_scr[d][rows, :] = g
            g1 = g.astype(BF16)
            gs_scr[d, :, 0:DK] = g1
            gs_scr[d, :, DK:2 * DK] = (g - g1.astype(F32)).astype(BF16)

    def decay_sums():
        return [_dot(dm_ref[d], gs_scr[d]) for d in range(2)]

    def factors(i, sums):
        rows = pl.ds(pl.multiple_of(i * tp, tp), tp)
        q = q_scr[rows, :]
        unsafe = jnp.int32(0)
        for d in range(2):
            r = sums[d][:, :DK] + sums[d][:, DK:]
            a = r[:tp]
            ext = r[tp:]
            ext_scr[d, i] = ext
            unsafe = unsafe | (jnp.max(jnp.abs(a)) > EXP_SAFE).astype(jnp.int32)
            qd = q * jnp.exp(a)
            kd = k_scr[d][rows, :] * jnp.exp(-a)
            qd_scr[d][rows, :] = qd.astype(BF16)
            kd_scr[d][rows, :] = kd.astype(BF16)
            e_ref = jnp.exp(ext[0:per_tile])
            e_lr = jnp.exp(ext[per_tile:2 * per_tile])
            for c in range(per_tile):
                cr = pl.ds(pl.multiple_of(i * tp + c * CHUNK, CHUNK), CHUNK)
                cs = slice(c * CHUNK, (c + 1) * CHUNK)
                qb_scr[d][cr, :] = (qd[cs] * e_ref[c:c + 1]).astype(BF16)
                kl_scr[d][cr, :] = (kd[cs] * e_lr[c:c + 1]).astype(BF16)
        flag_scr[i] = unsafe

    def init_states():
        for d in range(2):
            if has_s0:
                st_scr[d] = s0_ref[0, 0, d, 0].T
            else:
                st_scr[d] = jnp.zeros((DK, DK), F32)

    def state_in(d):
        return jnp.zeros((DK, DK), F32) if independent else st_scr[d]

    def state_out(d, tile, st):
        if independent:
            sf_ref[tile, 0, d, 0] = st.T
        else:
            st_scr[d] = st

    if not independent:
        init_states()

    ti = lax.broadcasted_iota(jnp.int32, (tp, tp), 0)
    si = lax.broadcasted_iota(jnp.int32, (tp, tp), 1)
    same = (ti // CHUNK) == (si // CHUNK)
    masks = (same & (ti >= si), same & (si >= ti))

    def scan_head(d, tile, slot=None):
        slot = d if slot is None else slot
        order = range(per_tile) if d == 0 else range(per_tile - 1, -1, -1)
        r0 = pl.multiple_of(tile * tp, tp)
        vv = v_scr[pl.ds(r0, tp), :]
        for c in range(per_tile):
            cr = pl.ds(pl.multiple_of(r0 + c * CHUNK, CHUNK), CHUNK)
            u_scr[slot, c] = _dot_tn(vv[c * CHUNK:(c + 1) * CHUNK, :], kl_scr[d][cr, :])
        ext = ext_scr[d, tile]
        decay = jnp.exp(ext[0:per_tile] + ext[per_tile:2 * per_tile])
        st = state_in(d)
        for c in order:
            sb_scr[slot, c] = st.astype(BF16)
            st = st * decay[c:c + 1] + u_scr[slot, c]
        state_out(d, tile, st)

    def scan_scores(d, tile):
        rows = pl.ds(pl.multiple_of(tile * tp, tp), tp)
        return jnp.where(masks[d], _dot_nt(qd_scr[d][rows, :], kd_scr[d][rows, :]), 0.0)

    def scan_out(d, tile, sc, slot=None):
        slot = d if slot is None else slot
        r0 = pl.multiple_of(tile * tp, tp)
        rows = pl.ds(r0, tp)
        o_scr[d][rows, :] = _dot(sc.astype(BF16), v_scr[rows, :])
        for c in range(per_tile):
            cr = pl.ds(pl.multiple_of(r0 + c * CHUNK, CHUNK), CHUNK)
            o_scr[d][cr, :] += _dot_nt(qb_scr[d][cr, :], sb_scr[slot, c])

    def scan_tail(d, tile, slot=None):
        scan_out(d, tile, scan_scores(d, tile), slot)

    def scan_tail_pair(a, b):
        sc = [scan_scores(d, tile) for d, tile, _ in (a, b)]
        for (d, tile, slot), s in zip((a, b), sc):
            scan_out(d, tile, s, slot)

    rid = lax.broadcasted_iota(jnp.int32, (BF16_ROWS, DK), 0)

    def scan_exact(d, tile):
        r0 = tile * tp

        def step(j, st):
            t = r0 + ((tp - 1 - j) if d == 1 else j)
            t0 = pl.multiple_of((t // BF16_ROWS) * BF16_ROWS, BF16_ROWS)
            grp = pl.ds(t0, BF16_ROWS)
            sel = rid == (t - t0)
            g = g_scr[d][grp, :]
            f = jnp.exp(jnp.sum(jnp.where(sel, g, 0.0), axis=0, keepdims=True))
            k = jnp.where(sel, 1.0 - jnp.exp(g), 0.0)
            st = st * f + _dot_tn(v_scr[grp, :], k.astype(BF16))
            o = _dot_nt(q_scr[grp, :].astype(BF16), st.astype(BF16))
            o_scr[d][grp, :] = jnp.where(sel, o, o_scr[d][grp, :])
            return st

        state_out(d, tile, lax.fori_loop(0, tp, step, state_in(d)))

    def finish(i):
        rows = pl.ds(pl.multiple_of(i * tp, tp), tp)
        o = o_scr[0][rows, :] + o_scr[1][rows, :]
        o = o * lax.rsqrt(jnp.mean(o * o, axis=-1, keepdims=True) + EPS) * gn_ref[...]
        o_ref[0, rows, :] = (o * sg_scr[rows, :]).astype(o_ref.dtype)

    dirs = (0, 1) if independent else (0,)

    def scan_heads(t):
        for d in dirs:
            scan_head(d, t)

    def scan_tails(t):
        if independent:
            scan_tail_pair((0, t, 0), (1, t, 1))
            finish(t)
        else:
            scan_tail(0, t)

    last = n_tiles - 1
    gates(0)
    if n_tiles > 1:
        sums = decay_sums()
        gates(1)
        factors(0, sums)

        def forward(i, carry):
            scan_heads(i - 1)
            sums = decay_sums()
            gates(i + 1)
            factors(i, sums)
            scan_tails(i - 1)
            return carry

        lax.fori_loop(1, last, forward, 0, unroll=SWEEP_UNROLL)
        scan_heads(last - 1)
        factors(last, decay_sums())
        scan_tails(last - 1)
    else:
        factors(0, decay_sums())

    if independent:
        scan_heads(last)
        scan_tails(last)
    else:
        scan_head(0, last)
        scan_head(1, last)
        scan_tail_pair((0, last, 0), (1, last, 1))
        finish(last)
        t0 = last - 1
        if t0 % 2 == 0:
            scan_head(1, t0, 0)
            scan_tail(1, t0, 0)
            finish(t0)
            t0 -= 1

        def backward(j, carry):
            t = t0 - 2 * j
            scan_head(1, t, 0)
            scan_head(1, t - 1, 1)
            scan_tail_pair((1, t, 0), (1, t - 1, 1))
            finish(t)
            finish(t - 1)
            return carry

        lax.fori_loop(0, (t0 + 1) // 2, backward, 0, unroll=BACK_UNROLL)

    unsafe = flag_scr[0]
    for t in range(1, n_tiles):
        unsafe = unsafe | flag_scr[t]

    @pl.when(unsafe != 0)
    def _():
        if not independent:
            init_states()

        def resweep(d):
            def body(j, carry):
                t = j if d == 0 else last - j

                @pl.when(flag_scr[t] != 0)
                def _():
                    scan_exact(d, t)

                @pl.when(flag_scr[t] == 0)
                def _():
                    scan_head(d, t)
                    scan_tail(d, t)

                if d == 1:
                    finish(t)
                return carry

            lax.fori_loop(0, n_tiles, body, 0)

        resweep(0)
        resweep(1)

    if not independent:
        for d in range(2):
            sf_ref[0, 0, d, 0] = st_scr[d].T


def _hgrn_mixer(hn, w_in, lb_raw, g_norm, s0):
    n_seq, seq_len, _ = hn.shape
    has_s0 = s0 is not None
    independent = (not has_s0) and seq_len == SCAN_TILE
    if independent:
        hn = hn.reshape(1, n_seq * seq_len, D)
    b, seq, _ = hn.shape
    st_rows = n_seq if independent else 1
    st_spec = pl.BlockSpec((st_rows, 1, 2, 1, DK, DK), lambda h, i: (i, 0, 0, h, 0, 0))
    in_specs = [pl.BlockSpec((1, seq, D), lambda h, i: (i, 0, 0))]
    in_specs += [pl.BlockSpec((D, DK), functools.partial(lambda j, h, i: (0, j * HEADS + h), j))
                 for j in range(PROJ)]
    in_specs += [pl.BlockSpec((2, 2, DK), lambda h, i: (0, 0, h)),
                 pl.BlockSpec((1, DK), lambda h, i: (0, 0)),
                 pl.BlockSpec((2, SCAN_TILE + BF16_ROWS, SCAN_TILE), lambda h, i: (0, 0, 0))]
    args = [hn] + [w_in] * PROJ + [lb_raw, g_norm, _decay_matrices()]
    if has_s0:
        in_specs.append(st_spec)
        args.append(s0)
    n_tiles = seq // SCAN_TILE
    bf16_rows = pltpu.VMEM((seq, DK), BF16)
    f32_rows = pltpu.VMEM((seq, DK), F32)
    o, s_fin = pl.pallas_call(
        functools.partial(_hgrn_kernel, has_s0=has_s0, independent=independent),
        grid=(HEADS, b),
        in_specs=in_specs,
        out_specs=[pl.BlockSpec((1, seq, DK), lambda h, i: (i, 0, h)), st_spec],
        out_shape=[jax.ShapeDtypeStruct((b, seq, D), BF16),
                   jax.ShapeDtypeStruct((n_seq, 1, 2, HEADS, DK, DK), F32)],
        scratch_shapes=[bf16_rows] * 8 + [f32_rows] * 6 + [
                        pltpu.VMEM((2, n_tiles, BF16_ROWS, DK), F32),
                        f32_rows, bf16_rows, f32_rows,
                        pltpu.VMEM((2, DK, DK), F32),
                        pltpu.VMEM((2, PER_TILE, DK, DK), F32),
                        pltpu.VMEM((2, PER_TILE, DK, DK), BF16),
                        pltpu.VMEM((2, SCAN_TILE, 2 * DK), BF16),
                        pltpu.VMEM((D, PROJ * DK), BF16),
                        pltpu.SMEM((n_tiles,), jnp.int32)],
        compiler_params=_params("parallel", "arbitrary"),
        name="hgrn_scan",
    )(*args)
    return o.reshape(n_seq, seq_len, D), s_fin


def kernel(x_prompt, x_sample, state_hgrn, c, c_ctx, ada_w, ada_b, norm_mix, norm_mlp, fnet_wo,
           hgrn_w_in, hgrn_lb, hgrn_norm, hgrn_wo, mlp_w1, mlp_w2, norm_final):
    bp, lp, _ = x_prompt.shape
    bs, ls, _ = x_sample.shape
    assert ada_w.shape[0] == 2 and ls == SEQ_PTS * SEQ_SLABS and lp == GD

    cs_ch, t_ctx, g_seq = _dft_tables()
    pos = _grid_pos_embed(ls, D)

    cond8 = jnp.zeros((8, D), F32).at[0].set(c_ctx).at[1:1 + bs].set(c)
    mod = _modulation(cond8, ada_w, ada_b)

    wo0 = fnet_wo.astype(BF16)
    wo1 = hgrn_wo.astype(BF16)
    w1 = mlp_w1.astype(BF16)
    w2 = mlp_w2.astype(BF16)
    w_heads = hgrn_w_in[0]
    gm = norm_mix.reshape(-1, 1, D)
    gf = norm_mlp.reshape(-1, 1, D)
    g_fin = norm_final.reshape(1, D)
    g_hn = hgrn_norm[0].reshape(1, DK)

    ctx_row = lambda r: 0
    smp_row = lambda r: 1 + r // ls

    mixed = _fnet_ctx(x_prompt, mod, 0, gm[0], cs_ch, t_ctx)
    x2, hn = _post_mixer(x_prompt.reshape(bp * lp, D), None, mixed.reshape(bp * lp, D), mod, 0,
                         ctx_row, wo0, w1, w2, gf[0], gm[1], final=False)
    o, new_state = _hgrn_mixer(hn.reshape(bp, lp, D), w_heads, hgrn_lb, g_hn, None)
    y_prompt = _post_mixer(x2, None, o.reshape(bp * lp, D), mod, 1, ctx_row, wo1, w1, w2,
                           gf[1], g_fin, final=True).reshape(bp, lp, D)

    mixed = _fnet_sample(x_sample, pos, mod, 0, gm[0], cs_ch, g_seq)
    x2, hn = _post_mixer(x_sample.reshape(bs * ls, D), pos, mixed.reshape(bs * ls, D), mod, 0,
                         smp_row, wo0, w1, w2, gf[0], gm[1], final=False)
    o, _ = _hgrn_mixer(hn.reshape(bs, ls, D), w_heads, hgrn_lb, g_hn, state_hgrn)
    y_sample = _post_mixer(x2, None, o.reshape(bs * ls, D), mod, 1, smp_row, wo1, w1, w2,
                           gf[1], g_fin, final=True).reshape(bs, ls, D)
    return (y_prompt, y_sample, new_state)
```

```python
import functools
import math

import numpy as np
import jax
import jax.numpy as jnp
from jax import lax
from jax.experimental import pallas as pl
from jax.experimental.pallas import tpu as pltpu

F32 = jnp.float32
BF16 = jnp.bfloat16

D = 1024
N_MOD = 6
D_FF = 4 * D
EPS = 1e-6
GROUPS = 4
GD = D // GROUPS
HEADS = 8
DK = 128
GRID_W = 64
POS_BASE = 10000.0
PROJ = 5

SEQ_SLABS = 16
SEQ_PTS = 256
LANES = 128
BF16_ROWS = 16
CHUNK = 64
SCAN_TILE = 256
EXP_SAFE = 85.0
SLAB_SUB = 512
POST_TM = 512
POST_SUB = 256
POST_TF = 1024

VMEM_LIMIT = 56 * 1024 * 1024


def _dft_tables():
    n = np.arange(GD)
    ang = 2.0 * np.pi * ((n[:, None] * n[None, :]) % GD) / GD
    cs_ch = np.concatenate([np.cos(ang), np.sin(ang)], axis=1) / math.sqrt(GD)
    t_ctx = np.concatenate([np.cos(ang), -np.sin(ang)], axis=1) / math.sqrt(GD)
    length = SEQ_PTS * SEQ_SLABS
    k1 = np.arange(SEQ_PTS)[:, None]
    n1 = np.arange(SEQ_PTS)[None, :]
    blocks = []
    for n2 in range(SEQ_SLABS):
        a = 2.0 * np.pi * ((k1 * (SEQ_SLABS * n1 + n2)) % length) / length
        gc, gs = np.cos(a) / math.sqrt(length), np.sin(a) / math.sqrt(length)
        blocks.append(np.block([[gc, -gs], [gs, gc]]))
    g_seq = np.stack(blocks)
    return tuple(jnp.asarray(t, F32).astype(BF16) for t in (cs_ch, t_ctx, g_seq))


def _grid_pos_embed(n_tok, d):
    rows = n_tok // GRID_W
    quarter = d // 4
    omega = 1.0 / (POS_BASE ** (jnp.arange(quarter, dtype=F32) / quarter))
    r = jnp.arange(rows, dtype=F32)[:, None] * omega[None, :]
    cl = jnp.arange(GRID_W, dtype=F32)[:, None] * omega[None, :]
    er = jnp.concatenate([jnp.sin(r), jnp.cos(r)], axis=-1)
    ec = jnp.concatenate([jnp.sin(cl), jnp.cos(cl)], axis=-1)
    emb = jnp.concatenate([jnp.broadcast_to(er[:, None, :], (rows, GRID_W, d // 2)),
                           jnp.broadcast_to(ec[None, :, :], (rows, GRID_W, d // 2))], axis=-1)
    return emb.reshape(n_tok, d)


def _silu(x):
    return x * jax.nn.sigmoid(x)


def _rms(x, g):
    return x * lax.rsqrt(jnp.mean(x * x, axis=-1, keepdims=True) + EPS) * g


def _mod_part(m, idx):
    return m[:, idx * D:(idx + 1) * D]


def _dot(a, b):
    return jnp.dot(a, b, preferred_element_type=F32)


def _dot_nt(a, b):
    return lax.dot_general(a, b, (((1,), (1,)), ((), ())), preferred_element_type=F32)


def _dot_tn(a, b):
    return lax.dot_general(a, b, (((0,), (0,)), ((), ())), preferred_element_type=F32)


def _aligned(x, m):
    return x if isinstance(x, int) else pl.multiple_of(x, m)


def _params(*sem, **kw):
    return pltpu.CompilerParams(dimension_semantics=sem, vmem_limit_bytes=VMEM_LIMIT, **kw)


def _mod_kernel(c_ref, w_ref, b_ref, o_ref):
    s = _silu(c_ref[...])
    o_ref[0] = _dot(s.astype(BF16), w_ref[0].astype(BF16)) + b_ref[0]


def _modulation(cond8, ada_w, ada_b):
    depth = ada_w.shape[0]
    tn = 1536
    out = pl.pallas_call(
        _mod_kernel,
        grid=(depth, N_MOD * D // tn),
        in_specs=[pl.BlockSpec((8, D), lambda l, j: (0, 0)),
                  pl.BlockSpec((1, D, tn), lambda l, j: (l, 0, j)),
                  pl.BlockSpec((1, 1, tn), lambda l, j: (l, 0, j))],
        out_specs=pl.BlockSpec((1, 8, tn), lambda l, j: (l, 0, j)),
        out_shape=jax.ShapeDtypeStruct((depth, 8, N_MOD * D), F32),
        compiler_params=_params("parallel", "parallel"),
        name="adaln_mod",
    )(cond8, ada_w, ada_b.reshape(depth, 1, N_MOD * D))
    return out.reshape(depth, 8, 1, N_MOD * D)


def _channel_dft(hb, cs_ref, put):
    for g in range(GROUPS):
        y = _dot(hb[:, g * GD:(g + 1) * GD], cs_ref[...])
        put(0, g, y[:, :GD].astype(BF16))
        put(1, g, y[:, GD:].astype(BF16))


def _fnet_ctx_kernel(x_ref, mod_ref, g_ref, cs_ref, t_ref, o_ref, y_scr):
    m = mod_ref[0, 0]
    h = _rms(x_ref[0], g_ref[...]) * (1.0 + _mod_part(m, 1)) + _mod_part(m, 0)
    seq = x_ref.shape[1]

    def put(kind, g, val):
        y_scr[kind * seq:(kind + 1) * seq, g * GD:(g + 1) * GD] = val

    _channel_dft(h.astype(BF16), cs_ref, put)
    o_ref[0] = _dot(t_ref[...], y_scr[...]).astype(o_ref.dtype)


def _fnet_ctx(x, mod, layer, gain, cs_ch, t_ctx):
    b, seq, _ = x.shape
    return pl.pallas_call(
        _fnet_ctx_kernel,
        grid=(b,),
        in_specs=[pl.BlockSpec((1, seq, D), lambda i: (i, 0, 0)),
                  pl.BlockSpec((1, 1, 1, N_MOD * D), lambda i: (layer, 0, 0, 0)),
                  pl.BlockSpec((1, D), lambda i: (0, 0)),
                  pl.BlockSpec((GD, 2 * GD), lambda i: (0, 0)),
                  pl.BlockSpec((seq, 2 * seq), lambda i: (0, 0))],
        out_specs=pl.BlockSpec((1, seq, D), lambda i: (i, 0, 0)),
        out_shape=jax.ShapeDtypeStruct((b, seq, D), BF16),
        scratch_shapes=[pltpu.VMEM((2 * seq, D), BF16)],
        compiler_params=_params("parallel"),
        name="fnet_ctx",
    )(x, mod, gain, cs_ch, t_ctx)


def _fnet_slab_kernel(x_ref, pos_ref, mod_ref, g_ref, cs_ref, y_ref, col_scr, hb_scr):
    rows = SLAB_SUB // SEQ_SLABS
    n_col = D // LANES
    m = mod_ref[0, 0]
    for s in range(x_ref.shape[1] // SLAB_SUB):
        sub = slice(s * SLAB_SUB, (s + 1) * SLAB_SUB)
        for j in range(n_col):
            cols = slice(j * LANES, (j + 1) * LANES)
            col_scr[s, j] = x_ref[0, sub, cols] + pos_ref[sub, cols]
        for n2 in range(SEQ_SLABS):
            xs = jnp.concatenate([col_scr[s, j, pl.ds(n2, rows, stride=SEQ_SLABS), :]
                                  for j in range(n_col)], axis=1)
            h = _rms(xs, g_ref[...]) * (1.0 + _mod_part(m, 1)) + _mod_part(m, 0)
            hb_scr[s, n2 * rows:(n2 + 1) * rows, :] = h.astype(BF16)
        dst = slice(s * rows, (s + 1) * rows)
        for g in range(GROUPS):
            y = _dot(hb_scr[s, :, g * GD:(g + 1) * GD], cs_ref[...])
            for n2 in range(SEQ_SLABS):
                src = slice(n2 * rows, (n2 + 1) * rows)
                y_ref[0, n2, 0, dst, g * GD:(g + 1) * GD] = y[src, :GD].astype(BF16)
                y_ref[0, n2, 1, dst, g * GD:(g + 1) * GD] = y[src, GD:].astype(BF16)


def _fft_real_part(xs):
    n = len(xs)
    if n == 1:
        return xs
    ev = _fft_real_part(xs[0::2])
    od = _fft_real_part(xs[1::2])
    out = [None] * n
    for k in range(n // 2):
        c = math.cos(2.0 * math.pi * k / n)
        s = -math.sin(2.0 * math.pi * k / n)
        orr, oi = od[k]
        if k == 0:
            tr, ti = orr, oi
        elif 4 * k == n:
            tr, ti = oi, -orr
        else:
            tr, ti = orr * c - oi * s, orr * s + oi * c
        er, ei = ev[k]
        out[k] = (er + tr, ei + ti)
        out[k + n // 2] = (er - tr, ei - ti)
    return out


def _fnet_seq_kernel(y_ref, g_ref, o_ref, *z_scr):
    tc = o_ref.shape[2]
    s = pl.program_id(0)
    last = pl.num_programs(0) - 1
    rb = 8
    chunks = [(r0, c0) for r0 in range(0, SEQ_PTS, rb) for c0 in range(0, tc, LANES)]
    per_dot = len(chunks) // SEQ_SLABS

    def slab_dot(z_cur, n2):
        yb = y_ref[0, n2].reshape(2 * SEQ_PTS, tc)
        z_cur[n2] = _dot(g_ref[n2], yb)

    def fft_chunk(z_prev, r0, c0):
        rows, cols = slice(r0, r0 + rb), slice(c0, c0 + LANES)
        xs = [(z_prev[n2, rows, cols], -z_prev[n2, SEQ_PTS + r0:SEQ_PTS + r0 + rb, cols])
              for n2 in range(SEQ_SLABS)]
        out = _fft_real_part(xs)
        for k2 in range(SEQ_SLABS):
            o_ref[0, k2 * SEQ_PTS + r0:k2 * SEQ_PTS + r0 + rb, cols] = out[k2][0]

    for parity, (z_cur, z_prev) in enumerate(((z_scr[0], z_scr[1]), (z_scr[1], z_scr[0]))):
        mine = (s % 2) == parity

        @pl.when(mine & (s == 0))
        def _():
            for n2 in range(SEQ_SLABS):
                slab_dot(z_cur, n2)

        @pl.when(mine & (s > 0) & (s < last))
        def _():
            for n2 in range(SEQ_SLABS):
                slab_dot(z_cur, n2)
                for r0, c0 in chunks[n2 * per_dot:(n2 + 1) * per_dot]:
                    fft_chunk(z_prev, r0, c0)

        @pl.when(mine & (s == last))
        def _():
            for r0, c0 in chunks:
                fft_chunk(z_prev, r0, c0)


def _fnet_sample(x, pos, mod, layer, gain, cs_ch, g_seq):
    b, seq, _ = x.shape
    tm = 2 * SLAB_SUB
    tc = 256
    y = pl.pallas_call(
        _fnet_slab_kernel,
        grid=(seq // tm, b),
        in_specs=[pl.BlockSpec((1, tm, D), lambda j, i: (i, j, 0)),
                  pl.BlockSpec((tm, D), lambda j, i: (j, 0)),
                  pl.BlockSpec((1, 1, 1, N_MOD * D), lambda j, i: (layer, 1 + i, 0, 0)),
                  pl.BlockSpec((1, D), lambda j, i: (0, 0)),
                  pl.BlockSpec((GD, 2 * GD), lambda j, i: (0, 0))],
        out_specs=pl.BlockSpec((1, SEQ_SLABS, 2, tm // SEQ_SLABS, D), lambda j, i: (i, 0, 0, j, 0)),
        out_shape=jax.ShapeDtypeStruct((b, SEQ_SLABS, 2, SEQ_PTS, D), BF16),
        scratch_shapes=[pltpu.VMEM((tm // SLAB_SUB, D // LANES, SLAB_SUB, LANES), F32),
                        pltpu.VMEM((tm // SLAB_SUB, SLAB_SUB, D), BF16)],
        compiler_params=_params("parallel", "parallel"),
        name="fnet_slab",
    )(x, pos, mod, gain, cs_ch)
    ncb = D // tc
    n_blk = b * ncb
    blk_in = lambda s: jnp.minimum(s, n_blk - 1)
    blk_out = lambda s: jnp.maximum(s - 1, 0)
    return pl.pallas_call(
        _fnet_seq_kernel,
        grid=(n_blk + 1,),
        in_specs=[pl.BlockSpec((1, SEQ_SLABS, 2, SEQ_PTS, tc),
                               lambda s: (blk_in(s) // ncb, 0, 0, 0, blk_in(s) % ncb)),
                  pl.BlockSpec((SEQ_SLABS, 2 * SEQ_PTS, 2 * SEQ_PTS), lambda s: (0, 0, 0),
                               pipeline_mode=pl.Buffered(1))],
        out_specs=pl.BlockSpec((1, seq, tc), lambda s: (blk_out(s) // ncb, 0, blk_out(s) % ncb)),
        out_shape=jax.ShapeDtypeStruct((b, seq, D), F32),
        scratch_shapes=[pltpu.VMEM((SEQ_SLABS, 2 * SEQ_PTS, tc), F32)] * 2,
        compiler_params=_params("arbitrary"),
        name="fnet_seq",
    )(y, g_seq)


def _post_kernel(*refs, has_pos, final):
    it = iter(refs)
    x_ref = next(it)
    pos_ref = next(it) if has_pos else None
    a_ref, mod_ref, wo_ref, w1_ref, w2_ref, gmlp_ref, gnext_ref = (next(it) for _ in range(7))
    modn_ref = None if final else next(it)
    o_ref = next(it)
    hn_ref = None if final else next(it)
    x1_scr, h_scr, u_scr = (next(it) for _ in range(3))

    m = mod_ref[0, 0]
    subs = [slice(s * POST_SUB, (s + 1) * POST_SUB) for s in range(x_ref.shape[0] // POST_SUB)]

    for r in subs:
        x = x_ref[r, :]
        if has_pos:
            x = x + pos_ref[r, :]
        x1 = x + _mod_part(m, 2) * _dot(a_ref[r, :].astype(BF16), wo_ref[0])
        x1_scr[r, :] = x1
        h = _rms(x1, gmlp_ref[...]) * (1.0 + _mod_part(m, 4)) + _mod_part(m, 3)
        h_scr[r, :] = h.astype(BF16)

    for s, r in enumerate(subs):
        for j in range(D_FF // POST_TF):
            cols = slice(j * POST_TF, (j + 1) * POST_TF)
            u = jnp.maximum(_dot(h_scr[r, :], w1_ref[0, :, cols]), 0.0)
            u_scr[s, :, cols] = (u * u).astype(BF16)
        x2 = x1_scr[r, :] + _mod_part(m, 5) * _dot(u_scr[s], w2_ref[0])
        if final:
            o_ref[r, :] = _rms(x2, gnext_ref[...])
        else:
            o_ref[r, :] = x2
            mn = modn_ref[0, 0]
            hn = _rms(x2, gnext_ref[...]) * (1.0 + _mod_part(mn, 1)) + _mod_part(mn, 0)
            hn_ref[r, :] = hn.astype(BF16)


def _post_mixer(x, pos, a, mod, layer, mod_row, wo, w1, w2, g_mlp, g_next, final):
    rows = x.shape[0]
    tm = 2 * POST_TM if final else POST_TM
    has_pos = pos is not None
    n_pos = pos.shape[0] // tm if has_pos else 1
    row_map = lambda i: (i, 0)
    const = lambda i: (0, 0)
    resident = lambda shape: pl.BlockSpec(shape, lambda i: (layer, 0, 0), pipeline_mode=pl.Buffered(1))
    in_specs = [pl.BlockSpec((tm, D), row_map)]
    args = [x]
    if has_pos:
        in_specs.append(pl.BlockSpec((tm, D), lambda i: (i % n_pos, 0)))
        args.append(pos)
    in_specs += [pl.BlockSpec((tm, D), row_map),
                 pl.BlockSpec((1, 1, 1, N_MOD * D), lambda i: (layer, mod_row(i * tm), 0, 0)),
                 pl.BlockSpec((1, D, D), lambda i: (0, 0, 0), pipeline_mode=pl.Buffered(1)),
                 resident((1, D, D_FF)),
                 resident((1, D_FF, D)),
                 pl.BlockSpec((1, D), const),
                 pl.BlockSpec((1, D), const)]
    args += [a, mod, wo, w1, w2, g_mlp, g_next]
    out_specs = [pl.BlockSpec((tm, D), row_map)]
    out_shape = [jax.ShapeDtypeStruct((rows, D), F32)]
    if not final:
        in_specs.append(pl.BlockSpec((1, 1, 1, N_MOD * D),
                                     lambda i: (layer + 1, mod_row(i * tm), 0, 0)))
        args.append(mod)
        out_specs.append(pl.BlockSpec((tm, D), row_map))
        out_shape.append(jax.ShapeDtypeStruct((rows, D), BF16))
    res = pl.pallas_call(
        functools.partial(_post_kernel, has_pos=has_pos, final=final),
        grid=(rows // tm,),
        in_specs=in_specs,
        out_specs=out_specs,
        out_shape=out_shape,
        scratch_shapes=[pltpu.VMEM((tm, D), F32), pltpu.VMEM((tm, D), BF16),
                        pltpu.VMEM((tm // POST_SUB, POST_SUB, D_FF), BF16)],
        compiler_params=_params("parallel"),
        name="post_mlp_final" if final else "post_mlp",
    )(*args)
    return res[0] if final else (res[0], res[1])


PER_TILE = SCAN_TILE // CHUNK


def _decay_matrices():
    t = np.arange(SCAN_TILE)
    start = (t // CHUNK) * CHUNK
    first = np.arange(PER_TILE) * CHUNK
    same = (t[:, None] // CHUNK) == (t[None, :] // CHUNK)
    out = []
    for incl, mid, last in ((t[None, :] <= t[:, None], CHUNK // 2 - 1, CHUNK - 1),
                            (t[None, :] >= t[:, None], CHUNK // 2, 0)):
        bd = (same & incl).astype(np.float32)
        ref = bd[first + mid]
        pad = np.zeros((BF16_ROWS - 2 * PER_TILE, SCAN_TILE), np.float32)
        out.append(np.concatenate([bd - bd[start + mid], ref, bd[first + last] - ref, pad], axis=0))
    return jnp.asarray(np.stack(out), BF16)


def _hgrn_kernel(*refs, has_s0, independent):
    it = iter(refs)
    hn_ref = next(it)
    wp_refs = [next(it) for _ in range(PROJ)]
    lb_ref, gn_ref, dm_ref = (next(it) for _ in range(3))
    s0_ref = next(it) if has_s0 else None
    o_ref, sf_ref = next(it), next(it)
    qd_scr, kd_scr, qb_scr, kl_scr, g_scr, o_scr, k_scr = ((next(it), next(it)) for _ in range(7))
    ext_scr, q_scr, v_scr, sg_scr, st_scr, u_scr, sb_scr, gs_scr, w_scr, flag_scr = (
        next(it) for _ in range(10))

    @pl.when(pl.program_id(1) == 0)
    def _():
        for j, wp_ref in enumerate(wp_refs):
            w_scr[:, j * DK:(j + 1) * DK] = wp_ref[...].astype(BF16)

    seq = hn_ref.shape[1]
    tp = SCAN_TILE
    n_tiles = seq // tp
    per_tile = PER_TILE

    lraw = lb_ref[...]
    mx = jnp.max(lraw, axis=1, keepdims=True)
    ex = jnp.exp(lraw - mx)
    sm = ex / jnp.sum(ex, axis=1, keepdims=True)
    lbv = (sm[:, 0, :] + sm[:, 1, :]) - sm[:, 0, :]

    def gates(i, count):
        p_all = _dot(hn_ref[0, pl.ds(i * tp, count * tp), :], w_scr[...])
        for t in range(count):
            rows = pl.ds((i + t) * tp, tp)
            p = p_all[t * tp:(t + 1) * tp]
            q_scr[rows, :] = _silu(p[:, 0:DK])
            v_scr[rows, :] = p[:, 3 * DK:4 * DK].astype(BF16)
            sg_scr[rows, :] = _silu(p[:, 4 * DK:5 * DK])
            for d in range(2):
                lb = lbv[d:d + 1, :]
                f = lb + (1.0 - lb) * jax.nn.sigmoid(p[:, (1 + d) * DK:(2 + d) * DK])
                k_scr[d][rows, :] = 1.0 - f
                g = jnp.log(f)
                g_scr[d][rows, :] = g
                g1 = g.astype(BF16)
                gs_scr[(i + t) % 2, d, :, 0:DK] = g1
                gs_scr[(i + t) % 2, d, :, DK:2 * DK] = (g - g1.astype(F32)).astype(BF16)

    def decay_sums(i):
        return [_dot(dm_ref[d], gs_scr[i % 2, d]) for d in range(2)]

    def factors(i, sums):
        rows = pl.ds(_aligned(i * tp, tp), tp)
        q = q_scr[rows, :]
        unsafe = jnp.int32(0)
        for d in range(2):
            r = sums[d][:, :DK] + sums[d][:, DK:]
            a = r[:tp]
            ext = r[tp:]
            ext_scr[d, i] = ext
            unsafe = unsafe | (jnp.max(jnp.abs(a)) > EXP_SAFE).astype(jnp.int32)
            qd = q * jnp.exp(a)
            kd = k_scr[d][rows, :] * jnp.exp(-a)
            qd_scr[d][rows, :] = qd.astype(BF16)
            kd_scr[d][rows, :] = kd.astype(BF16)
            e_ref = jnp.exp(ext[0:per_tile])
            e_lr = jnp.exp(ext[per_tile:2 * per_tile])
            for c in range(per_tile):
                cr = pl.ds(_aligned(i * tp + c * CHUNK, CHUNK), CHUNK)
                cs = slice(c * CHUNK, (c + 1) * CHUNK)
                qb_scr[d][cr, :] = (qd[cs] * e_ref[c:c + 1]).astype(BF16)
                kl_scr[d][cr, :] = (kd[cs] * e_lr[c:c + 1]).astype(BF16)
        flag_scr[i] = unsafe

    def init_states():
        for d in range(2):
            if has_s0:
                st_scr[d] = s0_ref[0, 0, d, 0].T
            else:
                st_scr[d] = jnp.zeros((DK, DK), F32)

    def state_in(d):
        return jnp.zeros((DK, DK), F32) if independent else st_scr[d]

    def state_out(d, tile, st):
        if independent:
            sf_ref[tile, 0, d, 0] = st.T
        else:
            st_scr[d] = st

    if not independent:
        init_states()

    ti = lax.broadcasted_iota(jnp.int32, (tp, tp), 0)
    si = lax.broadcasted_iota(jnp.int32, (tp, tp), 1)
    same = (ti // CHUNK) == (si // CHUNK)
    masks = (same & (ti >= si), same & (si >= ti))

    def scan_head(d, tile, slot=None):
        slot = d if slot is None else slot
        order = range(per_tile) if d == 0 else range(per_tile - 1, -1, -1)
        r0 = _aligned(tile * tp, tp)
        vv = v_scr[pl.ds(r0, tp), :]
        for c in range(per_tile):
            cr = pl.ds(_aligned(r0 + c * CHUNK, CHUNK), CHUNK)
            u_scr[slot, c] = _dot_tn(vv[c * CHUNK:(c + 1) * CHUNK, :], kl_scr[d][cr, :])
        ext = ext_scr[d, tile]
        decay = jnp.exp(ext[0:per_tile] + ext[per_tile:2 * per_tile])
        st = state_in(d)
        for c in order:
            sb_scr[slot, c] = st.astype(BF16)
            st = st * decay[c:c + 1] + u_scr[slot, c]
        state_out(d, tile, st)

    def scan_scores(d, tile):
        rows = pl.ds(_aligned(tile * tp, tp), tp)
        return jnp.where(masks[d], _dot_nt(qd_scr[d][rows, :], kd_scr[d][rows, :]), 0.0)

    def scan_out(d, tile, sc, slot=None):
        slot = d if slot is None else slot
        r0 = _aligned(tile * tp, tp)
        rows = pl.ds(r0, tp)
        o_scr[d][rows, :] = _dot(sc.astype(BF16), v_scr[rows, :])
        for c in range(per_tile):
            cr = pl.ds(_aligned(r0 + c * CHUNK, CHUNK), CHUNK)
            o_scr[d][cr, :] += _dot_nt(qb_scr[d][cr, :], sb_scr[slot, c])

    def scan_tail(d, tile, slot=None):
        scan_out(d, tile, scan_scores(d, tile), slot)

    def scan_tail_pair(a, b):
        sc = [scan_scores(d, tile) for d, tile, _ in (a, b)]
        for (d, tile, slot), s in zip((a, b), sc):
            scan_out(d, tile, s, slot)

    rid = lax.broadcasted_iota(jnp.int32, (BF16_ROWS, DK), 0)

    def scan_exact(d, tile):
        r0 = tile * tp

        def step(j, st):
            t = r0 + ((tp - 1 - j) if d == 1 else j)
            t0 = _aligned((t // BF16_ROWS) * BF16_ROWS, BF16_ROWS)
            grp = pl.ds(t0, BF16_ROWS)
            sel = rid == (t - t0)
            g = g_scr[d][grp, :]
            f = jnp.exp(jnp.sum(jnp.where(sel, g, 0.0), axis=0, keepdims=True))
            k = jnp.where(sel, 1.0 - jnp.exp(g), 0.0)
            st = st * f + _dot_tn(v_scr[grp, :], k.astype(BF16))
            o = _dot_nt(q_scr[grp, :].astype(BF16), st.astype(BF16))
            o_scr[d][grp, :] = jnp.where(sel, o, o_scr[d][grp, :])
            return st

        state_out(d, tile, lax.fori_loop(0, tp, step, state_in(d)))

    def finish(i):
        rows = pl.ds(_aligned(i * tp, tp), tp)
        o = o_scr[0][rows, :] + o_scr[1][rows, :]
        o = o * lax.rsqrt(jnp.mean(o * o, axis=-1, keepdims=True) + EPS) * gn_ref[...]
        o_ref[0, rows, :] = (o * sg_scr[rows, :]).astype(o_ref.dtype)

    dirs = (0, 1) if independent else (0,)

    def scan_heads(t):
        for d in dirs:
            scan_head(d, t)
        return [scan_scores(d, t) for d in dirs]

    def scan_tails(t, scores):
        for d, sc in zip(dirs, scores):
            scan_out(d, t, sc)
        if independent:
            finish(t)

    last = n_tiles - 1
    gates(0, min(2, n_tiles))
    for i in range(n_tiles):
        scores = scan_heads(i - 1) if i > 0 else None
        sums = decay_sums(i)
        if i % 2 == 1 and i + 1 < n_tiles:
            gates(i + 1, min(2, n_tiles - i - 1))
        factors(i, sums)
        if i > 0:
            scan_tails(i - 1, scores)

    if independent:
        scan_tails(last, scan_heads(last))
    else:
        scan_head(0, last)
        scan_head(1, last)
        scan_tail_pair((0, last, 0), (1, last, 1))
        finish(last)
        t0 = last - 1
        if t0 % 2 == 0:
            scan_head(1, t0, 0)
            scan_tail(1, t0, 0)
            finish(t0)
            t0 -= 1

        for t in range(t0, 0, -2):
            scan_head(1, t, 0)
            scan_head(1, t - 1, 1)
            scan_tail_pair((1, t, 0), (1, t - 1, 1))
            finish(t)
            finish(t - 1)

    unsafe = flag_scr[0]
    for t in range(1, n_tiles):
        unsafe = unsafe | flag_scr[t]

    @pl.when(unsafe != 0)
    def _():
        if not independent:
            init_states()

        def resweep(d):
            def body(j, carry):
                t = j if d == 0 else last - j

                @pl.when(flag_scr[t] != 0)
                def _():
                    scan_exact(d, t)

                @pl.when(flag_scr[t] == 0)
                def _():
                    scan_head(d, t)
                    scan_tail(d, t)

                if d == 1:
                    finish(t)
                return carry

            lax.fori_loop(0, n_tiles, body, 0)

        resweep(0)
        resweep(1)

    if not independent:
        for d in range(2):
            sf_ref[0, 0, d, 0] = st_scr[d].T


def _hgrn_mixer(hn, w_in, lb_raw, g_norm, s0):
    n_seq, seq_len, _ = hn.shape
    has_s0 = s0 is not None
    independent = (not has_s0) and seq_len == SCAN_TILE
    if independent:
        hn = hn.reshape(1, n_seq * seq_len, D)
    b, seq, _ = hn.shape
    st_rows = n_seq if independent else 1
    st_spec = pl.BlockSpec((st_rows, 1, 2, 1, DK, DK), lambda h, i: (i, 0, 0, h, 0, 0))
    in_specs = [pl.BlockSpec((1, seq, D), lambda h, i: (i, 0, 0))]
    in_specs += [pl.BlockSpec((D, DK), functools.partial(lambda j, h, i: (0, j * HEADS + h), j))
                 for j in range(PROJ)]
    in_specs += [pl.BlockSpec((2, 2, DK), lambda h, i: (0, 0, h)),
                 pl.BlockSpec((1, DK), lambda h, i: (0, 0)),
                 pl.BlockSpec((2, SCAN_TILE + BF16_ROWS, SCAN_TILE), lambda h, i: (0, 0, 0))]
    args = [hn] + [w_in] * PROJ + [lb_raw, g_norm, _decay_matrices()]
    if has_s0:
        in_specs.append(st_spec)
        args.append(s0)
    n_tiles = seq // SCAN_TILE
    bf16_rows = pltpu.VMEM((seq, DK), BF16)
    f32_rows = pltpu.VMEM((seq, DK), F32)
    o, s_fin = pl.pallas_call(
        functools.partial(_hgrn_kernel, has_s0=has_s0, independent=independent),
        grid=(HEADS, b),
        in_specs=in_specs,
        out_specs=[pl.BlockSpec((1, seq, DK), lambda h, i: (i, 0, h)), st_spec],
        out_shape=[jax.ShapeDtypeStruct((b, seq, D), BF16),
                   jax.ShapeDtypeStruct((n_seq, 1, 2, HEADS, DK, DK), F32)],
        scratch_shapes=[bf16_rows] * 8 + [f32_rows] * 6 + [
                        pltpu.VMEM((2, n_tiles, BF16_ROWS, DK), F32),
                        f32_rows, bf16_rows, f32_rows,
                        pltpu.VMEM((2, DK, DK), F32),
                        pltpu.VMEM((2, PER_TILE, DK, DK), F32),
                        pltpu.VMEM((2, PER_TILE, DK, DK), BF16),
                        pltpu.VMEM((2, 2, SCAN_TILE, 2 * DK), BF16),
                        pltpu.VMEM((D, PROJ * DK), BF16),
                        pltpu.SMEM((n_tiles,), jnp.int32)],
        compiler_params=_params("parallel", "arbitrary"),
        name="hgrn_scan",
    )(*args)
    return o.reshape(n_seq, seq_len, D), s_fin


def kernel(x_prompt, x_sample, state_hgrn, c, c_ctx, ada_w, ada_b, norm_mix, norm_mlp, fnet_wo,
           hgrn_w_in, hgrn_lb, hgrn_norm, hgrn_wo, mlp_w1, mlp_w2, norm_final):
    bp, lp, _ = x_prompt.shape
    bs, ls, _ = x_sample.shape
    assert ada_w.shape[0] == 2 and ls == SEQ_PTS * SEQ_SLABS and lp == GD

    cs_ch, t_ctx, g_seq = _dft_tables()
    pos = _grid_pos_embed(ls, D)

    cond8 = jnp.zeros((8, D), F32).at[0].set(c_ctx).at[1:1 + bs].set(c)
    mod = _modulation(cond8, ada_w, ada_b)

    wo0 = fnet_wo.astype(BF16)
    wo1 = hgrn_wo.astype(BF16)
    w1 = mlp_w1.astype(BF16)
    w2 = mlp_w2.astype(BF16)
    w_heads = hgrn_w_in[0]
    gm = norm_mix.reshape(-1, 1, D)
    gf = norm_mlp.reshape(-1, 1, D)
    g_fin = norm_final.reshape(1, D)
    g_hn = hgrn_norm[0].reshape(1, DK)

    ctx_row = lambda r: 0
    smp_row = lambda r: 1 + r // ls

    mixed = _fnet_ctx(x_prompt, mod, 0, gm[0], cs_ch, t_ctx)
    x2, hn = _post_mixer(x_prompt.reshape(bp * lp, D), None, mixed.reshape(bp * lp, D), mod, 0,
                         ctx_row, wo0, w1, w2, gf[0], gm[1], final=False)
    o, new_state = _hgrn_mixer(hn.reshape(bp, lp, D), w_heads, hgrn_lb, g_hn, None)
    y_prompt = _post_mixer(x2, None, o.reshape(bp * lp, D), mod, 1, ctx_row, wo1, w1, w2,
                           gf[1], g_fin, final=True).reshape(bp, lp, D)

    mixed = _fnet_sample(x_sample, pos, mod, 0, gm[0], cs_ch, g_seq)
    x2, hn = _post_mixer(x_sample.reshape(bs * ls, D), pos, mixed.reshape(bs * ls, D), mod, 0,
                         smp_row, wo0, w1, w2, gf[0], gm[1], final=False)
    o, _ = _hgrn_mixer(hn.reshape(bs, ls, D), w_heads, hgrn_lb, g_hn, state_hgrn)
    y_sample = _post_mixer(x2, None, o.reshape(bs * ls, D), mod, 1, smp_row, wo1, w1, w2,
                           gf[1], g_fin, final=True).reshape(bs, ls, D)
    return (y_prompt, y_sample, new_state)
```

```python
import functools
import math

import numpy as np
import jax
import jax.numpy as jnp
from jax import lax
from jax.experimental import pallas as pl
from jax.experimental.pallas import tpu as pltpu

F32 = jnp.float32
BF16 = jnp.bfloat16

D = 1024
N_MOD = 6
D_FF = 4 * D
EPS = 1e-6
GROUPS = 4
GD = D // GROUPS
HEADS = 8
DK = 128
GRID_W = 64
POS_BASE = 10000.0
PROJ = 5

SEQ_SLABS = 16
SEQ_PTS = 256
LANES = 128
BF16_ROWS = 16
CHUNK = 64
SCAN_TILE = 256
EXP_SAFE = 85.0
SLAB_SUB = 512
POST_TM = 512
POST_SUB = 256
POST_TF = 1024

VMEM_LIMIT = 56 * 1024 * 1024


def _dft_tables():
    n = np.arange(GD)
    ang = 2.0 * np.pi * ((n[:, None] * n[None, :]) % GD) / GD
    cs_ch = np.concatenate([np.cos(ang), np.sin(ang)], axis=1) / math.sqrt(GD)
    t_ctx = np.concatenate([np.cos(ang), -np.sin(ang)], axis=1) / math.sqrt(GD)
    length = SEQ_PTS * SEQ_SLABS
    k1 = np.arange(SEQ_PTS)[:, None]
    n1 = np.arange(SEQ_PTS)[None, :]
    blocks = []
    for n2 in range(SEQ_SLABS):
        a = 2.0 * np.pi * ((k1 * (SEQ_SLABS * n1 + n2)) % length) / length
        gc, gs = np.cos(a) / math.sqrt(length), np.sin(a) / math.sqrt(length)
        blocks.append(np.block([[gc, -gs], [gs, gc]]))
    g_seq = np.stack(blocks)
    return tuple(jnp.asarray(t, F32).astype(BF16) for t in (cs_ch, t_ctx, g_seq))


def _grid_pos_embed(n_tok, d):
    rows = n_tok // GRID_W
    quarter = d // 4
    omega = 1.0 / (POS_BASE ** (jnp.arange(quarter, dtype=F32) / quarter))
    r = jnp.arange(rows, dtype=F32)[:, None] * omega[None, :]
    cl = jnp.arange(GRID_W, dtype=F32)[:, None] * omega[None, :]
    er = jnp.concatenate([jnp.sin(r), jnp.cos(r)], axis=-1)
    ec = jnp.concatenate([jnp.sin(cl), jnp.cos(cl)], axis=-1)
    emb = jnp.concatenate([jnp.broadcast_to(er[:, None, :], (rows, GRID_W, d // 2)),
                           jnp.broadcast_to(ec[None, :, :], (rows, GRID_W, d // 2))], axis=-1)
    return emb.reshape(n_tok, d)


def _silu(x):
    return x * jax.nn.sigmoid(x)


def _rms(x, g):
    return x * lax.rsqrt(jnp.mean(x * x, axis=-1, keepdims=True) + EPS) * g


def _mod_part(m, idx):
    return m[:, idx * D:(idx + 1) * D]


def _dot(a, b):
    return jnp.dot(a, b, preferred_element_type=F32)


def _dot_nt(a, b):
    return lax.dot_general(a, b, (((1,), (1,)), ((), ())), preferred_element_type=F32)


def _dot_tn(a, b):
    return lax.dot_general(a, b, (((0,), (0,)), ((), ())), preferred_element_type=F32)


def _aligned(x, m):
    return x if isinstance(x, int) else pl.multiple_of(x, m)


def _params(*sem, **kw):
    return pltpu.CompilerParams(dimension_semantics=sem, vmem_limit_bytes=VMEM_LIMIT, **kw)


def _mod_kernel(c_ref, w_ref, b_ref, o_ref):
    s = _silu(c_ref[...])
    o_ref[0] = _dot(s.astype(BF16), w_ref[0].astype(BF16)) + b_ref[0]


def _modulation(cond8, ada_w, ada_b):
    depth = ada_w.shape[0]
    tn = 1536
    out = pl.pallas_call(
        _mod_kernel,
        grid=(depth, N_MOD * D // tn),
        in_specs=[pl.BlockSpec((8, D), lambda l, j: (0, 0)),
                  pl.BlockSpec((1, D, tn), lambda l, j: (l, 0, j)),
                  pl.BlockSpec((1, 1, tn), lambda l, j: (l, 0, j))],
        out_specs=pl.BlockSpec((1, 8, tn), lambda l, j: (l, 0, j)),
        out_shape=jax.ShapeDtypeStruct((depth, 8, N_MOD * D), F32),
        compiler_params=_params("parallel", "parallel"),
        name="adaln_mod",
    )(cond8, ada_w, ada_b.reshape(depth, 1, N_MOD * D))
    return out.reshape(depth, 8, 1, N_MOD * D)


def _channel_dft(hb, cs_ref, put):
    for g in range(GROUPS):
        y = _dot(hb[:, g * GD:(g + 1) * GD], cs_ref[...])
        put(0, g, y[:, :GD].astype(BF16))
        put(1, g, y[:, GD:].astype(BF16))


def _fnet_ctx_kernel(x_ref, mod_ref, g_ref, cs_ref, t_ref, o_ref, y_scr):
    m = mod_ref[0, 0]
    h = _rms(x_ref[0], g_ref[...]) * (1.0 + _mod_part(m, 1)) + _mod_part(m, 0)
    seq = x_ref.shape[1]

    def put(kind, g, val):
        y_scr[kind * seq:(kind + 1) * seq, g * GD:(g + 1) * GD] = val

    _channel_dft(h.astype(BF16), cs_ref, put)
    o_ref[0] = _dot(t_ref[...], y_scr[...]).astype(o_ref.dtype)


def _fnet_ctx(x, mod, layer, gain, cs_ch, t_ctx):
    b, seq, _ = x.shape
    return pl.pallas_call(
        _fnet_ctx_kernel,
        grid=(b,),
        in_specs=[pl.BlockSpec((1, seq, D), lambda i: (i, 0, 0)),
                  pl.BlockSpec((1, 1, 1, N_MOD * D), lambda i: (layer, 0, 0, 0)),
                  pl.BlockSpec((1, D), lambda i: (0, 0)),
                  pl.BlockSpec((GD, 2 * GD), lambda i: (0, 0)),
                  pl.BlockSpec((seq, 2 * seq), lambda i: (0, 0))],
        out_specs=pl.BlockSpec((1, seq, D), lambda i: (i, 0, 0)),
        out_shape=jax.ShapeDtypeStruct((b, seq, D), BF16),
        scratch_shapes=[pltpu.VMEM((2 * seq, D), BF16)],
        compiler_params=_params("parallel"),
        name="fnet_ctx",
    )(x, mod, gain, cs_ch, t_ctx)


def _fnet_slab_kernel(x_ref, pos_ref, mod_ref, g_ref, cs_ref, y_ref, col_scr, hb_scr):
    rows = SLAB_SUB // SEQ_SLABS
    n_col = D // LANES
    m = mod_ref[0, 0]
    for s in range(x_ref.shape[1] // SLAB_SUB):
        sub = slice(s * SLAB_SUB, (s + 1) * SLAB_SUB)
        for j in range(n_col):
            cols = slice(j * LANES, (j + 1) * LANES)
            col_scr[s, j] = x_ref[0, sub, cols] + pos_ref[sub, cols]
        for n2 in range(SEQ_SLABS):
            xs = jnp.concatenate([col_scr[s, j, pl.ds(n2, rows, stride=SEQ_SLABS), :]
                                  for j in range(n_col)], axis=1)
            h = _rms(xs, g_ref[...]) * (1.0 + _mod_part(m, 1)) + _mod_part(m, 0)
            hb_scr[s, n2 * rows:(n2 + 1) * rows, :] = h.astype(BF16)
        dst = slice(s * rows, (s + 1) * rows)
        for g in range(GROUPS):
            y = _dot(hb_scr[s, :, g * GD:(g + 1) * GD], cs_ref[...])
            for n2 in range(SEQ_SLABS):
                src = slice(n2 * rows, (n2 + 1) * rows)
                y_ref[0, n2, 0, dst, g * GD:(g + 1) * GD] = y[src, :GD].astype(BF16)
                y_ref[0, n2, 1, dst, g * GD:(g + 1) * GD] = y[src, GD:].astype(BF16)


def _fft_real_part(xs):
    n = len(xs)
    if n == 1:
        return xs
    ev = _fft_real_part(xs[0::2])
    od = _fft_real_part(xs[1::2])
    out = [None] * n
    for k in range(n // 2):
        c = math.cos(2.0 * math.pi * k / n)
        s = -math.sin(2.0 * math.pi * k / n)
        orr, oi = od[k]
        if k == 0:
            tr, ti = orr, oi
        elif 4 * k == n:
            tr, ti = oi, -orr
        else:
            tr, ti = orr * c - oi * s, orr * s + oi * c
        er, ei = ev[k]
        out[k] = (er + tr, ei + ti)
        out[k + n // 2] = (er - tr, ei - ti)
    return out


def _fnet_seq_kernel(y_ref, g_ref, o_ref, *z_scr):
    tc = o_ref.shape[2]
    s = pl.program_id(0)
    last = pl.num_programs(0) - 1
    rb = 8
    chunks = [(r0, c0) for r0 in range(0, SEQ_PTS, 2 * rb) for c0 in range(0, tc, LANES)]
    per_dot = len(chunks) // SEQ_SLABS

    def slab_dot(z_cur, n2):
        yb = y_ref[0, n2].reshape(2 * SEQ_PTS, tc)
        z_cur[n2] = _dot(g_ref[n2], yb)

    def fft_chunk(z_prev, r0, c0):
        cols = slice(c0, c0 + LANES)
        halves = []
        for r in (r0, r0 + rb):
            xs = [(z_prev[n2, r:r + rb, cols], -z_prev[n2, SEQ_PTS + r:SEQ_PTS + r + rb, cols])
                  for n2 in range(SEQ_SLABS)]
            halves.append(_fft_real_part(xs))
        for k2 in range(SEQ_SLABS):
            both = jnp.concatenate([halves[0][k2][0], halves[1][k2][0]], axis=0)
            o_ref[0, k2 * SEQ_PTS + r0:k2 * SEQ_PTS + r0 + 2 * rb, cols] = both.astype(o_ref.dtype)

    for parity, (z_cur, z_prev) in enumerate(((z_scr[0], z_scr[1]), (z_scr[1], z_scr[0]))):
        mine = (s % 2) == parity

        @pl.when(mine & (s == 0))
        def _():
            for n2 in range(SEQ_SLABS):
                slab_dot(z_cur, n2)

        @pl.when(mine & (s > 0) & (s < last))
        def _():
            for n2 in range(SEQ_SLABS):
                slab_dot(z_cur, n2)
                for r0, c0 in chunks[n2 * per_dot:(n2 + 1) * per_dot]:
                    fft_chunk(z_prev, r0, c0)

        @pl.when(mine & (s == last))
        def _():
            for r0, c0 in chunks:
                fft_chunk(z_prev, r0, c0)


def _fnet_sample(x, pos, mod, layer, gain, cs_ch, g_seq):
    b, seq, _ = x.shape
    tm = 2 * SLAB_SUB
    tc = 256
    y = pl.pallas_call(
        _fnet_slab_kernel,
        grid=(seq // tm, b),
        in_specs=[pl.BlockSpec((1, tm, D), lambda j, i: (i, j, 0)),
                  pl.BlockSpec((tm, D), lambda j, i: (j, 0)),
                  pl.BlockSpec((1, 1, 1, N_MOD * D), lambda j, i: (layer, 1 + i, 0, 0)),
                  pl.BlockSpec((1, D), lambda j, i: (0, 0)),
                  pl.BlockSpec((GD, 2 * GD), lambda j, i: (0, 0))],
        out_specs=pl.BlockSpec((1, SEQ_SLABS, 2, tm // SEQ_SLABS, D), lambda j, i: (i, 0, 0, j, 0)),
        out_shape=jax.ShapeDtypeStruct((b, SEQ_SLABS, 2, SEQ_PTS, D), BF16),
        scratch_shapes=[pltpu.VMEM((tm // SLAB_SUB, D // LANES, SLAB_SUB, LANES), F32),
                        pltpu.VMEM((tm // SLAB_SUB, SLAB_SUB, D), BF16)],
        compiler_params=_params("parallel", "parallel"),
        name="fnet_slab",
    )(x, pos, mod, gain, cs_ch)
    ncb = D // tc
    n_blk = b * ncb
    blk_in = lambda s: jnp.minimum(s, n_blk - 1)
    blk_out = lambda s: jnp.maximum(s - 1, 0)
    return pl.pallas_call(
        _fnet_seq_kernel,
        grid=(n_blk + 1,),
        in_specs=[pl.BlockSpec((1, SEQ_SLABS, 2, SEQ_PTS, tc),
                               lambda s: (blk_in(s) // ncb, 0, 0, 0, blk_in(s) % ncb)),
                  pl.BlockSpec((SEQ_SLABS, 2 * SEQ_PTS, 2 * SEQ_PTS), lambda s: (0, 0, 0),
                               pipeline_mode=pl.Buffered(1))],
        out_specs=pl.BlockSpec((1, seq, tc), lambda s: (blk_out(s) // ncb, 0, blk_out(s) % ncb)),
        out_shape=jax.ShapeDtypeStruct((b, seq, D), BF16),
        scratch_shapes=[pltpu.VMEM((SEQ_SLABS, 2 * SEQ_PTS, tc), F32)] * 2,
        compiler_params=_params("arbitrary"),
        name="fnet_seq",
    )(y, g_seq)


def _post_kernel(*refs, has_pos, final):
    it = iter(refs)
    x_ref = next(it)
    pos_ref = next(it) if has_pos else None
    a_ref, mod_ref, wo_ref, w1_ref, w2_ref, gmlp_ref, gnext_ref = (next(it) for _ in range(7))
    modn_ref = None if final else next(it)
    o_ref = next(it)
    hn_ref = None if final else next(it)
    x1_scr, h_scr, u_scr = (next(it) for _ in range(3))

    m = mod_ref[0, 0]
    subs = [slice(s * POST_SUB, (s + 1) * POST_SUB) for s in range(x_ref.shape[0] // POST_SUB)]

    for r in subs:
        x = x_ref[r, :]
        if has_pos:
            x = x + pos_ref[r, :]
        x1 = x + _mod_part(m, 2) * _dot(a_ref[r, :].astype(BF16), wo_ref[0])
        x1_scr[r, :] = x1
        h = _rms(x1, gmlp_ref[...]) * (1.0 + _mod_part(m, 4)) + _mod_part(m, 3)
        h_scr[r, :] = h.astype(BF16)

    for s, r in enumerate(subs):
        for j in range(D_FF // POST_TF):
            cols = slice(j * POST_TF, (j + 1) * POST_TF)
            u = jnp.maximum(_dot(h_scr[r, :], w1_ref[0, :, cols]), 0.0)
            u_scr[s, :, cols] = (u * u).astype(BF16)
        x2 = x1_scr[r, :] + _mod_part(m, 5) * _dot(u_scr[s], w2_ref[0])
        if final:
            o_ref[r, :] = _rms(x2, gnext_ref[...])
        else:
            o_ref[r, :] = x2
            mn = modn_ref[0, 0]
            hn = _rms(x2, gnext_ref[...]) * (1.0 + _mod_part(mn, 1)) + _mod_part(mn, 0)
            hn_ref[r, :] = hn.astype(BF16)


def _post_mixer(x, pos, a, mod, layer, mod_row, wo, w1, w2, g_mlp, g_next, final):
    rows = x.shape[0]
    tm = 2 * POST_TM if final else POST_TM
    has_pos = pos is not None
    n_pos = pos.shape[0] // tm if has_pos else 1
    row_map = lambda i: (i, 0)
    const = lambda i: (0, 0)
    resident = lambda shape: pl.BlockSpec(shape, lambda i: (layer, 0, 0), pipeline_mode=pl.Buffered(1))
    in_specs = [pl.BlockSpec((tm, D), row_map)]
    args = [x]
    if has_pos:
        in_specs.append(pl.BlockSpec((tm, D), lambda i: (i % n_pos, 0)))
        args.append(pos)
    in_specs += [pl.BlockSpec((tm, D), row_map),
                 pl.BlockSpec((1, 1, 1, N_MOD * D), lambda i: (layer, mod_row(i * tm), 0, 0)),
                 pl.BlockSpec((1, D, D), lambda i: (0, 0, 0), pipeline_mode=pl.Buffered(1)),
                 resident((1, D, D_FF)),
                 resident((1, D_FF, D)),
                 pl.BlockSpec((1, D), const),
                 pl.BlockSpec((1, D), const)]
    args += [a, mod, wo, w1, w2, g_mlp, g_next]
    out_specs = [pl.BlockSpec((tm, D), row_map)]
    out_shape = [jax.ShapeDtypeStruct((rows, D), F32)]
    if not final:
        in_specs.append(pl.BlockSpec((1, 1, 1, N_MOD * D),
                                     lambda i: (layer + 1, mod_row(i * tm), 0, 0)))
        args.append(mod)
        out_specs.append(pl.BlockSpec((tm, D), row_map))
        out_shape.append(jax.ShapeDtypeStruct((rows, D), BF16))
    res = pl.pallas_call(
        functools.partial(_post_kernel, has_pos=has_pos, final=final),
        grid=(rows // tm,),
        in_specs=in_specs,
        out_specs=out_specs,
        out_shape=out_shape,
        scratch_shapes=[pltpu.VMEM((tm, D), F32), pltpu.VMEM((tm, D), BF16),
                        pltpu.VMEM((tm // POST_SUB, POST_SUB, D_FF), BF16)],
        compiler_params=_params("parallel"),
        name="post_mlp_final" if final else "post_mlp",
    )(*args)
    return res[0] if final else (res[0], res[1])


PER_TILE = SCAN_TILE // CHUNK


def _decay_matrices():
    t = np.arange(SCAN_TILE)
    start = (t // CHUNK) * CHUNK
    first = np.arange(PER_TILE) * CHUNK
    same = (t[:, None] // CHUNK) == (t[None, :] // CHUNK)
    out = []
    for incl, mid, last in ((t[None, :] <= t[:, None], CHUNK // 2 - 1, CHUNK - 1),
                            (t[None, :] >= t[:, None], CHUNK // 2, 0)):
        bd = (same & incl).astype(np.float32)
        ref = bd[first + mid]
        pad = np.zeros((BF16_ROWS - 2 * PER_TILE, SCAN_TILE), np.float32)
        out.append(np.concatenate([bd - bd[start + mid], ref, bd[first + last] - ref, pad], axis=0))
    return jnp.asarray(np.stack(out), BF16)


def _hgrn_kernel(*refs, has_s0, independent):
    it = iter(refs)
    hn_ref = next(it)
    wp_refs = [next(it) for _ in range(PROJ)]
    lb_ref, gn_ref, dm_ref = (next(it) for _ in range(3))
    s0_ref = next(it) if has_s0 else None
    o_ref, sf_ref = next(it), next(it)
    qd_scr, kd_scr, qb_scr, kl_scr, g_scr, o_scr, k_scr = ((next(it), next(it)) for _ in range(7))
    ext_scr, q_scr, v_scr, sg_scr, st_scr, u_scr, sb_scr, gs_scr, w_scr, flag_scr = (
        next(it) for _ in range(10))

    @pl.when(pl.program_id(1) == 0)
    def _():
        for j, wp_ref in enumerate(wp_refs):
            w_scr[:, j * DK:(j + 1) * DK] = wp_ref[...].astype(BF16)

    seq = hn_ref.shape[1]
    tp = SCAN_TILE
    n_tiles = seq // tp
    per_tile = PER_TILE

    lraw = lb_ref[...]
    mx = jnp.max(lraw, axis=1, keepdims=True)
    ex = jnp.exp(lraw - mx)
    sm = ex / jnp.sum(ex, axis=1, keepdims=True)
    lbv = (sm[:, 0, :] + sm[:, 1, :]) - sm[:, 0, :]

    def gates(i, count):
        p_all = _dot(hn_ref[0, pl.ds(i * tp, count * tp), :], w_scr[...])
        for t in range(count):
            rows = pl.ds((i + t) * tp, tp)
            p = p_all[t * tp:(t + 1) * tp]
            q_scr[rows, :] = _silu(p[:, 0:DK])
            v_scr[rows, :] = p[:, 3 * DK:4 * DK].astype(BF16)
            sg_scr[rows, :] = _silu(p[:, 4 * DK:5 * DK])
            for d in range(2):
                lb = lbv[d:d + 1, :]
                f = lb + (1.0 - lb) * jax.nn.sigmoid(p[:, (1 + d) * DK:(2 + d) * DK])
                k_scr[d][rows, :] = 1.0 - f
                g = jnp.log(f)
                g_scr[d][rows, :] = g
                g1 = g.astype(BF16)
                gs_scr[(i + t) % 2, d, :, 0:DK] = g1
                gs_scr[(i + t) % 2, d, :, DK:2 * DK] = (g - g1.astype(F32)).astype(BF16)

    def decay_sums(i):
        return [_dot(dm_ref[d], gs_scr[i % 2, d]) for d in range(2)]

    def factors(i, sums):
        rows = pl.ds(_aligned(i * tp, tp), tp)
        q = q_scr[rows, :]
        unsafe = jnp.int32(0)
        for d in range(2):
            r = sums[d][:, :DK] + sums[d][:, DK:]
            a = r[:tp]
            ext = r[tp:]
            ext_scr[d, i] = ext
            unsafe = unsafe | (jnp.max(jnp.abs(a)) > EXP_SAFE).astype(jnp.int32)
            qd = q * jnp.exp(a)
            kd = k_scr[d][rows, :] * jnp.exp(-a)
            qd_scr[d][rows, :] = qd.astype(BF16)
            kd_scr[d][rows, :] = kd.astype(BF16)
            e_ref = jnp.exp(ext[0:per_tile])
            e_lr = jnp.exp(ext[per_tile:2 * per_tile])
            for c in range(per_tile):
                cr = pl.ds(_aligned(i * tp + c * CHUNK, CHUNK), CHUNK)
                cs = slice(c * CHUNK, (c + 1) * CHUNK)
                qb_scr[d][cr, :] = (qd[cs] * e_ref[c:c + 1]).astype(BF16)
                kl_scr[d][cr, :] = (kd[cs] * e_lr[c:c + 1]).astype(BF16)
        flag_scr[i] = unsafe

    def init_states():
        for d in range(2):
            if has_s0:
                st_scr[d] = s0_ref[0, 0, d, 0].T
            else:
                st_scr[d] = jnp.zeros((DK, DK), F32)

    def state_in(d):
        return jnp.zeros((DK, DK), F32) if independent else st_scr[d]

    def state_out(d, tile, st):
        if independent:
            sf_ref[tile, 0, d, 0] = st.T
        else:
            st_scr[d] = st

    if not independent:
        init_states()

    ti = lax.broadcasted_iota(jnp.int32, (tp, tp), 0)
    si = lax.broadcasted_iota(jnp.int32, (tp, tp), 1)
    same = (ti // CHUNK) == (si // CHUNK)
    masks = (same & (ti >= si), same & (si >= ti))

    def scan_head(d, tile, slot=None):
        slot = d if slot is None else slot
        order = range(per_tile) if d == 0 else range(per_tile - 1, -1, -1)
        r0 = _aligned(tile * tp, tp)
        vv = v_scr[pl.ds(r0, tp), :]
        for c in range(per_tile):
            cr = pl.ds(_aligned(r0 + c * CHUNK, CHUNK), CHUNK)
            u_scr[slot, c] = _dot_tn(vv[c * CHUNK:(c + 1) * CHUNK, :], kl_scr[d][cr, :])
        ext = ext_scr[d, tile]
        decay = jnp.exp(ext[0:per_tile] + ext[per_tile:2 * per_tile])
        st = state_in(d)
        for c in order:
            sb_scr[slot, c] = st.astype(BF16)
            st = st * decay[c:c + 1] + u_scr[slot, c]
        state_out(d, tile, st)

    def scan_scores(d, tile):
        rows = pl.ds(_aligned(tile * tp, tp), tp)
        return jnp.where(masks[d], _dot_nt(qd_scr[d][rows, :], kd_scr[d][rows, :]), 0.0)

    def scan_out(d, tile, sc, slot=None):
        slot = d if slot is None else slot
        r0 = _aligned(tile * tp, tp)
        rows = pl.ds(r0, tp)
        o_scr[d][rows, :] = _dot(sc.astype(BF16), v_scr[rows, :])
        for c in range(per_tile):
            cr = pl.ds(_aligned(r0 + c * CHUNK, CHUNK), CHUNK)
            o_scr[d][cr, :] += _dot_nt(qb_scr[d][cr, :], sb_scr[slot, c])

    def scan_tail(d, tile, slot=None):
        scan_out(d, tile, scan_scores(d, tile), slot)

    def scan_tail_pair(a, b):
        sc = [scan_scores(d, tile) for d, tile, _ in (a, b)]
        for (d, tile, slot), s in zip((a, b), sc):
            scan_out(d, tile, s, slot)

    rid = lax.broadcasted_iota(jnp.int32, (BF16_ROWS, DK), 0)

    def scan_exact(d, tile):
        r0 = tile * tp

        def step(j, st):
            t = r0 + ((tp - 1 - j) if d == 1 else j)
            t0 = _aligned((t // BF16_ROWS) * BF16_ROWS, BF16_ROWS)
            grp = pl.ds(t0, BF16_ROWS)
            sel = rid == (t - t0)
            g = g_scr[d][grp, :]
            f = jnp.exp(jnp.sum(jnp.where(sel, g, 0.0), axis=0, keepdims=True))
            k = jnp.where(sel, 1.0 - jnp.exp(g), 0.0)
            st = st * f + _dot_tn(v_scr[grp, :], k.astype(BF16))
            o = _dot_nt(q_scr[grp, :].astype(BF16), st.astype(BF16))
            o_scr[d][grp, :] = jnp.where(sel, o, o_scr[d][grp, :])
            return st

        state_out(d, tile, lax.fori_loop(0, tp, step, state_in(d)))

    def finish(i):
        rows = pl.ds(_aligned(i * tp, tp), tp)
        o = o_scr[0][rows, :] + o_scr[1][rows, :]
        o = o * lax.rsqrt(jnp.mean(o * o, axis=-1, keepdims=True) + EPS) * gn_ref[...]
        o_ref[0, rows, :] = (o * sg_scr[rows, :]).astype(o_ref.dtype)

    dirs = (0, 1) if independent else (0,)

    def scan_heads(t):
        for d in dirs:
            scan_head(d, t)
        return [scan_scores(d, t) for d in dirs]

    def scan_tails(t, scores):
        for d, sc in zip(dirs, scores):
            scan_out(d, t, sc)
        if independent:
            finish(t)

    last = n_tiles - 1
    gates(0, min(2, n_tiles))
    for i in range(n_tiles):
        scores = scan_heads(i - 1) if i > 0 else None
        sums = decay_sums(i)
        if i % 2 == 1 and i + 1 < n_tiles:
            gates(i + 1, min(2, n_tiles - i - 1))
        factors(i, sums)
        if i > 0:
            scan_tails(i - 1, scores)

    if independent:
        scan_tails(last, scan_heads(last))
    else:
        scan_head(0, last)
        scan_head(1, last)
        scan_tail_pair((0, last, 0), (1, last, 1))
        finish(last)
        t0 = last - 1
        if t0 % 2 == 0:
            scan_head(1, t0, 0)
            scan_tail(1, t0, 0)
            finish(t0)
            t0 -= 1

        for t in range(t0, 0, -2):
            scan_head(1, t, 0)
            scan_head(1, t - 1, 1)
            scan_tail_pair((1, t, 0), (1, t - 1, 1))
            finish(t)
            finish(t - 1)

    unsafe = flag_scr[0]
    for t in range(1, n_tiles):
        unsafe = unsafe | flag_scr[t]

    @pl.when(unsafe != 0)
    def _():
        if not independent:
            init_states()

        def resweep(d):
            def body(j, carry):
                t = j if d == 0 else last - j

                @pl.when(flag_scr[t] != 0)
                def _():
                    scan_exact(d, t)

                @pl.when(flag_scr[t] == 0)
                def _():
                    scan_head(d, t)
                    scan_tail(d, t)

                if d == 1:
                    finish(t)
                return carry

            lax.fori_loop(0, n_tiles, body, 0)

        resweep(0)
        resweep(1)

    if not independent:
        for d in range(2):
            sf_ref[0, 0, d, 0] = st_scr[d].T


def _hgrn_mixer(hn, w_in, lb_raw, g_norm, s0):
    n_seq, seq_len, _ = hn.shape
    has_s0 = s0 is not None
    independent = (not has_s0) and seq_len == SCAN_TILE
    if independent:
        hn = hn.reshape(1, n_seq * seq_len, D)
    b, seq, _ = hn.shape
    st_rows = n_seq if independent else 1
    st_spec = pl.BlockSpec((st_rows, 1, 2, 1, DK, DK), lambda h, i: (i, 0, 0, h, 0, 0))
    in_specs = [pl.BlockSpec((1, seq, D), lambda h, i: (i, 0, 0))]
    in_specs += [pl.BlockSpec((D, DK), functools.partial(lambda j, h, i: (0, j * HEADS + h), j))
                 for j in range(PROJ)]
    in_specs += [pl.BlockSpec((2, 2, DK), lambda h, i: (0, 0, h)),
                 pl.BlockSpec((1, DK), lambda h, i: (0, 0)),
                 pl.BlockSpec((2, SCAN_TILE + BF16_ROWS, SCAN_TILE), lambda h, i: (0, 0, 0))]
    args = [hn] + [w_in] * PROJ + [lb_raw, g_norm, _decay_matrices()]
    if has_s0:
        in_specs.append(st_spec)
        args.append(s0)
    n_tiles = seq // SCAN_TILE
    bf16_rows = pltpu.VMEM((seq, DK), BF16)
    f32_rows = pltpu.VMEM((seq, DK), F32)
    o, s_fin = pl.pallas_call(
        functools.partial(_hgrn_kernel, has_s0=has_s0, independent=independent),
        grid=(HEADS, b),
        in_specs=in_specs,
        out_specs=[pl.BlockSpec((1, seq, DK), lambda h, i: (i, 0, h)), st_spec],
        out_shape=[jax.ShapeDtypeStruct((b, seq, D), BF16),
                   jax.ShapeDtypeStruct((n_seq, 1, 2, HEADS, DK, DK), F32)],
        scratch_shapes=[bf16_rows] * 8 + [f32_rows] * 6 + [
                        pltpu.VMEM((2, n_tiles, BF16_ROWS, DK), F32),
                        f32_rows, bf16_rows, f32_rows,
                        pltpu.VMEM((2, DK, DK), F32),
                        pltpu.VMEM((2, PER_TILE, DK, DK), F32),
                        pltpu.VMEM((2, PER_TILE, DK, DK), BF16),
                        pltpu.VMEM((2, 2, SCAN_TILE, 2 * DK), BF16),
                        pltpu.VMEM((D, PROJ * DK), BF16),
                        pltpu.SMEM((n_tiles,), jnp.int32)],
        compiler_params=_params("parallel", "arbitrary"),
        name="hgrn_scan",
    )(*args)
    return o.reshape(n_seq, seq_len, D), s_fin


def kernel(x_prompt, x_sample, state_hgrn, c, c_ctx, ada_w, ada_b, norm_mix, norm_mlp, fnet_wo,
           hgrn_w_in, hgrn_lb, hgrn_norm, hgrn_wo, mlp_w1, mlp_w2, norm_final):
    bp, lp, _ = x_prompt.shape
    bs, ls, _ = x_sample.shape
    assert ada_w.shape[0] == 2 and ls == SEQ_PTS * SEQ_SLABS and lp == GD

    cs_ch, t_ctx, g_seq = _dft_tables()
    pos = _grid_pos_embed(ls, D)

    cond8 = jnp.zeros((8, D), F32).at[0].set(c_ctx).at[1:1 + bs].set(c)
    mod = _modulation(cond8, ada_w, ada_b)

    wo0 = fnet_wo.astype(BF16)
    wo1 = hgrn_wo.astype(BF16)
    w1 = mlp_w1.astype(BF16)
    w2 = mlp_w2.astype(BF16)
    w_heads = hgrn_w_in[0]
    gm = norm_mix.reshape(-1, 1, D)
    gf = norm_mlp.reshape(-1, 1, D)
    g_fin = norm_final.reshape(1, D)
    g_hn = hgrn_norm[0].reshape(1, DK)

    ctx_row = lambda r: 0
    smp_row = lambda r: 1 + r // ls

    mixed = _fnet_ctx(x_prompt, mod, 0, gm[0], cs_ch, t_ctx)
    x2, hn = _post_mixer(x_prompt.reshape(bp * lp, D), None, mixed.reshape(bp * lp, D), mod, 0,
                         ctx_row, wo0, w1, w2, gf[0], gm[1], final=False)
    o, new_state = _hgrn_mixer(hn.reshape(bp, lp, D), w_heads, hgrn_lb, g_hn, None)
    y_prompt = _post_mixer(x2, None, o.reshape(bp * lp, D), mod, 1, ctx_row, wo1, w1, w2,
                           gf[1], g_fin, final=True).reshape(bp, lp, D)

    mixed = _fnet_sample(x_sample, pos, mod, 0, gm[0], cs_ch, g_seq)
    x2, hn = _post_mixer(x_sample.reshape(bs * ls, D), pos, mixed.reshape(bs * ls, D), mod, 0,
                         smp_row, wo0, w1, w2, gf[0], gm[1], final=False)
    o, _ = _hgrn_mixer(hn.reshape(bs, ls, D), w_heads, hgrn_lb, g_hn, state_hgrn)
    y_sample = _post_mixer(x2, None, o.reshape(bs * ls, D), mod, 1, smp_row, wo1, w1, w2,
                           gf[1], g_fin, final=True).reshape(bs, ls, D)
    return (y_prompt, y_sample, new_state)
```

```python
import functools
import math

import numpy as np
import jax
import jax.numpy as jnp
from jax import lax
from jax.experimental import pallas as pl
from jax.experimental.pallas import tpu as pltpu

F32 = jnp.float32
BF16 = jnp.bfloat16

D = 1024
N_MOD = 6
D_FF = 4 * D
EPS = 1e-6
GROUPS = 4
GD = D // GROUPS
HEADS = 8
DK = 128
GRID_W = 64
POS_BASE = 10000.0
PROJ = 5

SEQ_SLABS = 16
SEQ_PTS = 256
LANES = 128
BF16_ROWS = 16
CHUNK = 64
SCAN_TILE = 256
EXP_SAFE = 85.0
SLAB_SUB = 512
POST_TM = 512
POST_SUB = 256
POST_TF = 1024

VMEM_LIMIT = 56 * 1024 * 1024


def _dft_tables():
    n = np.arange(GD)
    ang = 2.0 * np.pi * ((n[:, None] * n[None, :]) % GD) / GD
    cs_ch = np.concatenate([np.cos(ang), np.sin(ang)], axis=1) / math.sqrt(GD)
    t_ctx = np.concatenate([np.cos(ang), -np.sin(ang)], axis=1) / math.sqrt(GD)
    length = SEQ_PTS * SEQ_SLABS
    k1 = np.arange(SEQ_PTS)[:, None]
    n1 = np.arange(SEQ_PTS)[None, :]
    blocks = []
    for n2 in range(SEQ_SLABS):
        a = 2.0 * np.pi * ((k1 * (SEQ_SLABS * n1 + n2)) % length) / length
        gc, gs = np.cos(a) / math.sqrt(length), np.sin(a) / math.sqrt(length)
        blocks.append(np.block([[gc, -gs], [gs, gc]]))
    g_seq = np.stack(blocks)
    return tuple(jnp.asarray(t, F32).astype(BF16) for t in (cs_ch, t_ctx, g_seq))


def _grid_pos_embed(n_tok, d):
    rows = n_tok // GRID_W
    quarter = d // 4
    omega = 1.0 / (POS_BASE ** (jnp.arange(quarter, dtype=F32) / quarter))
    r = jnp.arange(rows, dtype=F32)[:, None] * omega[None, :]
    cl = jnp.arange(GRID_W, dtype=F32)[:, None] * omega[None, :]
    er = jnp.concatenate([jnp.sin(r), jnp.cos(r)], axis=-1)
    ec = jnp.concatenate([jnp.sin(cl), jnp.cos(cl)], axis=-1)
    emb = jnp.concatenate([jnp.broadcast_to(er[:, None, :], (rows, GRID_W, d // 2)),
                           jnp.broadcast_to(ec[None, :, :], (rows, GRID_W, d // 2))], axis=-1)
    return emb.reshape(n_tok, d)


def _silu(x):
    return x * jax.nn.sigmoid(x)


def _rms(x, g):
    return x * lax.rsqrt(jnp.mean(x * x, axis=-1, keepdims=True) + EPS) * g


def _mod_part(m, idx):
    return m[:, idx * D:(idx + 1) * D]


def _dot(a, b):
    return jnp.dot(a, b, preferred_element_type=F32)


def _dot_nt(a, b):
    return lax.dot_general(a, b, (((1,), (1,)), ((), ())), preferred_element_type=F32)


def _dot_tn(a, b):
    return lax.dot_general(a, b, (((0,), (0,)), ((), ())), preferred_element_type=F32)


def _aligned(x, m):
    return x if isinstance(x, int) else pl.multiple_of(x, m)


def _params(*sem, **kw):
    return pltpu.CompilerParams(dimension_semantics=sem, vmem_limit_bytes=VMEM_LIMIT, **kw)


def _mod_kernel(c_ref, w_ref, b_ref, o_ref):
    s = _silu(c_ref[...])
    o_ref[0] = _dot(s.astype(BF16), w_ref[0].astype(BF16)) + b_ref[0]


def _modulation(cond8, ada_w, ada_b):
    depth = ada_w.shape[0]
    tn = 1536
    out = pl.pallas_call(
        _mod_kernel,
        grid=(depth, N_MOD * D // tn),
        in_specs=[pl.BlockSpec((8, D), lambda l, j: (0, 0)),
                  pl.BlockSpec((1, D, tn), lambda l, j: (l, 0, j)),
                  pl.BlockSpec((1, 1, tn), lambda l, j: (l, 0, j))],
        out_specs=pl.BlockSpec((1, 8, tn), lambda l, j: (l, 0, j)),
        out_shape=jax.ShapeDtypeStruct((depth, 8, N_MOD * D), F32),
        compiler_params=_params("parallel", "parallel"),
        name="adaln_mod",
    )(cond8, ada_w, ada_b.reshape(depth, 1, N_MOD * D))
    return out.reshape(depth, 8, 1, N_MOD * D)


def _channel_dft(hb, cs_ref, put):
    for g in range(GROUPS):
        y = _dot(hb[:, g * GD:(g + 1) * GD], cs_ref[...])
        put(0, g, y[:, :GD].astype(BF16))
        put(1, g, y[:, GD:].astype(BF16))


def _fnet_ctx_kernel(x_ref, mod_ref, g_ref, cs_ref, t_ref, o_ref, y_scr):
    m = mod_ref[0, 0]
    n_b, seq = x_ref.shape[0], x_ref.shape[1]
    for bi in range(n_b):
        h = _rms(x_ref[bi], g_ref[...]) * (1.0 + _mod_part(m, 1)) + _mod_part(m, 0)

        def put(kind, g, val, bi=bi):
            y_scr[bi, kind * seq:(kind + 1) * seq, g * GD:(g + 1) * GD] = val

        _channel_dft(h.astype(BF16), cs_ref, put)
    for bi in range(n_b):
        o_ref[bi] = _dot(t_ref[...], y_scr[bi]).astype(o_ref.dtype)


def _fnet_ctx(x, mod, layer, gain, cs_ch, t_ctx):
    b, seq, _ = x.shape
    n_b = 4
    return pl.pallas_call(
        _fnet_ctx_kernel,
        grid=(b // n_b,),
        in_specs=[pl.BlockSpec((n_b, seq, D), lambda i: (i, 0, 0)),
                  pl.BlockSpec((1, 1, 1, N_MOD * D), lambda i: (layer, 0, 0, 0)),
                  pl.BlockSpec((1, D), lambda i: (0, 0)),
                  pl.BlockSpec((GD, 2 * GD), lambda i: (0, 0)),
                  pl.BlockSpec((seq, 2 * seq), lambda i: (0, 0))],
        out_specs=pl.BlockSpec((n_b, seq, D), lambda i: (i, 0, 0)),
        out_shape=jax.ShapeDtypeStruct((b, seq, D), BF16),
        scratch_shapes=[pltpu.VMEM((n_b, 2 * seq, D), BF16)],
        compiler_params=_params("parallel"),
        name="fnet_ctx",
    )(x, mod, gain, cs_ch, t_ctx)


def _fnet_slab_kernel(x_ref, pos_ref, mod_ref, g_ref, cs_ref, y_ref, col_scr, hb_scr):
    rows = SLAB_SUB // SEQ_SLABS
    n_col = D // LANES
    m = mod_ref[0, 0]
    for s in range(x_ref.shape[1] // SLAB_SUB):
        sub = slice(s * SLAB_SUB, (s + 1) * SLAB_SUB)
        for j in range(n_col):
            cols = slice(j * LANES, (j + 1) * LANES)
            col_scr[s, j] = x_ref[0, sub, cols] + pos_ref[sub, cols]
        for n2 in range(SEQ_SLABS):
            xs = jnp.concatenate([col_scr[s, j, pl.ds(n2, rows, stride=SEQ_SLABS), :]
                                  for j in range(n_col)], axis=1)
            h = _rms(xs, g_ref[...]) * (1.0 + _mod_part(m, 1)) + _mod_part(m, 0)
            hb_scr[s, n2 * rows:(n2 + 1) * rows, :] = h.astype(BF16)
        dst = slice(s * rows, (s + 1) * rows)
        for g in range(GROUPS):
            y = _dot(hb_scr[s, :, g * GD:(g + 1) * GD], cs_ref[...])
            for n2 in range(SEQ_SLABS):
                src = slice(n2 * rows, (n2 + 1) * rows)
                y_ref[0, n2, 0, dst, g * GD:(g + 1) * GD] = y[src, :GD].astype(BF16)
                y_ref[0, n2, 1, dst, g * GD:(g + 1) * GD] = y[src, GD:].astype(BF16)


def _fft_real_part(xs):
    n = len(xs)
    if n == 1:
        return xs
    ev = _fft_real_part(xs[0::2])
    od = _fft_real_part(xs[1::2])
    out = [None] * n
    for k in range(n // 2):
        c = math.cos(2.0 * math.pi * k / n)
        s = -math.sin(2.0 * math.pi * k / n)
        orr, oi = od[k]
        if k == 0:
            tr, ti = orr, oi
        elif 4 * k == n:
            tr, ti = oi, -orr
        else:
            tr, ti = orr * c - oi * s, orr * s + oi * c
        er, ei = ev[k]
        out[k] = (er + tr, ei + ti)
        out[k + n // 2] = (er - tr, ei - ti)
    return out


def _fnet_seq_kernel(y_ref, g_ref, o_ref, *z_scr):
    tc = o_ref.shape[2]
    s = pl.program_id(0)
    last = pl.num_programs(0) - 1
    rb = 8
    chunks = [(r0, c0) for r0 in range(0, SEQ_PTS, 2 * rb) for c0 in range(0, tc, LANES)]
    per_dot = len(chunks) // SEQ_SLABS

    def slab_dot(z_cur, n2):
        yb = y_ref[0, n2].reshape(2 * SEQ_PTS, tc)
        z_cur[n2] = _dot(g_ref[n2], yb)

    def fft_chunk(z_prev, r0, c0):
        cols = slice(c0, c0 + LANES)
        halves = []
        for r in (r0, r0 + rb):
            xs = [(z_prev[n2, r:r + rb, cols], -z_prev[n2, SEQ_PTS + r:SEQ_PTS + r + rb, cols])
                  for n2 in range(SEQ_SLABS)]
            halves.append(_fft_real_part(xs))
        for k2 in range(SEQ_SLABS):
            both = jnp.concatenate([halves[0][k2][0], halves[1][k2][0]], axis=0)
            o_ref[0, k2 * SEQ_PTS + r0:k2 * SEQ_PTS + r0 + 2 * rb, cols] = both.astype(o_ref.dtype)

    for parity, (z_cur, z_prev) in enumerate(((z_scr[0], z_scr[1]), (z_scr[1], z_scr[0]))):
        mine = (s % 2) == parity

        @pl.when(mine & (s == 0))
        def _():
            for n2 in range(SEQ_SLABS):
                slab_dot(z_cur, n2)

        @pl.when(mine & (s > 0) & (s < last))
        def _():
            for n2 in range(SEQ_SLABS):
                slab_dot(z_cur, n2)
                for r0, c0 in chunks[n2 * per_dot:(n2 + 1) * per_dot]:
                    fft_chunk(z_prev, r0, c0)

        @pl.when(mine & (s == last))
        def _():
            for r0, c0 in chunks:
                fft_chunk(z_prev, r0, c0)


def _fnet_sample(x, pos, mod, layer, gain, cs_ch, g_seq):
    b, seq, _ = x.shape
    tm = 2 * SLAB_SUB
    tc = 256
    y = pl.pallas_call(
        _fnet_slab_kernel,
        grid=(seq // tm, b),
        in_specs=[pl.BlockSpec((1, tm, D), lambda j, i: (i, j, 0)),
                  pl.BlockSpec((tm, D), lambda j, i: (j, 0)),
                  pl.BlockSpec((1, 1, 1, N_MOD * D), lambda j, i: (layer, 1 + i, 0, 0)),
                  pl.BlockSpec((1, D), lambda j, i: (0, 0)),
                  pl.BlockSpec((GD, 2 * GD), lambda j, i: (0, 0))],
        out_specs=pl.BlockSpec((1, SEQ_SLABS, 2, tm // SEQ_SLABS, D), lambda j, i: (i, 0, 0, j, 0)),
        out_shape=jax.ShapeDtypeStruct((b, SEQ_SLABS, 2, SEQ_PTS, D), BF16),
        scratch_shapes=[pltpu.VMEM((tm // SLAB_SUB, D // LANES, SLAB_SUB, LANES), F32),
                        pltpu.VMEM((tm // SLAB_SUB, SLAB_SUB, D), BF16)],
        compiler_params=_params("parallel", "parallel"),
        name="fnet_slab",
    )(x, pos, mod, gain, cs_ch)
    ncb = D // tc
    n_blk = b * ncb
    blk_in = lambda s: jnp.minimum(s, n_blk - 1)
    blk_out = lambda s: jnp.maximum(s - 1, 0)
    return pl.pallas_call(
        _fnet_seq_kernel,
        grid=(n_blk + 1,),
        in_specs=[pl.BlockSpec((1, SEQ_SLABS, 2, SEQ_PTS, tc),
                               lambda s: (blk_in(s) // ncb, 0, 0, 0, blk_in(s) % ncb)),
                  pl.BlockSpec((SEQ_SLABS, 2 * SEQ_PTS, 2 * SEQ_PTS), lambda s: (0, 0, 0),
                               pipeline_mode=pl.Buffered(1))],
        out_specs=pl.BlockSpec((1, seq, tc), lambda s: (blk_out(s) // ncb, 0, blk_out(s) % ncb)),
        out_shape=jax.ShapeDtypeStruct((b, seq, D), BF16),
        scratch_shapes=[pltpu.VMEM((SEQ_SLABS, 2 * SEQ_PTS, tc), F32)] * 2,
        compiler_params=_params("arbitrary"),
        name="fnet_seq",
    )(y, g_seq)


def _post_kernel(*refs, has_pos, final):
    it = iter(refs)
    x_ref = next(it)
    pos_ref = next(it) if has_pos else None
    a_ref, mod_ref, wo_ref, w1_ref, w2_ref, gmlp_ref, gnext_ref = (next(it) for _ in range(7))
    modn_ref = None if final else next(it)
    o_ref = next(it)
    hn_ref = None if final else next(it)
    x1_scr, h_scr, u_scr = (next(it) for _ in range(3))

    m = mod_ref[0, 0]
    subs = [slice(s * POST_SUB, (s + 1) * POST_SUB) for s in range(x_ref.shape[0] // POST_SUB)]

    for r in subs:
        x = x_ref[r, :]
        if has_pos:
            x = x + pos_ref[r, :]
        x1 = x + _mod_part(m, 2) * _dot(a_ref[r, :].astype(BF16), wo_ref[0])
        x1_scr[r, :] = x1
        h = _rms(x1, gmlp_ref[...]) * (1.0 + _mod_part(m, 4)) + _mod_part(m, 3)
        h_scr[r, :] = h.astype(BF16)

    for s, r in enumerate(subs):
        for j in range(D_FF // POST_TF):
            cols = slice(j * POST_TF, (j + 1) * POST_TF)
            u = jnp.maximum(_dot(h_scr[r, :], w1_ref[0, :, cols]), 0.0)
            u_scr[s, :, cols] = (u * u).astype(BF16)
        x2 = x1_scr[r, :] + _mod_part(m, 5) * _dot(u_scr[s], w2_ref[0])
        if final:
            o_ref[r, :] = _rms(x2, gnext_ref[...])
        else:
            o_ref[r, :] = x2
            mn = modn_ref[0, 0]
            hn = _rms(x2, gnext_ref[...]) * (1.0 + _mod_part(mn, 1)) + _mod_part(mn, 0)
            hn_ref[r, :] = hn.astype(BF16)


def _post_mixer(x, pos, a, mod, layer, mod_row, wo, w1, w2, g_mlp, g_next, final):
    rows = x.shape[0]
    tm = 2 * POST_TM if final else POST_TM
    has_pos = pos is not None
    n_pos = pos.shape[0] // tm if has_pos else 1
    row_map = lambda i: (i, 0)
    const = lambda i: (0, 0)
    resident = lambda shape: pl.BlockSpec(shape, lambda i: (layer, 0, 0), pipeline_mode=pl.Buffered(1))
    in_specs = [pl.BlockSpec((tm, D), row_map)]
    args = [x]
    if has_pos:
        in_specs.append(pl.BlockSpec((tm, D), lambda i: (i % n_pos, 0)))
        args.append(pos)
    in_specs += [pl.BlockSpec((tm, D), row_map),
                 pl.BlockSpec((1, 1, 1, N_MOD * D), lambda i: (layer, mod_row(i * tm), 0, 0)),
                 pl.BlockSpec((1, D, D), lambda i: (0, 0, 0), pipeline_mode=pl.Buffered(1)),
                 resident((1, D, D_FF)),
                 resident((1, D_FF, D)),
                 pl.BlockSpec((1, D), const),
                 pl.BlockSpec((1, D), const)]
    args += [a, mod, wo, w1, w2, g_mlp, g_next]
    out_specs = [pl.BlockSpec((tm, D), row_map)]
    out_shape = [jax.ShapeDtypeStruct((rows, D), F32)]
    if not final:
        in_specs.append(pl.BlockSpec((1, 1, 1, N_MOD * D),
                                     lambda i: (layer + 1, mod_row(i * tm), 0, 0)))
        args.append(mod)
        out_specs.append(pl.BlockSpec((tm, D), row_map))
        out_shape.append(jax.ShapeDtypeStruct((rows, D), BF16))
    res = pl.pallas_call(
        functools.partial(_post_kernel, has_pos=has_pos, final=final),
        grid=(rows // tm,),
        in_specs=in_specs,
        out_specs=out_specs,
        out_shape=out_shape,
        scratch_shapes=[pltpu.VMEM((tm, D), F32), pltpu.VMEM((tm, D), BF16),
                        pltpu.VMEM((tm // POST_SUB, POST_SUB, D_FF), BF16)],
        compiler_params=_params("parallel"),
        name="post_mlp_final" if final else "post_mlp",
    )(*args)
    return res[0] if final else (res[0], res[1])


PER_TILE = SCAN_TILE // CHUNK


def _decay_matrices():
    t = np.arange(SCAN_TILE)
    start = (t // CHUNK) * CHUNK
    first = np.arange(PER_TILE) * CHUNK
    same = (t[:, None] // CHUNK) == (t[None, :] // CHUNK)
    out = []
    for incl, mid, last in ((t[None, :] <= t[:, None], CHUNK // 2 - 1, CHUNK - 1),
                            (t[None, :] >= t[:, None], CHUNK // 2, 0)):
        bd = (same & incl).astype(np.float32)
        ref = bd[first + mid]
        pad = np.zeros((BF16_ROWS - 2 * PER_TILE, SCAN_TILE), np.float32)
        out.append(np.concatenate([bd - bd[start + mid], ref, bd[first + last] - ref, pad], axis=0))
    return jnp.asarray(np.stack(out), BF16)


def _hgrn_kernel(*refs, has_s0, independent):
    it = iter(refs)
    hn_ref = next(it)
    wp_refs = [next(it) for _ in range(PROJ)]
    lb_ref, gn_ref, dm_ref = (next(it) for _ in range(3))
    s0_ref = next(it) if has_s0 else None
    o_ref, sf_ref = next(it), next(it)
    qd_scr, kd_scr, qb_scr, kl_scr, g_scr, o_scr, k_scr = ((next(it), next(it)) for _ in range(7))
    ext_scr, q_scr, v_scr, sg_scr, st_scr, u_scr, sb_scr, gs_scr, w_scr, flag_scr = (
        next(it) for _ in range(10))

    @pl.when(pl.program_id(1) == 0)
    def _():
        for j, wp_ref in enumerate(wp_refs):
            w_scr[:, j * DK:(j + 1) * DK] = wp_ref[...].astype(BF16)

    seq = hn_ref.shape[1]
    tp = SCAN_TILE
    n_tiles = seq // tp
    per_tile = PER_TILE

    lraw = lb_ref[...]
    mx = jnp.max(lraw, axis=1, keepdims=True)
    ex = jnp.exp(lraw - mx)
    sm = ex / jnp.sum(ex, axis=1, keepdims=True)
    lbv = (sm[:, 0, :] + sm[:, 1, :]) - sm[:, 0, :]

    def gates(i, count):
        p_all = _dot(hn_ref[0, pl.ds(i * tp, count * tp), :], w_scr[...])
        for t in range(count):
            rows = pl.ds((i + t) * tp, tp)
            p = p_all[t * tp:(t + 1) * tp]
            q_scr[rows, :] = _silu(p[:, 0:DK])
            v_scr[rows, :] = p[:, 3 * DK:4 * DK].astype(BF16)
            sg_scr[rows, :] = _silu(p[:, 4 * DK:5 * DK])
            for d in range(2):
                lb = lbv[d:d + 1, :]
                f = lb + (1.0 - lb) * jax.nn.sigmoid(p[:, (1 + d) * DK:(2 + d) * DK])
                k_scr[d][rows, :] = 1.0 - f
                g = jnp.log(f)
                g_scr[d][rows, :] = g
                g1 = g.astype(BF16)
                gs_scr[(i + t) % 2, d, :, 0:DK] = g1
                gs_scr[(i + t) % 2, d, :, DK:2 * DK] = (g - g1.astype(F32)).astype(BF16)

    def decay_sums(i):
        return [_dot(dm_ref[d], gs_scr[i % 2, d]) for d in range(2)]

    def factors(i, sums):
        rows = pl.ds(_aligned(i * tp, tp), tp)
        q = q_scr[rows, :]
        unsafe = jnp.int32(0)
        for d in range(2):
            r = sums[d][:, :DK] + sums[d][:, DK:]
            a = r[:tp]
            ext = r[tp:]
            ext_scr[d, i] = ext
            unsafe = unsafe | (jnp.max(jnp.abs(a)) > EXP_SAFE).astype(jnp.int32)
            qd = q * jnp.exp(a)
            kd = k_scr[d][rows, :] * jnp.exp(-a)
            qd_scr[d][rows, :] = qd.astype(BF16)
            kd_scr[d][rows, :] = kd.astype(BF16)
            e_ref = jnp.exp(ext[0:per_tile])
            e_lr = jnp.exp(ext[per_tile:2 * per_tile])
            for c in range(per_tile):
                cr = pl.ds(_aligned(i * tp + c * CHUNK, CHUNK), CHUNK)
                cs = slice(c * CHUNK, (c + 1) * CHUNK)
                qb_scr[d][cr, :] = (qd[cs] * e_ref[c:c + 1]).astype(BF16)
                kl_scr[d][cr, :] = (kd[cs] * e_lr[c:c + 1]).astype(BF16)
        flag_scr[i] = unsafe

    def init_states():
        for d in range(2):
            if has_s0:
                st_scr[d] = s0_ref[0, 0, d, 0].T
            else:
                st_scr[d] = jnp.zeros((DK, DK), F32)

    def state_in(d):
        return jnp.zeros((DK, DK), F32) if independent else st_scr[d]

    def state_out(d, tile, st):
        if independent:
            sf_ref[tile, 0, d, 0] = st.T
        else:
            st_scr[d] = st

    if not independent:
        init_states()

    ti = lax.broadcasted_iota(jnp.int32, (tp, tp), 0)
    si = lax.broadcasted_iota(jnp.int32, (tp, tp), 1)
    same = (ti // CHUNK) == (si // CHUNK)
    masks = (same & (ti >= si), same & (si >= ti))

    def scan_head(d, tile, slot=None):
        slot = d if slot is None else slot
        order = range(per_tile) if d == 0 else range(per_tile - 1, -1, -1)
        r0 = _aligned(tile * tp, tp)
        vv = v_scr[pl.ds(r0, tp), :]
        for c in range(per_tile):
            cr = pl.ds(_aligned(r0 + c * CHUNK, CHUNK), CHUNK)
            u_scr[slot, c] = _dot_tn(vv[c * CHUNK:(c + 1) * CHUNK, :], kl_scr[d][cr, :])
        ext = ext_scr[d, tile]
        decay = jnp.exp(ext[0:per_tile] + ext[per_tile:2 * per_tile])
        st = state_in(d)
        for c in order:
            sb_scr[slot, c] = st.astype(BF16)
            st = st * decay[c:c + 1] + u_scr[slot, c]
        state_out(d, tile, st)

    def scan_scores(d, tile):
        rows = pl.ds(_aligned(tile * tp, tp), tp)
        return jnp.where(masks[d], _dot_nt(qd_scr[d][rows, :], kd_scr[d][rows, :]), 0.0)

    def scan_out(d, tile, sc, slot=None):
        slot = d if slot is None else slot
        r0 = _aligned(tile * tp, tp)
        rows = pl.ds(r0, tp)
        o_scr[d][rows, :] = _dot(sc.astype(BF16), v_scr[rows, :])
        for c in range(per_tile):
            cr = pl.ds(_aligned(r0 + c * CHUNK, CHUNK), CHUNK)
            o_scr[d][cr, :] += _dot_nt(qb_scr[d][cr, :], sb_scr[slot, c])

    def scan_tail(d, tile, slot=None):
        scan_out(d, tile, scan_scores(d, tile), slot)

    def scan_tail_pair(a, b):
        sc = [scan_scores(d, tile) for d, tile, _ in (a, b)]
        for (d, tile, slot), s in zip((a, b), sc):
            scan_out(d, tile, s, slot)

    rid = lax.broadcasted_iota(jnp.int32, (BF16_ROWS, DK), 0)

    def scan_exact(d, tile):
        r0 = tile * tp

        def step(j, st):
            t = r0 + ((tp - 1 - j) if d == 1 else j)
            t0 = _aligned((t // BF16_ROWS) * BF16_ROWS, BF16_ROWS)
            grp = pl.ds(t0, BF16_ROWS)
            sel = rid == (t - t0)
            g = g_scr[d][grp, :]
            f = jnp.exp(jnp.sum(jnp.where(sel, g, 0.0), axis=0, keepdims=True))
            k = jnp.where(sel, 1.0 - jnp.exp(g), 0.0)
            st = st * f + _dot_tn(v_scr[grp, :], k.astype(BF16))
            o = _dot_nt(q_scr[grp, :].astype(BF16), st.astype(BF16))
            o_scr[d][grp, :] = jnp.where(sel, o, o_scr[d][grp, :])
            return st

        state_out(d, tile, lax.fori_loop(0, tp, step, state_in(d)))

    def finish(i):
        rows = pl.ds(_aligned(i * tp, tp), tp)
        o = o_scr[0][rows, :] + o_scr[1][rows, :]
        o = o * lax.rsqrt(jnp.mean(o * o, axis=-1, keepdims=True) + EPS) * gn_ref[...]
        o_ref[0, rows, :] = (o * sg_scr[rows, :]).astype(o_ref.dtype)

    dirs = (0, 1) if independent else (0,)

    def scan_heads(t):
        for d in dirs:
            scan_head(d, t)
        return [scan_scores(d, t) for d in dirs]

    def scan_tails(t, scores):
        for d, sc in zip(dirs, scores):
            scan_out(d, t, sc)
        if independent:
            finish(t)

    last = n_tiles - 1
    gates(0, min(2, n_tiles))
    for i in range(n_tiles):
        scores = scan_heads(i - 1) if i > 0 else None
        sums = decay_sums(i)
        if i % 2 == 1 and i + 1 < n_tiles:
            gates(i + 1, min(2, n_tiles - i - 1))
        factors(i, sums)
        if i > 0:
            scan_tails(i - 1, scores)

    if independent:
        scan_tails(last, scan_heads(last))
    else:
        scan_head(0, last)
        scan_head(1, last)
        scan_tail_pair((0, last, 0), (1, last, 1))
        finish(last)
        t0 = last - 1
        if t0 % 2 == 0:
            scan_head(1, t0, 0)
            scan_tail(1, t0, 0)
            finish(t0)
            t0 -= 1

        for t in range(t0, 0, -2):
            scan_head(1, t, 0)
            scan_head(1, t - 1, 1)
            scan_tail_pair((1, t, 0), (1, t - 1, 1))
            finish(t)
            finish(t - 1)

    unsafe = flag_scr[0]
    for t in range(1, n_tiles):
        unsafe = unsafe | flag_scr[t]

    @pl.when(unsafe != 0)
    def _():
        if not independent:
            init_states()

        def resweep(d):
            def body(j, carry):
                t = j if d == 0 else last - j

                @pl.when(flag_scr[t] != 0)
                def _():
                    scan_exact(d, t)

                @pl.when(flag_scr[t] == 0)
                def _():
                    scan_head(d, t)
                    scan_tail(d, t)

                if d == 1:
                    finish(t)
                return carry

            lax.fori_loop(0, n_tiles, body, 0)

        resweep(0)
        resweep(1)

    if not independent:
        for d in range(2):
            sf_ref[0, 0, d, 0] = st_scr[d].T


def _hgrn_mixer(hn, w_in, lb_raw, g_norm, s0):
    n_seq, seq_len, _ = hn.shape
    has_s0 = s0 is not None
    independent = (not has_s0) and seq_len == SCAN_TILE
    if independent:
        hn = hn.reshape(1, n_seq * seq_len, D)
    b, seq, _ = hn.shape
    st_rows = n_seq if independent else 1
    st_spec = pl.BlockSpec((st_rows, 1, 2, 1, DK, DK), lambda h, i: (i, 0, 0, h, 0, 0))
    in_specs = [pl.BlockSpec((1, seq, D), lambda h, i: (i, 0, 0))]
    in_specs += [pl.BlockSpec((D, DK), functools.partial(lambda j, h, i: (0, j * HEADS + h), j))
                 for j in range(PROJ)]
    in_specs += [pl.BlockSpec((2, 2, DK), lambda h, i: (0, 0, h)),
                 pl.BlockSpec((1, DK), lambda h, i: (0, 0)),
                 pl.BlockSpec((2, SCAN_TILE + BF16_ROWS, SCAN_TILE), lambda h, i: (0, 0, 0))]
    args = [hn] + [w_in] * PROJ + [lb_raw, g_norm, _decay_matrices()]
    if has_s0:
        in_specs.append(st_spec)
        args.append(s0)
    n_tiles = seq // SCAN_TILE
    bf16_rows = pltpu.VMEM((seq, DK), BF16)
    f32_rows = pltpu.VMEM((seq, DK), F32)
    o, s_fin = pl.pallas_call(
        functools.partial(_hgrn_kernel, has_s0=has_s0, independent=independent),
        grid=(HEADS, b),
        in_specs=in_specs,
        out_specs=[pl.BlockSpec((1, seq, DK), lambda h, i: (i, 0, h)), st_spec],
        out_shape=[jax.ShapeDtypeStruct((b, seq, D), BF16),
                   jax.ShapeDtypeStruct((n_seq, 1, 2, HEADS, DK, DK), F32)],
        scratch_shapes=[bf16_rows] * 8 + [f32_rows] * 6 + [
                        pltpu.VMEM((2, n_tiles, BF16_ROWS, DK), F32),
                        f32_rows, bf16_rows, f32_rows,
                        pltpu.VMEM((2, DK, DK), F32),
                        pltpu.VMEM((2, PER_TILE, DK, DK), F32),
                        pltpu.VMEM((2, PER_TILE, DK, DK), BF16),
                        pltpu.VMEM((2, 2, SCAN_TILE, 2 * DK), BF16),
                        pltpu.VMEM((D, PROJ * DK), BF16),
                        pltpu.SMEM((n_tiles,), jnp.int32)],
        compiler_params=_params("parallel", "arbitrary"),
        name="hgrn_scan",
    )(*args)
    return o.reshape(n_seq, seq_len, D), s_fin


def kernel(x_prompt, x_sample, state_hgrn, c, c_ctx, ada_w, ada_b, norm_mix, norm_mlp, fnet_wo,
           hgrn_w_in, hgrn_lb, hgrn_norm, hgrn_wo, mlp_w1, mlp_w2, norm_final):
    bp, lp, _ = x_prompt.shape
    bs, ls, _ = x_sample.shape
    assert ada_w.shape[0] == 2 and ls == SEQ_PTS * SEQ_SLABS and lp == GD

    cs_ch, t_ctx, g_seq = _dft_tables()
    pos = _grid_pos_embed(ls, D)

    cond8 = jnp.zeros((8, D), F32).at[0].set(c_ctx).at[1:1 + bs].set(c)
    mod = _modulation(cond8, ada_w, ada_b)

    wo0 = fnet_wo.astype(BF16)
    wo1 = hgrn_wo.astype(BF16)
    w1 = mlp_w1.astype(BF16)
    w2 = mlp_w2.astype(BF16)
    w_heads = hgrn_w_in[0]
    gm = norm_mix.reshape(-1, 1, D)
    gf = norm_mlp.reshape(-1, 1, D)
    g_fin = norm_final.reshape(1, D)
    g_hn = hgrn_norm[0].reshape(1, DK)

    ctx_row = lambda r: 0
    smp_row = lambda r: 1 + r // ls

    mixed = _fnet_ctx(x_prompt, mod, 0, gm[0], cs_ch, t_ctx)
    x2, hn = _post_mixer(x_prompt.reshape(bp * lp, D), None, mixed.reshape(bp * lp, D), mod, 0,
                         ctx_row, wo0, w1, w2, gf[0], gm[1], final=False)
    o, new_state = _hgrn_mixer(hn.reshape(bp, lp, D), w_heads, hgrn_lb, g_hn, None)
    y_prompt = _post_mixer(x2, None, o.reshape(bp * lp, D), mod, 1, ctx_row, wo1, w1, w2,
                           gf[1], g_fin, final=True).reshape(bp, lp, D)

    mixed = _fnet_sample(x_sample, pos, mod, 0, gm[0], cs_ch, g_seq)
    x2, hn = _post_mixer(x_sample.reshape(bs * ls, D), pos, mixed.reshape(bs * ls, D), mod, 0,
                         smp_row, wo0, w1, w2, gf[0], gm[1], final=False)
    o, _ = _hgrn_mixer(hn.reshape(bs, ls, D), w_heads, hgrn_lb, g_hn, state_hgrn)
    y_sample = _post_mixer(x2, None, o.reshape(bs * ls, D), mod, 1, smp_row, wo1, w1, w2,
                           gf[1], g_fin, final=True).reshape(bs, ls, D)
    return (y_prompt, y_sample, new_state)
```

```python
import functools
import math

import numpy as np
import jax
import jax.numpy as jnp
from jax import lax
from jax.experimental import pallas as pl
from jax.experimental.pallas import tpu as pltpu

F32 = jnp.float32
BF16 = jnp.bfloat16

D = 1024
N_MOD = 6
D_FF = 4 * D
EPS = 1e-6
GROUPS = 4
GD = D // GROUPS
HEADS = 8
DK = 128
GRID_W = 64
POS_BASE = 10000.0
PROJ = 5

SEQ_SLABS = 16
SEQ_PTS = 256
LANES = 128
BF16_ROWS = 16
CHUNK = 64
SCAN_TILE = 256
EXP_SAFE = 85.0
SLAB_SUB = 512
POST_TM = 512
POST_SUB = 256
POST_TF = 1024

VMEM_LIMIT = 56 * 1024 * 1024


def _dft_tables():
    n = np.arange(GD)
    ang = 2.0 * np.pi * ((n[:, None] * n[None, :]) % GD) / GD
    cs_ch = np.concatenate([np.cos(ang), np.sin(ang)], axis=1) / math.sqrt(GD)
    t_ctx = np.concatenate([np.cos(ang), -np.sin(ang)], axis=1) / math.sqrt(GD)
    length = SEQ_PTS * SEQ_SLABS
    k1 = np.arange(SEQ_PTS)[:, None]
    n1 = np.arange(SEQ_PTS)[None, :]
    blocks = []
    for n2 in range(SEQ_SLABS):
        a = 2.0 * np.pi * ((k1 * (SEQ_SLABS * n1 + n2)) % length) / length
        gc, gs = np.cos(a) / math.sqrt(length), np.sin(a) / math.sqrt(length)
        blocks.append(np.block([[gc, -gs], [gs, gc]]))
    g_seq = np.stack(blocks)
    return tuple(jnp.asarray(t, F32).astype(BF16) for t in (cs_ch, t_ctx, g_seq))


def _grid_pos_embed(n_tok, d):
    rows = n_tok // GRID_W
    quarter = d // 4
    omega = 1.0 / (POS_BASE ** (jnp.arange(quarter, dtype=F32) / quarter))
    r = jnp.arange(rows, dtype=F32)[:, None] * omega[None, :]
    cl = jnp.arange(GRID_W, dtype=F32)[:, None] * omega[None, :]
    er = jnp.concatenate([jnp.sin(r), jnp.cos(r)], axis=-1)
    ec = jnp.concatenate([jnp.sin(cl), jnp.cos(cl)], axis=-1)
    emb = jnp.concatenate([jnp.broadcast_to(er[:, None, :], (rows, GRID_W, d // 2)),
                           jnp.broadcast_to(ec[None, :, :], (rows, GRID_W, d // 2))], axis=-1)
    return emb.reshape(n_tok, d)


def _silu(x):
    return x * jax.nn.sigmoid(x)


def _rms(x, g):
    return x * lax.rsqrt(jnp.mean(x * x, axis=-1, keepdims=True) + EPS) * g


def _mod_part(m, idx):
    return m[:, idx * D:(idx + 1) * D]


def _dot(a, b):
    return jnp.dot(a, b, preferred_element_type=F32)


def _dot_nt(a, b):
    return lax.dot_general(a, b, (((1,), (1,)), ((), ())), preferred_element_type=F32)


def _dot_tn(a, b):
    return lax.dot_general(a, b, (((0,), (0,)), ((), ())), preferred_element_type=F32)


def _aligned(x, m):
    return x if isinstance(x, int) else pl.multiple_of(x, m)


def _params(*sem, **kw):
    return pltpu.CompilerParams(dimension_semantics=sem, vmem_limit_bytes=VMEM_LIMIT, **kw)


def _mod_kernel(c_ref, w_ref, b_ref, o_ref):
    s = _silu(c_ref[...])
    o_ref[0] = _dot(s.astype(BF16), w_ref[0].astype(BF16)) + b_ref[0]


def _modulation(cond8, ada_w, ada_b):
    depth = ada_w.shape[0]
    tn = 1536
    out = pl.pallas_call(
        _mod_kernel,
        grid=(depth, N_MOD * D // tn),
        in_specs=[pl.BlockSpec((8, D), lambda l, j: (0, 0)),
                  pl.BlockSpec((1, D, tn), lambda l, j: (l, 0, j)),
                  pl.BlockSpec((1, 1, tn), lambda l, j: (l, 0, j))],
        out_specs=pl.BlockSpec((1, 8, tn), lambda l, j: (l, 0, j)),
        out_shape=jax.ShapeDtypeStruct((depth, 8, N_MOD * D), F32),
        compiler_params=_params("parallel", "parallel"),
        name="adaln_mod",
    )(cond8, ada_w, ada_b.reshape(depth, 1, N_MOD * D))
    return out.reshape(depth, 8, 1, N_MOD * D)


def _channel_dft(hb, cs_ref, put):
    for g in range(GROUPS):
        y = _dot(hb[:, g * GD:(g + 1) * GD], cs_ref[...])
        put(0, g, y[:, :GD].astype(BF16))
        put(1, g, y[:, GD:].astype(BF16))


def _fnet_ctx_kernel(x_ref, mod_ref, g_ref, cs_ref, t_ref, o_ref, y_scr):
    m = mod_ref[0, 0]
    n_b, seq = x_ref.shape[0], x_ref.shape[1]
    for bi in range(n_b):
        h = _rms(x_ref[bi], g_ref[...]) * (1.0 + _mod_part(m, 1)) + _mod_part(m, 0)

        def put(kind, g, val, bi=bi):
            y_scr[bi, kind * seq:(kind + 1) * seq, g * GD:(g + 1) * GD] = val

        _channel_dft(h.astype(BF16), cs_ref, put)
    for bi in range(n_b):
        o_ref[bi] = _dot(t_ref[...], y_scr[bi]).astype(o_ref.dtype)


def _fnet_ctx(x, mod, layer, gain, cs_ch, t_ctx):
    b, seq, _ = x.shape
    n_b = 4
    return pl.pallas_call(
        _fnet_ctx_kernel,
        grid=(b // n_b,),
        in_specs=[pl.BlockSpec((n_b, seq, D), lambda i: (i, 0, 0)),
                  pl.BlockSpec((1, 1, 1, N_MOD * D), lambda i: (layer, 0, 0, 0)),
                  pl.BlockSpec((1, D), lambda i: (0, 0)),
                  pl.BlockSpec((GD, 2 * GD), lambda i: (0, 0)),
                  pl.BlockSpec((seq, 2 * seq), lambda i: (0, 0))],
        out_specs=pl.BlockSpec((n_b, seq, D), lambda i: (i, 0, 0)),
        out_shape=jax.ShapeDtypeStruct((b, seq, D), BF16),
        scratch_shapes=[pltpu.VMEM((n_b, 2 * seq, D), BF16)],
        compiler_params=_params("parallel"),
        name="fnet_ctx",
    )(x, mod, gain, cs_ch, t_ctx)


def _fnet_slab_kernel(x_ref, pos_ref, mod_ref, g_ref, cs_ref, y_ref, col_scr, hb_scr):
    rows = SLAB_SUB // SEQ_SLABS
    n_col = D // LANES
    m = mod_ref[0, 0]
    for s in range(x_ref.shape[1] // SLAB_SUB):
        sub = slice(s * SLAB_SUB, (s + 1) * SLAB_SUB)
        for j in range(n_col):
            cols = slice(j * LANES, (j + 1) * LANES)
            col_scr[s, j] = x_ref[0, sub, cols] + pos_ref[sub, cols]
        for n2 in range(SEQ_SLABS):
            xs = jnp.concatenate([col_scr[s, j, pl.ds(n2, rows, stride=SEQ_SLABS), :]
                                  for j in range(n_col)], axis=1)
            h = _rms(xs, g_ref[...]) * (1.0 + _mod_part(m, 1)) + _mod_part(m, 0)
            hb_scr[s, n2 * rows:(n2 + 1) * rows, :] = h.astype(BF16)
        dst = slice(s * rows, (s + 1) * rows)
        for g in range(GROUPS):
            y = _dot(hb_scr[s, :, g * GD:(g + 1) * GD], cs_ref[...])
            for n2 in range(SEQ_SLABS):
                src = slice(n2 * rows, (n2 + 1) * rows)
                y_ref[0, n2, 0, dst, g * GD:(g + 1) * GD] = y[src, :GD].astype(BF16)
                y_ref[0, n2, 1, dst, g * GD:(g + 1) * GD] = y[src, GD:].astype(BF16)


def _fft_real_part(xs):
    n = len(xs)
    if n == 1:
        return xs
    ev = _fft_real_part(xs[0::2])
    od = _fft_real_part(xs[1::2])
    out = [None] * n
    for k in range(n // 2):
        c = math.cos(2.0 * math.pi * k / n)
        s = -math.sin(2.0 * math.pi * k / n)
        orr, oi = od[k]
        if k == 0:
            tr, ti = orr, oi
        elif 4 * k == n:
            tr, ti = oi, -orr
        else:
            tr, ti = orr * c - oi * s, orr * s + oi * c
        er, ei = ev[k]
        out[k] = (er + tr, ei + ti)
        out[k + n // 2] = (er - tr, ei - ti)
    return out


def _fnet_seq_kernel(y_ref, g_ref, o_ref, *z_scr):
    tc = o_ref.shape[2]
    s = pl.program_id(0)
    last = pl.num_programs(0) - 1
    rb = 8
    chunks = [(r0, c0) for r0 in range(0, SEQ_PTS, 2 * rb) for c0 in range(0, tc, LANES)]
    per_dot = len(chunks) // SEQ_SLABS

    def slab_dot(z_cur, n2):
        yb = y_ref[0, n2].reshape(2 * SEQ_PTS, tc)
        z_cur[n2] = _dot(g_ref[n2], yb)

    def fft_chunk(z_prev, r0, c0):
        cols = slice(c0, c0 + LANES)
        halves = []
        for r in (r0, r0 + rb):
            xs = [(z_prev[n2, r:r + rb, cols], -z_prev[n2, SEQ_PTS + r:SEQ_PTS + r + rb, cols])
                  for n2 in range(SEQ_SLABS)]
            halves.append(_fft_real_part(xs))
        for k2 in range(SEQ_SLABS):
            both = jnp.concatenate([halves[0][k2][0], halves[1][k2][0]], axis=0)
            o_ref[0, k2 * SEQ_PTS + r0:k2 * SEQ_PTS + r0 + 2 * rb, cols] = both.astype(o_ref.dtype)

    for parity, (z_cur, z_prev) in enumerate(((z_scr[0], z_scr[1]), (z_scr[1], z_scr[0]))):
        mine = (s % 2) == parity

        @pl.when(mine & (s == 0))
        def _():
            for n2 in range(SEQ_SLABS):
                slab_dot(z_cur, n2)

        @pl.when(mine & (s > 0) & (s < last))
        def _():
            for n2 in range(SEQ_SLABS):
                slab_dot(z_cur, n2)
                for r0, c0 in chunks[n2 * per_dot:(n2 + 1) * per_dot]:
                    fft_chunk(z_prev, r0, c0)

        @pl.when(mine & (s == last))
        def _():
            for r0, c0 in chunks:
                fft_chunk(z_prev, r0, c0)


def _fnet_sample(x, pos, mod, layer, gain, cs_ch, g_seq):
    b, seq, _ = x.shape
    tm = 2 * SLAB_SUB
    tc = 256
    y = pl.pallas_call(
        _fnet_slab_kernel,
        grid=(seq // tm, b),
        in_specs=[pl.BlockSpec((1, tm, D), lambda j, i: (i, j, 0)),
                  pl.BlockSpec((tm, D), lambda j, i: (j, 0)),
                  pl.BlockSpec((1, 1, 1, N_MOD * D), lambda j, i: (layer, 1 + i, 0, 0)),
                  pl.BlockSpec((1, D), lambda j, i: (0, 0)),
                  pl.BlockSpec((GD, 2 * GD), lambda j, i: (0, 0))],
        out_specs=pl.BlockSpec((1, SEQ_SLABS, 2, tm // SEQ_SLABS, D), lambda j, i: (i, 0, 0, j, 0)),
        out_shape=jax.ShapeDtypeStruct((b, SEQ_SLABS, 2, SEQ_PTS, D), BF16),
        scratch_shapes=[pltpu.VMEM((tm // SLAB_SUB, D // LANES, SLAB_SUB, LANES), F32),
                        pltpu.VMEM((tm // SLAB_SUB, SLAB_SUB, D), BF16)],
        compiler_params=_params("parallel", "parallel"),
        name="fnet_slab",
    )(x, pos, mod, gain, cs_ch)
    ncb = D // tc
    n_blk = b * ncb
    blk_in = lambda s: jnp.minimum(s, n_blk - 1)
    blk_out = lambda s: jnp.maximum(s - 1, 0)
    return pl.pallas_call(
        _fnet_seq_kernel,
        grid=(n_blk + 1,),
        in_specs=[pl.BlockSpec((1, SEQ_SLABS, 2, SEQ_PTS, tc),
                               lambda s: (blk_in(s) // ncb, 0, 0, 0, blk_in(s) % ncb)),
                  pl.BlockSpec((SEQ_SLABS, 2 * SEQ_PTS, 2 * SEQ_PTS), lambda s: (0, 0, 0),
                               pipeline_mode=pl.Buffered(1))],
        out_specs=pl.BlockSpec((1, seq, tc), lambda s: (blk_out(s) // ncb, 0, blk_out(s) % ncb)),
        out_shape=jax.ShapeDtypeStruct((b, seq, D), BF16),
        scratch_shapes=[pltpu.VMEM((SEQ_SLABS, 2 * SEQ_PTS, tc), F32)] * 2,
        compiler_params=_params("arbitrary"),
        name="fnet_seq",
    )(y, g_seq)


def _post_kernel(*refs, has_pos, final):
    it = iter(refs)
    x_ref = next(it)
    pos_ref = next(it) if has_pos else None
    a_ref, mod_ref, wo_ref, w1_ref, w2_ref, gmlp_ref, gnext_ref = (next(it) for _ in range(7))
    modn_ref = None if final else next(it)
    o_ref = next(it)
    hn_ref = None if final else next(it)
    x1_scr, h_scr, u_scr = (next(it) for _ in range(3))

    m = mod_ref[0, 0]
    subs = [slice(s * POST_SUB, (s + 1) * POST_SUB) for s in range(x_ref.shape[0] // POST_SUB)]

    for r in subs:
        x = x_ref[r, :]
        if has_pos:
            x = x + pos_ref[r, :]
        x1 = x + _mod_part(m, 2) * _dot(a_ref[r, :].astype(BF16), wo_ref[0])
        x1_scr[r, :] = x1
        h = _rms(x1, gmlp_ref[...]) * (1.0 + _mod_part(m, 4)) + _mod_part(m, 3)
        h_scr[r, :] = h.astype(BF16)

    for s, r in enumerate(subs):
        for j in range(D_FF // POST_TF):
            cols = slice(j * POST_TF, (j + 1) * POST_TF)
            u = jnp.maximum(_dot(h_scr[r, :], w1_ref[0, :, cols]), 0.0)
            u_scr[s, :, cols] = (u * u).astype(BF16)
        x2 = x1_scr[r, :] + _mod_part(m, 5) * _dot(u_scr[s], w2_ref[0])
        if final:
            o_ref[r, :] = _rms(x2, gnext_ref[...])
        else:
            o_ref[r, :] = x2
            mn = modn_ref[0, 0]
            hn = _rms(x2, gnext_ref[...]) * (1.0 + _mod_part(mn, 1)) + _mod_part(mn, 0)
            hn_ref[r, :] = hn.astype(BF16)


def _post_mixer(x, pos, a, mod, layer, mod_row, wo, w1, w2, g_mlp, g_next, final):
    rows = x.shape[0]
    tm = 2 * POST_TM if final else POST_TM
    has_pos = pos is not None
    n_pos = pos.shape[0] // tm if has_pos else 1
    row_map = lambda i: (i, 0)
    const = lambda i: (0, 0)
    resident = lambda shape: pl.BlockSpec(shape, lambda i: (layer, 0, 0), pipeline_mode=pl.Buffered(1))
    in_specs = [pl.BlockSpec((tm, D), row_map)]
    args = [x]
    if has_pos:
        in_specs.append(pl.BlockSpec((tm, D), lambda i: (i % n_pos, 0)))
        args.append(pos)
    in_specs += [pl.BlockSpec((tm, D), row_map),
                 pl.BlockSpec((1, 1, 1, N_MOD * D), lambda i: (layer, mod_row(i * tm), 0, 0)),
                 pl.BlockSpec((1, D, D), lambda i: (0, 0, 0), pipeline_mode=pl.Buffered(1)),
                 resident((1, D, D_FF)),
                 resident((1, D_FF, D)),
                 pl.BlockSpec((1, D), const),
                 pl.BlockSpec((1, D), const)]
    args += [a, mod, wo, w1, w2, g_mlp, g_next]
    out_specs = [pl.BlockSpec((tm, D), row_map)]
    out_shape = [jax.ShapeDtypeStruct((rows, D), F32)]
    if not final:
        in_specs.append(pl.BlockSpec((1, 1, 1, N_MOD * D),
                                     lambda i: (layer + 1, mod_row(i * tm), 0, 0)))
        args.append(mod)
        out_specs.append(pl.BlockSpec((tm, D), row_map))
        out_shape.append(jax.ShapeDtypeStruct((rows, D), BF16))
    res = pl.pallas_call(
        functools.partial(_post_kernel, has_pos=has_pos, final=final),
        grid=(rows // tm,),
        in_specs=in_specs,
        out_specs=out_specs,
        out_shape=out_shape,
        scratch_shapes=[pltpu.VMEM((tm, D), F32), pltpu.VMEM((tm, D), BF16),
                        pltpu.VMEM((tm // POST_SUB, POST_SUB, D_FF), BF16)],
        compiler_params=_params("parallel",
                                allow_input_fusion=[arg is wo or arg is w1 or arg is w2 for arg in args]),
        name="post_mlp_final" if final else "post_mlp",
    )(*args)
    return res[0] if final else (res[0], res[1])


PER_TILE = SCAN_TILE // CHUNK


def _decay_matrices():
    t = np.arange(SCAN_TILE)
    start = (t // CHUNK) * CHUNK
    first = np.arange(PER_TILE) * CHUNK
    same = (t[:, None] // CHUNK) == (t[None, :] // CHUNK)
    out = []
    for incl, mid, last in ((t[None, :] <= t[:, None], CHUNK // 2 - 1, CHUNK - 1),
                            (t[None, :] >= t[:, None], CHUNK // 2, 0)):
        bd = (same & incl).astype(np.float32)
        ref = bd[first + mid]
        pad = np.zeros((BF16_ROWS - 2 * PER_TILE, SCAN_TILE), np.float32)
        out.append(np.concatenate([bd - bd[start + mid], ref, bd[first + last] - ref, pad], axis=0))
    return jnp.asarray(np.stack(out), BF16)


def _hgrn_kernel(*refs, has_s0, independent):
    it = iter(refs)
    hn_ref = next(it)
    wp_refs = [next(it) for _ in range(PROJ)]
    lb_ref, gn_ref, dm_ref = (next(it) for _ in range(3))
    s0_ref = next(it) if has_s0 else None
    o_ref, sf_ref = next(it), next(it)
    qd_scr, kd_scr, qb_scr, kl_scr, g_scr, o_scr, k_scr = ((next(it), next(it)) for _ in range(7))
    ext_scr, q_scr, v_scr, sg_scr, st_scr, u_scr, sb_scr, gs_scr, w_scr, flag_scr = (
        next(it) for _ in range(10))

    @pl.when(pl.program_id(1) == 0)
    def _():
        for j, wp_ref in enumerate(wp_refs):
            w_scr[:, j * DK:(j + 1) * DK] = wp_ref[...].astype(BF16)

    seq = hn_ref.shape[1]
    tp = SCAN_TILE
    n_tiles = seq // tp
    per_tile = PER_TILE

    lraw = lb_ref[...]
    mx = jnp.max(lraw, axis=1, keepdims=True)
    ex = jnp.exp(lraw - mx)
    sm = ex / jnp.sum(ex, axis=1, keepdims=True)
    lbv = (sm[:, 0, :] + sm[:, 1, :]) - sm[:, 0, :]

    def gates(i, count):
        p_all = _dot(hn_ref[0, pl.ds(i * tp, count * tp), :], w_scr[...])
        for t in range(count):
            rows = pl.ds((i + t) * tp, tp)
            p = p_all[t * tp:(t + 1) * tp]
            q_scr[rows, :] = _silu(p[:, 0:DK])
            v_scr[rows, :] = p[:, 3 * DK:4 * DK].astype(BF16)
            sg_scr[rows, :] = _silu(p[:, 4 * DK:5 * DK])
            for d in range(2):
                lb = lbv[d:d + 1, :]
                f = lb + (1.0 - lb) * jax.nn.sigmoid(p[:, (1 + d) * DK:(2 + d) * DK])
                k_scr[d][rows, :] = 1.0 - f
                g = jnp.log(f)
                g_scr[d][rows, :] = g
                g1 = g.astype(BF16)
                gs_scr[(i + t) % 2, d, :, 0:DK] = g1
                gs_scr[(i + t) % 2, d, :, DK:2 * DK] = (g - g1.astype(F32)).astype(BF16)

    def decay_sums(i):
        return [_dot(dm_ref[d], gs_scr[i % 2, d]) for d in range(2)]

    def factors(i, sums):
        rows = pl.ds(_aligned(i * tp, tp), tp)
        q = q_scr[rows, :]
        unsafe = jnp.int32(0)
        for d in range(2):
            r = sums[d][:, :DK] + sums[d][:, DK:]
            a = r[:tp]
            ext = r[tp:]
            ext_scr[d, i] = ext
            unsafe = unsafe | (jnp.max(jnp.abs(a)) > EXP_SAFE).astype(jnp.int32)
            qd = q * jnp.exp(a)
            kd = k_scr[d][rows, :] * jnp.exp(-a)
            qd_scr[d][rows, :] = qd.astype(BF16)
            kd_scr[d][rows, :] = kd.astype(BF16)
            e_ref = jnp.exp(ext[0:per_tile])
            e_lr = jnp.exp(ext[per_tile:2 * per_tile])
            for c in range(per_tile):
                cr = pl.ds(_aligned(i * tp + c * CHUNK, CHUNK), CHUNK)
                cs = slice(c * CHUNK, (c + 1) * CHUNK)
                qb_scr[d][cr, :] = (qd[cs] * e_ref[c:c + 1]).astype(BF16)
                kl_scr[d][cr, :] = (kd[cs] * e_lr[c:c + 1]).astype(BF16)
        flag_scr[i] = unsafe

    def init_states():
        for d in range(2):
            if has_s0:
                st_scr[d] = s0_ref[0, 0, d, 0].T
            else:
                st_scr[d] = jnp.zeros((DK, DK), F32)

    def state_in(d):
        return jnp.zeros((DK, DK), F32) if independent else st_scr[d]

    def state_out(d, tile, st):
        if independent:
            sf_ref[tile, 0, d, 0] = st.T
        else:
            st_scr[d] = st

    if not independent:
        init_states()

    ti = lax.broadcasted_iota(jnp.int32, (tp, tp), 0)
    si = lax.broadcasted_iota(jnp.int32, (tp, tp), 1)
    same = (ti // CHUNK) == (si // CHUNK)
    masks = (same & (ti >= si), same & (si >= ti))

    def scan_head(d, tile, slot=None):
        slot = d if slot is None else slot
        order = range(per_tile) if d == 0 else range(per_tile - 1, -1, -1)
        r0 = _aligned(tile * tp, tp)
        vv = v_scr[pl.ds(r0, tp), :]
        for c in range(per_tile):
            cr = pl.ds(_aligned(r0 + c * CHUNK, CHUNK), CHUNK)
            u_scr[slot, c] = _dot_tn(vv[c * CHUNK:(c + 1) * CHUNK, :], kl_scr[d][cr, :])
        ext = ext_scr[d, tile]
        decay = jnp.exp(ext[0:per_tile] + ext[per_tile:2 * per_tile])
        st = state_in(d)
        for c in order:
            sb_scr[slot, c] = st.astype(BF16)
            st = st * decay[c:c + 1] + u_scr[slot, c]
        state_out(d, tile, st)

    def scan_scores(d, tile):
        rows = pl.ds(_aligned(tile * tp, tp), tp)
        return jnp.where(masks[d], _dot_nt(qd_scr[d][rows, :], kd_scr[d][rows, :]), 0.0)

    def scan_out(d, tile, sc, slot=None):
        slot = d if slot is None else slot
        r0 = _aligned(tile * tp, tp)
        rows = pl.ds(r0, tp)
        o_scr[d][rows, :] = _dot(sc.astype(BF16), v_scr[rows, :])
        for c in range(per_tile):
            cr = pl.ds(_aligned(r0 + c * CHUNK, CHUNK), CHUNK)
            o_scr[d][cr, :] += _dot_nt(qb_scr[d][cr, :], sb_scr[slot, c])

    def scan_tail(d, tile, slot=None):
        scan_out(d, tile, scan_scores(d, tile), slot)

    def scan_tail_pair(a, b):
        sc = [scan_scores(d, tile) for d, tile, _ in (a, b)]
        for (d, tile, slot), s in zip((a, b), sc):
            scan_out(d, tile, s, slot)

    rid = lax.broadcasted_iota(jnp.int32, (BF16_ROWS, DK), 0)

    def scan_exact(d, tile):
        r0 = tile * tp

        def step(j, st):
            t = r0 + ((tp - 1 - j) if d == 1 else j)
            t0 = _aligned((t // BF16_ROWS) * BF16_ROWS, BF16_ROWS)
            grp = pl.ds(t0, BF16_ROWS)
            sel = rid == (t - t0)
            g = g_scr[d][grp, :]
            f = jnp.exp(jnp.sum(jnp.where(sel, g, 0.0), axis=0, keepdims=True))
            k = jnp.where(sel, 1.0 - jnp.exp(g), 0.0)
            st = st * f + _dot_tn(v_scr[grp, :], k.astype(BF16))
            o = _dot_nt(q_scr[grp, :].astype(BF16), st.astype(BF16))
            o_scr[d][grp, :] = jnp.where(sel, o, o_scr[d][grp, :])
            return st

        state_out(d, tile, lax.fori_loop(0, tp, step, state_in(d)))

    def finish(i):
        rows = pl.ds(_aligned(i * tp, tp), tp)
        o = o_scr[0][rows, :] + o_scr[1][rows, :]
        o = o * lax.rsqrt(jnp.mean(o * o, axis=-1, keepdims=True) + EPS) * gn_ref[...]
        o_ref[0, rows, :] = (o * sg_scr[rows, :]).astype(o_ref.dtype)

    dirs = (0, 1) if independent else (0,)

    def scan_heads(t):
        for d in dirs:
            scan_head(d, t)
        return [scan_scores(d, t) for d in dirs]

    def scan_tails(t, scores):
        for d, sc in zip(dirs, scores):
            scan_out(d, t, sc)
        if independent:
            finish(t)

    last = n_tiles - 1
    gates(0, min(2, n_tiles))
    for i in range(n_tiles):
        scores = scan_heads(i - 1) if i > 0 else None
        sums = decay_sums(i)
        if i % 2 == 1 and i + 1 < n_tiles:
            gates(i + 1, min(2, n_tiles - i - 1))
        factors(i, sums)
        if i > 0:
            scan_tails(i - 1, scores)

    if independent:
        scan_tails(last, scan_heads(last))
    else:
        scan_head(0, last)
        scan_head(1, last)
        scan_tail_pair((0, last, 0), (1, last, 1))
        finish(last)
        t0 = last - 1
        if t0 % 2 == 0:
            scan_head(1, t0, 0)
            scan_tail(1, t0, 0)
            finish(t0)
            t0 -= 1

        for t in range(t0, 0, -2):
            scan_head(1, t, 0)
            scan_head(1, t - 1, 1)
            scan_tail_pair((1, t, 0), (1, t - 1, 1))
            finish(t)
            finish(t - 1)

    unsafe = flag_scr[0]
    for t in range(1, n_tiles):
        unsafe = unsafe | flag_scr[t]

    @pl.when(unsafe != 0)
    def _():
        if not independent:
            init_states()

        def resweep(d):
            def body(j, carry):
                t = j if d == 0 else last - j

                @pl.when(flag_scr[t] != 0)
                def _():
                    scan_exact(d, t)

                @pl.when(flag_scr[t] == 0)
                def _():
                    scan_head(d, t)
                    scan_tail(d, t)

                if d == 1:
                    finish(t)
                return carry

            lax.fori_loop(0, n_tiles, body, 0)

        resweep(0)
        resweep(1)

    if not independent:
        for d in range(2):
            sf_ref[0, 0, d, 0] = st_scr[d].T


def _hgrn_mixer(hn, w_in, lb_raw, g_norm, s0):
    n_seq, seq_len, _ = hn.shape
    has_s0 = s0 is not None
    independent = (not has_s0) and seq_len == SCAN_TILE
    if independent:
        hn = hn.reshape(1, n_seq * seq_len, D)
    b, seq, _ = hn.shape
    st_rows = n_seq if independent else 1
    st_spec = pl.BlockSpec((st_rows, 1, 2, 1, DK, DK), lambda h, i: (i, 0, 0, h, 0, 0))
    in_specs = [pl.BlockSpec((1, seq, D), lambda h, i: (i, 0, 0))]
    in_specs += [pl.BlockSpec((D, DK), functools.partial(lambda j, h, i: (0, j * HEADS + h), j))
                 for j in range(PROJ)]
    in_specs += [pl.BlockSpec((2, 2, DK), lambda h, i: (0, 0, h)),
                 pl.BlockSpec((1, DK), lambda h, i: (0, 0)),
                 pl.BlockSpec((2, SCAN_TILE + BF16_ROWS, SCAN_TILE), lambda h, i: (0, 0, 0))]
    args = [hn] + [w_in] * PROJ + [lb_raw, g_norm, _decay_matrices()]
    if has_s0:
        in_specs.append(st_spec)
        args.append(s0)
    n_tiles = seq // SCAN_TILE
    bf16_rows = pltpu.VMEM((seq, DK), BF16)
    f32_rows = pltpu.VMEM((seq, DK), F32)
    o, s_fin = pl.pallas_call(
        functools.partial(_hgrn_kernel, has_s0=has_s0, independent=independent),
        grid=(HEADS, b),
        in_specs=in_specs,
        out_specs=[pl.BlockSpec((1, seq, DK), lambda h, i: (i, 0, h)), st_spec],
        out_shape=[jax.ShapeDtypeStruct((b, seq, D), BF16),
                   jax.ShapeDtypeStruct((n_seq, 1, 2, HEADS, DK, DK), F32)],
        scratch_shapes=[bf16_rows] * 8 + [f32_rows] * 6 + [
                        pltpu.VMEM((2, n_tiles, BF16_ROWS, DK), F32),
                        f32_rows, bf16_rows, f32_rows,
                        pltpu.VMEM((2, DK, DK), F32),
                        pltpu.VMEM((2, PER_TILE, DK, DK), F32),
                        pltpu.VMEM((2, PER_TILE, DK, DK), BF16),
                        pltpu.VMEM((2, 2, SCAN_TILE, 2 * DK), BF16),
                        pltpu.VMEM((D, PROJ * DK), BF16),
                        pltpu.SMEM((n_tiles,), jnp.int32)],
        compiler_params=_params("parallel", "arbitrary"),
        name="hgrn_scan",
    )(*args)
    return o.reshape(n_seq, seq_len, D), s_fin


def kernel(x_prompt, x_sample, state_hgrn, c, c_ctx, ada_w, ada_b, norm_mix, norm_mlp, fnet_wo,
           hgrn_w_in, hgrn_lb, hgrn_norm, hgrn_wo, mlp_w1, mlp_w2, norm_final):
    bp, lp, _ = x_prompt.shape
    bs, ls, _ = x_sample.shape
    assert ada_w.shape[0] == 2 and ls == SEQ_PTS * SEQ_SLABS and lp == GD

    cs_ch, t_ctx, g_seq = _dft_tables()
    pos = _grid_pos_embed(ls, D)

    cond8 = jnp.zeros((8, D), F32).at[0].set(c_ctx).at[1:1 + bs].set(c)
    mod = _modulation(cond8, ada_w, ada_b)

    wo0 = fnet_wo.astype(BF16)
    wo1 = hgrn_wo.astype(BF16)
    w1 = mlp_w1.astype(BF16)
    w2 = mlp_w2.astype(BF16)
    w_heads = hgrn_w_in[0]
    gm = norm_mix.reshape(-1, 1, D)
    gf = norm_mlp.reshape(-1, 1, D)
    g_fin = norm_final.reshape(1, D)
    g_hn = hgrn_norm[0].reshape(1, DK)

    ctx_row = lambda r: 0
    smp_row = lambda r: 1 + r // ls

    mixed = _fnet_ctx(x_prompt, mod, 0, gm[0], cs_ch, t_ctx)
    x2, hn = _post_mixer(x_prompt.reshape(bp * lp, D), None, mixed.reshape(bp * lp, D), mod, 0,
                         ctx_row, wo0, w1, w2, gf[0], gm[1], final=False)
    o, new_state = _hgrn_mixer(hn.reshape(bp, lp, D), w_heads, hgrn_lb, g_hn, None)
    y_prompt = _post_mixer(x2, None, o.reshape(bp * lp, D), mod, 1, ctx_row, wo1, w1, w2,
                           gf[1], g_fin, final=True).reshape(bp, lp, D)

    mixed = _fnet_sample(x_sample, pos, mod, 0, gm[0], cs_ch, g_seq)
    x2, hn = _post_mixer(x_sample.reshape(bs * ls, D), pos, mixed.reshape(bs * ls, D), mod, 0,
                         smp_row, wo0, w1, w2, gf[0], gm[1], final=False)
    o, _ = _hgrn_mixer(hn.reshape(bs, ls, D), w_heads, hgrn_lb, g_hn, state_hgrn)
    y_sample = _post_mixer(x2, None, o.reshape(bs * ls, D), mod, 1, smp_row, wo1, w1, w2,
                           gf[1], g_fin, final=True).reshape(bs, ls, D)
    return (y_prompt, y_sample, new_state)
```
